```python
import math
import jax, jax.numpy as jnp
from jax import lax
import numpy as np

D_MODEL = 1024
BATCH = 8
SEQ = 4096
DEPTH = 1
DEC_BATCH = 128
DEC_SEQ = 8
PAST_LEN = 8192
PAGE_SIZE = 128

MIX_W = D_MODEL
SGU_W = MIX_W // 2
SGU_GROUPS = 4
SGU_GW = SGU_W // SGU_GROUPS
CHUNK = 128
NSA_W = MIX_W - SGU_W
HEAD_DIM = 64
N_HEADS = NSA_W // HEAD_DIM
N_KV = 2
GQA_R = N_HEADS // N_KV
CMP_LEN = 32
CMP_STRIDE = 16
SEL_BLOCK = 64
N_SELECT = 16
WINDOW = 512
QBLK = 64
KV_W = 2 * N_KV * HEAD_DIM
N_BUCKETS = 32
MAX_EXACT = N_BUCKETS // 2
MAX_DIST = 128
MEM_LEN = 256
MEM_HEADS = 4
MEM_HD = 128
N_EXPERTS = 32
TOP_K = 4
D_FF = D_MODEL
SWIGLU_LIMIT = 7.0
SWIGLU_ALPHA = 1.702
EXPERT_BLOCK = 128

EPS = 1e-6
NEG = -1e30
FORCE = 1e9

kernel_name = 'hymba_sgu_nsa_moe_decode_step'


def rmsnorm(x, g):
    xf = x.astype(jnp.float32)
    y = xf * lax.rsqrt(jnp.mean(xf * xf, axis=-1, keepdims=True) + EPS)
    return (y * g.astype(jnp.float32)).astype(x.dtype)


def layernorm(x, g, b):
    xf = x.astype(jnp.float32)
    mu = jnp.mean(xf, axis=-1, keepdims=True)
    var = jnp.mean(jnp.square(xf - mu), axis=-1, keepdims=True)
    y = (xf - mu) * lax.rsqrt(var + EPS)
    return (y * g.astype(jnp.float32) + b.astype(jnp.float32)).astype(x.dtype)


def rel_bucket(dist):
    n = jnp.maximum(dist, 0)
    nf = jnp.maximum(n, 1).astype(jnp.float32)
    large = MAX_EXACT + (jnp.log(nf / MAX_EXACT) / math.log(MAX_DIST / MAX_EXACT)
                         * (N_BUCKETS - MAX_EXACT)).astype(jnp.int32)
    large = jnp.minimum(large, N_BUCKETS - 1)
    return jnp.where(n < MAX_EXACT, n, large)


def masked_softmax(s, valid):
    p = jax.nn.softmax(jnp.where(valid, s, NEG), axis=-1)
    return jnp.where(valid, p, 0.0)


def sgu_mixer(u, v, norm_g, norm_b, sgu_w, sgu_b):
    bsz, t_len, _ = u.shape
    c = min(t_len, CHUNK)
    nc = t_len // c
    u = jax.nn.gelu(u)
    v = layernorm(jax.nn.gelu(v), norm_g, norm_b)
    w = jnp.where(jnp.tril(jnp.ones((c, c), bool)), sgu_w[:, :c, :c], 0.0).astype(v.dtype)
    vc = v.reshape(bsz, nc, c, SGU_GROUPS, SGU_GW)
    mixed = jnp.einsum('gst,bntgc->bnsgc', w, vc) + sgu_b[:, :c].T[None, None, :, :, None]
    return u * mixed.reshape(bsz, t_len, SGU_W), v


def nsa_mixer(q, kv_c, kv_s, kv_w, gates, pos0, w0, cmp_w, cmp_b, rel_bias):
    bsz, t_len = q.shape[:2]
    seq_len = kv_c.shape[1]
    dt = q.dtype
    f32 = jnp.float32
    q = q.reshape(bsz, t_len, N_KV, GQA_R, HEAD_DIM)
    tbl = rel_bias.T.reshape(N_KV, GQA_R, N_BUCKETS).astype(f32)
    scale = HEAD_DIM ** -0.5

    n_half = -(-seq_len // CMP_STRIDE)
    kvc = jnp.pad(kv_c, ((0, 0), (0, n_half * CMP_STRIDE - seq_len), (0, 0), (0, 0), (0, 0)))
    kvc = kvc.reshape(bsz, n_half, CMP_STRIDE, 2, N_KV, HEAD_DIM)
    lo = jnp.einsum('bnrcgd,cgrde->bncge', kvc, cmp_w[:, :, :CMP_STRIDE])
    hi = jnp.einsum('bnrcgd,cgrde->bncge', kvc, cmp_w[:, :, CMP_STRIDE:])
    comp = lo[:, :-1] + hi[:, 1:] + cmp_b
    k_cmp, v_cmp = comp[:, :, 0], comp[:, :, 1]
    n_cmp = n_half - 1
    cmp_start = jnp.arange(n_cmp) * CMP_STRIDE
    cmp_end = cmp_start + CMP_LEN - 1

    n_sel = -(-seq_len // SEL_BLOCK)
    kvs = jnp.pad(kv_s, ((0, 0), (0, n_sel * SEL_BLOCK - seq_len), (0, 0), (0, 0), (0, 0)))
    kvs = kvs.reshape(bsz, n_sel, SEL_BLOCK, 2, N_KV, HEAD_DIM).transpose(0, 4, 1, 2, 3, 5)
    sel_start = jnp.arange(n_sel) * SEL_BLOCK
    overlap = ((cmp_start[:, None] < sel_start[None, :] + SEL_BLOCK)
               & (cmp_start[:, None] + CMP_LEN > sel_start[None, :])).astype(f32)
    k_sel = min(N_SELECT, n_sel)

    kvw = jnp.pad(kv_w, ((0, 0), (WINDOW, 0), (0, 0), (0, 0), (0, 0)))

    qb = min(t_len, QBLK)
    nb = t_len // qb
    q_blocks = jnp.moveaxis(q.reshape(bsz, nb, qb, N_KV, GQA_R, HEAD_DIM), 1, 0)
    g_blocks = jnp.moveaxis(jax.nn.sigmoid(gates.astype(f32)).astype(dt)
                            .reshape(bsz, nb, qb, 3, N_KV, GQA_R), 1, 0)
    starts = jnp.arange(nb, dtype=jnp.int32) * qb
    b_idx = jnp.arange(bsz)[:, None, None, None]
    g_idx = jnp.arange(N_KV)[None, :, None, None]
    gi = jnp.arange(N_KV)[None, :, None, None, None]
    ri = jnp.arange(GQA_R)[None, None, :, None, None]
    j_sel = jnp.arange(n_sel)

    def one_block(args):
        qblk, gblk, start = args
        t = pos0 + start + jnp.arange(qb)
        dist_c = t[:, None] - cmp_end[None, :]
        valid_c = dist_c >= 0
        s_c = jnp.einsum('btgrd,bngd->bgrtn', qblk, k_cmp, preferred_element_type=f32) * scale
        s_c = s_c + tbl[:, :, rel_bucket(dist_c)]
        p_c = masked_softmax(s_c, valid_c)
        o_c = jnp.einsum('bgrtn,bngd->btgrd', p_c.astype(dt), v_cmp)
        imp = jnp.einsum('bgrtn,nj->bgtj', p_c, overlap)
        cur = t // SEL_BLOCK
        forced = (j_sel[None] == 0) | (j_sel[None] == cur[:, None]) | (j_sel[None] == cur[:, None] - 1)
        future = j_sel[None] * SEL_BLOCK > t[:, None]
        imp = jnp.where(forced, FORCE, imp)
        imp = jnp.where(future, NEG, imp)
        top_s, idx = lax.top_k(imp, k_sel)
        blk_ok = top_s > NEG * 0.5
        sel = kvs[b_idx, g_idx, idx]
        sel = sel.reshape(bsz, N_KV, qb, k_sel * SEL_BLOCK, 2, HEAD_DIM)
        pos_s = (idx[..., None] * SEL_BLOCK + jnp.arange(SEL_BLOCK)).reshape(bsz, N_KV, qb, k_sel * SEL_BLOCK)
        tq = t[None, None, :, None]
        valid_s = jnp.repeat(blk_ok, SEL_BLOCK, axis=-1) & (pos_s <= tq)
        bias_s = tbl[gi, ri, rel_bucket(tq - pos_s)[:, :, None]]
        s_s = jnp.einsum('btgrd,bgtsd->bgrts', qblk, sel[..., 0, :], preferred_element_type=f32) * scale + bias_s
        p_s = masked_softmax(s_s, valid_s[:, :, None])
        o_s = jnp.einsum('bgrts,bgtsd->btgrd', p_s.astype(dt), sel[..., 1, :])
        kw = lax.dynamic_slice_in_dim(kvw, pos0 + start - w0, WINDOW + qb, axis=1)
        pos_w = pos0 + start - WINDOW + jnp.arange(WINDOW + qb)
        dist_w = t[:, None] - pos_w[None, :]
        valid_w = (dist_w >= 0) & (dist_w < WINDOW) & (pos_w[None, :] >= 0)
        s_w = jnp.einsum('btgrd,bsgd->bgrts', qblk, kw[:, :, 0], preferred_element_type=f32) * scale
        s_w = s_w + tbl[:, :, rel_bucket(dist_w)]
        p_w = masked_softmax(s_w, valid_w)
        o_w = jnp.einsum('bgrts,bsgd->btgrd', p_w.astype(dt), kw[:, :, 1])
        return (gblk[:, :, 0, ..., None] * o_c + gblk[:, :, 1, ..., None] * o_s
                + gblk[:, :, 2, ..., None] * o_w)

    out = lax.map(one_block, (q_blocks, g_blocks, starts))
    return jnp.moveaxis(out, 0, 1).reshape(bsz, t_len, NSA_W)


def memory_kv(mem, norm_mem, w_mem_kv):
    bsz, m_len, _ = mem.shape
    return (rmsnorm(mem, norm_mem) @ w_mem_kv).reshape(bsz, m_len, 2, MEM_HEADS, MEM_HD)


def cross_attention(x, mem_kv, norm_x, w_xq, w_xo):
    bsz, t_len, _ = x.shape
    q = (rmsnorm(x, norm_x) @ w_xq).reshape(bsz, t_len, MEM_HEADS, MEM_HD)
    s = jnp.einsum('bthd,bmhd->bhtm', q, mem_kv[:, :, 0], preferred_element_type=jnp.float32) * MEM_HD ** -0.5
    p = jax.nn.softmax(s, axis=-1).astype(x.dtype)
    o = jnp.einsum('bhtm,bmhd->bthd', p, mem_kv[:, :, 1]).reshape(bsz, t_len, MEM_HEADS * MEM_HD)
    return o @ w_xo


def moe(x2d, w_router, b_router, w_gu, b_gu, w_down, b_down):
    n_tok = x2d.shape[0]
    logits = jnp.dot(x2d, w_router, preferred_element_type=jnp.float32) + b_router.astype(jnp.float32)
    top_l, top_e = lax.top_k(logits, TOP_K)
    gate = jax.nn.softmax(top_l, axis=-1).astype(x2d.dtype)
    nk = n_tok * TOP_K
    flat_e = top_e.reshape(nk)
    order = jnp.argsort(flat_e)
    sorted_e = flat_e[order]
    tok = order // TOP_K
    counts = jnp.bincount(flat_e, length=N_EXPERTS)
    padded = ((counts + EXPERT_BLOCK - 1) // EXPERT_BLOCK) * EXPERT_BLOCK
    start = jnp.cumsum(counts) - counts
    pend = jnp.cumsum(padded)
    pstart = pend - padded
    dest = pstart[sorted_e] + (jnp.arange(nk) - start[sorted_e])
    n_blocks = -(-(nk + N_EXPERTS * (EXPERT_BLOCK - 1)) // EXPERT_BLOCK)
    buf = jnp.zeros((n_blocks * EXPERT_BLOCK, x2d.shape[1]), x2d.dtype).at[dest].set(x2d[tok])
    blk_e = jnp.minimum(jnp.searchsorted(pend, jnp.arange(n_blocks) * EXPERT_BLOCK, side='right'),
                        N_EXPERTS - 1)

    def expert_block(args):
        xb, e = args
        gu = xb @ w_gu[e] + b_gu[e]
        glu, lin = jnp.split(gu, 2, axis=-1)
        glu = jnp.minimum(glu, SWIGLU_LIMIT)
        lin = jnp.clip(lin, -SWIGLU_LIMIT, SWIGLU_LIMIT)
        hdn = glu * jax.nn.sigmoid(SWIGLU_ALPHA * glu) * (lin + 1.0)
        return hdn @ w_down[e] + b_down[e]

    y = lax.map(expert_block, (buf.reshape(n_blocks, EXPERT_BLOCK, -1), blk_e)).reshape(n_blocks * EXPERT_BLOCK, -1)
    contrib = y[dest] * gate.reshape(nk)[order][:, None]
    return jnp.zeros_like(x2d).at[tok].add(contrib)


def trunk_layer(x, pos0, past_c, past_s, past_w, mem_kv, p, rel_bias):
    bsz, t_len, _ = x.shape
    h = rmsnorm(x, p['norm_mix'])
    proj = h @ p['w_in']
    cuts = np.cumsum([SGU_W, SGU_W, NSA_W, KV_W, KV_W, KV_W]).tolist()
    u, v, q, kc, ks, kw, gt = jnp.split(proj, cuts, axis=-1)
    a_out, v_rows = sgu_mixer(u, v, p['sgu_norm_g'], p['sgu_norm_b'], p['sgu_w'], p['sgu_b'])
    kv_shape = (bsz, t_len, 2, N_KV, HEAD_DIM)
    kc, ks, kw = kc.reshape(kv_shape), ks.reshape(kv_shape), kw.reshape(kv_shape)
    kv_c = kc if past_c is None else jnp.concatenate([past_c, kc], axis=1)
    kv_s = ks if past_s is None else jnp.concatenate([past_s, ks], axis=1)
    kv_w = kw if past_w is None else jnp.concatenate([past_w, kw], axis=1)
    w0 = pos0 - (kv_w.shape[1] - t_len)
    b_out = nsa_mixer(q.reshape(bsz, t_len, N_HEADS, HEAD_DIM), kv_c, kv_s, kv_w,
                      gt.reshape(bsz, t_len, 3, N_HEADS), pos0, w0, p['cmp_w'], p['cmp_b'], rel_bias)
    x = x + jnp.concatenate([a_out, b_out], axis=-1) @ p['w_o']
    x = x + cross_attention(x, mem_kv, p['norm_x'], p['w_xq'], p['w_xo'])
    h = rmsnorm(x, p['norm_ffn'])
    x = x + moe(h.reshape(bsz * t_len, D_MODEL), p['w_router'], p['b_router'], p['w_gu'],
                p['b_gu'], p['w_down'], p['b_down']).reshape(x.shape)
    return x, kc, ks, kv_w, v_rows


def setup_inputs(seed: int = 0) -> dict:
    key = jax.random.key(seed)
    ks = jax.random.split(key, 32)
    f32 = jnp.float32

    def nrm(k, shape, scale):
        return jax.random.normal(k, shape, f32) * scale

    n_pages = PAST_LEN // PAGE_SIZE
    n_phys = (DEC_BATCH * n_pages * 5) // 4
    wb = min(WINDOW, PAST_LEN)
    in_cols = 2 * SGU_W + NSA_W + 3 * KV_W + 3 * N_HEADS
    page_table = jax.random.permutation(ks[7], n_phys)[: DEC_BATCH * n_pages]
    page_table = page_table.reshape(DEC_BATCH, n_pages).astype(jnp.int32)
    return {
        'x_prompt': nrm(ks[0], (BATCH, SEQ, D_MODEL), 1.0),
        'x_sample': nrm(ks[1], (DEC_BATCH, DEC_SEQ, D_MODEL), 1.0),
        'mem_prompt': nrm(ks[2], (BATCH, MEM_LEN, D_MODEL), 1.0),
        'cache_cmp_kv': nrm(ks[3], (DEPTH, n_phys, PAGE_SIZE, 2, N_KV, HEAD_DIM), 1.0),
        'cache_slc_kv': nrm(ks[4], (DEPTH, n_phys, PAGE_SIZE, 2, N_KV, HEAD_DIM), 1.0),
        'cache_win_kv': nrm(ks[5], (DEPTH, DEC_BATCH, wb, 2, N_KV, HEAD_DIM), 1.0),
        'cache_mem_kv': nrm(ks[6], (DEPTH, DEC_BATCH, MEM_LEN, 2, MEM_HEADS, MEM_HD), 1.0),
        'page_table': page_table,
        'norm_mix': 1.0 + nrm(ks[8], (DEPTH, D_MODEL), 0.05),
        'w_in': nrm(ks[9], (DEPTH, D_MODEL, in_cols), D_MODEL ** -0.5),
        'sgu_norm_g': 1.0 + nrm(ks[10], (DEPTH, SGU_W), 0.05),
        'sgu_norm_b': nrm(ks[11], (DEPTH, SGU_W), 0.02),
        'sgu_w': nrm(ks[12], (DEPTH, SGU_GROUPS, CHUNK, CHUNK), CHUNK ** -0.5),
        'sgu_b': 1.0 + nrm(ks[13], (DEPTH, SGU_GROUPS, CHUNK), 0.1),
        'cmp_w': nrm(ks[14], (DEPTH, 2, N_KV, CMP_LEN, HEAD_DIM, HEAD_DIM), (CMP_LEN * HEAD_DIM) ** -0.5),
        'cmp_b': nrm(ks[15], (DEPTH, 2, N_KV, HEAD_DIM), 0.02),
        'w_o': nrm(ks[16], (DEPTH, MIX_W, D_MODEL), MIX_W ** -0.5),
        'rel_bias': nrm(ks[17], (N_BUCKETS, N_HEADS), 0.1),
        'norm_x': 1.0 + nrm(ks[18], (DEPTH, D_MODEL), 0.05),
        'norm_mem': 1.0 + nrm(ks[19], (DEPTH, D_MODEL), 0.05),
        'w_xq': nrm(ks[20], (DEPTH, D_MODEL, MEM_HEADS * MEM_HD), D_MODEL ** -0.5),
        'w_mem_kv': nrm(ks[21], (DEPTH, D_MODEL, 2 * MEM_HEADS * MEM_HD), D_MODEL ** -0.5),
        'w_xo': nrm(ks[22], (DEPTH, MEM_HEADS * MEM_HD, D_MODEL), (MEM_HEADS * MEM_HD) ** -0.5),
        'norm_ffn': 1.0 + nrm(ks[23], (DEPTH, D_MODEL), 0.05),
        'w_router': nrm(ks[24], (DEPTH, D_MODEL, N_EXPERTS), D_MODEL ** -0.5),
        'b_router': nrm(ks[25], (DEPTH, N_EXPERTS), 0.01),
        'w_gu': nrm(ks[26], (DEPTH, N_EXPERTS, D_MODEL, 2 * D_FF), D_MODEL ** -0.5),
        'b_gu': nrm(ks[27], (DEPTH, N_EXPERTS, 2 * D_FF), 0.01),
        'w_down': nrm(ks[28], (DEPTH, N_EXPERTS, D_FF, D_MODEL), D_FF ** -0.5),
        'b_down': nrm(ks[29], (DEPTH, N_EXPERTS, D_MODEL), 0.01),
        'final_norm': 1.0 + nrm(ks[30], (D_MODEL,), 0.05),
    }


def reference(x_prompt, x_sample, mem_prompt, cache_cmp_kv, cache_slc_kv, cache_win_kv, cache_mem_kv,
              page_table, norm_mix, w_in, sgu_norm_g, sgu_norm_b, sgu_w, sgu_b, cmp_w, cmp_b, w_o,
              rel_bias, norm_x, norm_mem, w_xq, w_mem_kv, w_xo, norm_ffn, w_router, b_router,
              w_gu, b_gu, w_down, b_down, final_norm):
    db = x_sample.shape[0]
    past = page_table.shape[1] * cache_cmp_kv.shape[2]
    t_p = x_prompt.shape[1]
    wb = cache_win_kv.shape[2]
    xp, xs = x_prompt, x_sample
    c_p, s_p, w_p, m_p, v_p = [], [], [], [], []
    c_s, s_s, w_s, v_s = [], [], [], []
    for l in range(DEPTH):
        p = dict(norm_mix=norm_mix[l], w_in=w_in[l], sgu_norm_g=sgu_norm_g[l], sgu_norm_b=sgu_norm_b[l],
                 sgu_w=sgu_w[l], sgu_b=sgu_b[l], cmp_w=cmp_w[l], cmp_b=cmp_b[l], w_o=w_o[l],
                 norm_x=norm_x[l], w_xq=w_xq[l], w_xo=w_xo[l], norm_ffn=norm_ffn[l],
                 w_router=w_router[l], b_router=b_router[l], w_gu=w_gu[l], b_gu=b_gu[l],
                 w_down=w_down[l], b_down=b_down[l])
        mem_kv_p = memory_kv(mem_prompt, norm_mem[l], w_mem_kv[l])
        xp, kc, ks, kw, vr = trunk_layer(xp, 0, None, None, None, mem_kv_p, p, rel_bias)
        c_p.append(kc)
        s_p.append(ks)
        w_p.append(kw[:, -min(WINDOW, t_p):])
        m_p.append(mem_kv_p)
        v_p.append(vr[:, -min(CHUNK, t_p):])
        past_c = cache_cmp_kv[l][page_table].reshape(db, past, 2, N_KV, HEAD_DIM)
        past_s = cache_slc_kv[l][page_table].reshape(db, past, 2, N_KV, HEAD_DIM)
        xs, kc, ks, kw, vr = trunk_layer(xs, past, past_c, past_s, cache_win_kv[l], cache_mem_kv[l], p, rel_bias)
        c_s.append(kc)
        s_s.append(ks)
        w_s.append(kw[:, -wb:])
        v_s.append(vr)
    y_prompt = rmsnorm(xp, final_norm)
    y_sample = rmsnorm(xs, final_norm)
    return (y_prompt, y_sample, jnp.stack(c_p), jnp.stack(s_p), jnp.stack(w_p), jnp.stack(m_p),
            jnp.stack(v_p), jnp.stack(c_s), jnp.stack(s_s), jnp.stack(w_s), jnp.stack(v_s))
```

```python
import functools
import math

import numpy as np
import jax
import jax.numpy as jnp
from jax import lax
from jax.experimental import pallas as pl
from jax.experimental.pallas import tpu as pltpu

D_MODEL = 1024
SGU_W = 512
SGU_GROUPS = 4
SGU_GW = 128
CHUNK = 128
NSA_W = 512
HEAD_DIM = 64
N_HEADS = 8
N_KV = 2
GQA_R = 4
CMP_LEN = 32
CMP_STRIDE = 16
SEL_BLOCK = 64
N_SELECT = 16
WINDOW = 512
KV_W = 256
N_BUCKETS = 32
MAX_EXACT = 16
MAX_DIST = 128
MEM_LEN = 256
MEM_HEADS = 4
MEM_HD = 128
N_EXPERTS = 32
TOP_K = 4
D_FF = 1024
SWIGLU_LIMIT = 7.0
SWIGLU_ALPHA = 1.702
EPS = 1e-6
NEG = -1e30
FORCE = 1e9

LANES = 128
VMEM_LIMIT = 56 * 1024 * 1024

ROW_TILE = 512
QT = 256
KT = 256

F32 = jnp.float32
BF16 = jnp.bfloat16


def _cparams(*sem):
    return pltpu.CompilerParams(dimension_semantics=sem, vmem_limit_bytes=VMEM_LIMIT)


def _const_spec(shape):
    nd = len(shape)
    return pl.BlockSpec(shape, lambda *_: (0,) * nd)


def _rms(x, g):
    return x * lax.rsqrt(jnp.mean(x * x, axis=-1, keepdims=True) + EPS) * g


def _rel_bucket_np(dist):
    n = np.maximum(dist, 0)
    nf = np.maximum(n, 1).astype(np.float32)
    large = MAX_EXACT + (np.log(nf / np.float32(MAX_EXACT)) / np.float32(math.log(MAX_DIST / MAX_EXACT))
                         * np.float32(N_BUCKETS - MAX_EXACT)).astype(np.int32)
    large = np.minimum(large, N_BUCKETS - 1)
    return np.where(n < MAX_EXACT, n, large).astype(np.int32)


C_U, C_V, C_Q, C_KC, C_GT, C_END = 0, 512, 1024, 2048, 2816, 2944


def _inproj_kernel(x_ref, g_ref, w_ref, lng_ref, lnb_ref, mix_ref, mixb_ref, *out_refs, mix_block, attn_extras):
    a_ref, v_ref, q_ref, gt_ref = out_refs[:4]
    kv_refs = out_refs[4:10]
    x = x_ref[...]
    h = _rms(x, g_ref[...]).astype(BF16)

    def proj(lo, hi):
        return jnp.dot(h, w_ref[:, lo:hi], preferred_element_type=F32)

    q_ref[...] = proj(C_Q, C_KC).astype(BF16)
    gt_ref[...] = proj(C_GT, C_END)
    kv = [proj(C_KC + LANES * j, C_KC + LANES * (j + 1)) for j in range(6)]
    for r, val in zip(kv_refs, kv):
        r[...] = val
    if attn_extras:
        kskb_ref, ksvt_ref, kwkb_ref, kwvt_ref = out_refs[10:14]
        kskb_ref[...] = kv[2].astype(BF16)
        kwkb_ref[...] = kv[4].astype(BF16)
        for vt_ref, val in ((ksvt_ref, kv[3]), (kwvt_ref, kv[5])):
            vt = val.T.astype(BF16)
            for j in range(val.shape[0] // KT):
                vt_ref[j] = vt[:, j * KT:(j + 1) * KT]

    u = jax.nn.gelu(proj(C_U, C_V))
    v = jax.nn.gelu(proj(C_V, C_Q))
    mu = jnp.mean(v, axis=-1, keepdims=True)
    var = jnp.mean(jnp.square(v - mu), axis=-1, keepdims=True)
    v = (v - mu) * lax.rsqrt(var + EPS) * lng_ref[...] + lnb_ref[...]
    v_ref[...] = v
    vb = v.astype(BF16)
    rows = x.shape[0]
    for blk in range(rows // mix_block):
        r0 = blk * mix_block
        for g in range(SGU_GROUPS):
            c0 = g * SGU_GW
            mixed = jnp.dot(mix_ref[g], vb[r0:r0 + mix_block, c0:c0 + SGU_GW],
                            preferred_element_type=F32) + mixb_ref[:, c0:c0 + SGU_GW]
            a_ref[r0:r0 + mix_block, c0:c0 + SGU_GW] = (
                u[r0:r0 + mix_block, c0:c0 + SGU_GW] * mixed).astype(BF16)


def _inproj(x2d, g, w_all, ln_g, ln_b, mix, mixb, attn_extras):
    n = x2d.shape[0]
    mb = mix.shape[1]
    row = lambda c: pl.BlockSpec((ROW_TILE, c), lambda i: (i, 0))
    outs = [(SGU_W, BF16), (SGU_W, F32), (N_HEADS * LANES, BF16), (LANES, F32)] + [(LANES, F32)] * 6
    out_specs = [row(c) for c, _ in outs]
    out_shape = [jax.ShapeDtypeStruct((n, c), dt) for c, dt in outs]
    if attn_extras:
        tiles = ROW_TILE // KT
        vt_spec = pl.BlockSpec((tiles, LANES, KT), lambda i: (i, 0, 0))
        vt_shape = jax.ShapeDtypeStruct((n // KT, LANES, KT), BF16)
        out_specs += [row(LANES), vt_spec, row(LANES), vt_spec]
        out_shape += [jax.ShapeDtypeStruct((n, LANES), BF16), vt_shape,
                      jax.ShapeDtypeStruct((n, LANES), BF16), vt_shape]
    return pl.pallas_call(
        functools.partial(_inproj_kernel, mix_block=mb, attn_extras=attn_extras),
        grid=(n // ROW_TILE,),
        in_specs=[row(D_MODEL), _const_spec((1, D_MODEL)), _const_spec(w_all.shape),
                  _const_spec((1, SGU_W)), _const_spec((1, SGU_W)), _const_spec(mix.shape),
                  _const_spec(mixb.shape)],
        out_specs=out_specs,
        out_shape=out_shape,
        compiler_params=_cparams("parallel"),
        name="inproj_sgu",
    )(x2d, g, w_all, ln_g, ln_b, mix, mixb)


def _build_w_in(w_in):
    u = w_in[:, 0:512]
    v = w_in[:, 512:1024]
    q = w_in[:, 1024:1536].reshape(D_MODEL, N_KV, GQA_R, HEAD_DIM) * (HEAD_DIM ** -0.5)
    qp = jnp.zeros((D_MODEL, N_KV, GQA_R, N_KV, HEAD_DIM), F32)
    for g in range(N_KV):
        qp = qp.at[:, g, :, g, :].set(q[:, g])
    qp = qp.reshape(D_MODEL, N_HEADS * LANES)
    kv = w_in[:, 1536:2304]
    gt = jnp.pad(w_in[:, 2304:2328], ((0, 0), (0, LANES - 3 * N_HEADS)))
    return jnp.concatenate([u, v, qp, kv, gt], axis=1).astype(BF16)


def _build_sgu_mix(sgu_w, sgu_b, chunk, mix_block):
    tri = jnp.tril(jnp.ones((chunk, chunk), bool))
    w = jnp.where(tri, sgu_w[:, :chunk, :chunk], 0.0)
    reps = mix_block // chunk
    eye = jnp.eye(reps, dtype=F32)
    mix = jnp.einsum('ab,gst->gasbt', eye, w).reshape(SGU_GROUPS, mix_block, mix_block)
    b = jnp.tile(sgu_b[:, :chunk], (1, reps))
    mixb = jnp.repeat(b.T, SGU_GW, axis=1)
    return mix.astype(BF16), mixb


def _build_cmp_w(cmp_w):
    w = cmp_w.reshape(2, N_KV, 2, CMP_STRIDE, HEAD_DIM, HEAD_DIM)
    out = jnp.zeros((2, CMP_STRIDE, N_KV, HEAD_DIM, 2, N_KV, HEAD_DIM), F32)
    for g in range(N_KV):
        out = out.at[:, :, g, :, :, g, :].set(jnp.transpose(w[:, g], (0, 2, 3, 1, 4)))
    return out.reshape(2, CMP_STRIDE, LANES, 2 * LANES).astype(BF16)


def _compress_kernel(kk_ref, kv_ref, w_ref, b_ref, ck_ref, cvt_ref, *, n_half):
    for c, src in enumerate((kk_ref, kv_ref)):
        acc = jnp.zeros((n_half, 2 * LANES), F32)
        for r in range(CMP_STRIDE):
            xr = src[pl.ds(r, n_half, stride=CMP_STRIDE), :].astype(BF16)
            acc = acc + jnp.dot(xr, w_ref[c, r], preferred_element_type=F32)
        lo = acc[:, :LANES]
        hi_next = pltpu.roll(acc[:, LANES:], n_half - 1, 0)
        comp = lo + hi_next + b_ref[c:c + 1, :]
        if c == 0:
            ck_ref[0] = comp.astype(BF16)
        else:
            cvt_ref[0] = comp.T.astype(BF16)


def _compress_prompt(kck, kcv, w_cmp, b_cmp, bsz, t_len):
    n_half = t_len // CMP_STRIDE
    return pl.pallas_call(
        functools.partial(_compress_kernel, n_half=n_half),
        grid=(bsz,),
        in_specs=[pl.BlockSpec((t_len, LANES), lambda b: (b, 0)), pl.BlockSpec((t_len, LANES), lambda b: (b, 0)),
                  _const_spec(w_cmp.shape), _const_spec(b_cmp.shape)],
        out_specs=[pl.BlockSpec((1, n_half, LANES), lambda b: (b, 0, 0)),
                   pl.BlockSpec((1, LANES, n_half), lambda b: (b, 0, 0))],
        out_shape=[jax.ShapeDtypeStruct((bsz, n_half, LANES), BF16),
                   jax.ShapeDtypeStruct((bsz, LANES, n_half), BF16)],
        compiler_params=_cparams("parallel"),
        name="compress_prompt",
    )(kck, kcv, w_cmp, b_cmp)


def _attn_tables(rel_bias, t_len):
    n_half = t_len // CMP_STRIDE
    n_sel = t_len // SEL_BLOCK
    y = np.arange(QT)[None, :]
    x = np.arange(KT)[:, None]
    dists = np.stack([y - x, y - x + KT, y - x + 2 * KT])
    valid = np.stack([(y - x) >= 0, np.ones((KT, QT), bool), x > y])
    near = jnp.where(valid[..., None], rel_bias[_rel_bucket_np(dists)], NEG)
    near = near.reshape(3, KT, QT, N_KV, GQA_R).transpose(0, 3, 1, 4, 2).reshape(3, N_KV, KT, GQA_R * QT)
    far = jnp.repeat(rel_bias[N_BUCKETS - 1].reshape(N_KV, 1, GQA_R), QT, axis=2).reshape(N_KV, 1, GQA_R * QT)
    t = np.arange(t_len).reshape(t_len // QT, 1, QT)
    dc = t - (np.arange(n_half)[None, :, None] * CMP_STRIDE + CMP_LEN - 1)
    biasc = jnp.where((dc >= 0)[..., None], rel_bias[_rel_bucket_np(dc)], NEG)
    biasc = biasc.reshape(-1, n_half, QT, N_KV, GQA_R).transpose(0, 3, 1, 4, 2).reshape(-1, N_KV, n_half, GQA_R * QT)
    cs = np.arange(n_half - 1)[None, :] * CMP_STRIDE
    ss = np.arange(n_sel)[:, None] * SEL_BLOCK
    ov = np.zeros((n_sel, n_half), np.float32)
    ov[:, :n_half - 1] = (cs < ss + SEL_BLOCK) & (cs + CMP_LEN > ss)
    ovt4 = jnp.asarray(np.tile(ov, (1, GQA_R)), BF16)
    return near, far, biasc, ovt4


def _prompt_attn_kernel(q_ref, gt_ref, ck_ref, cvt_ref, ksk_ref, ksvt_ref, kwk_ref, kwvt_ref,
                        biasc_ref, near_ref, far_ref, ovt_ref, o_ref,
                        imp_ref, selneg_ref, m_ref, l_ref, acc_ref, outt_ref, *, n_sel):
    i = pl.program_id(1)
    nt = (((1,), (1,)), ((), ()))
    gsig = jax.nn.sigmoid(gt_ref[0]).T
    jj = lax.broadcasted_iota(jnp.int32, (n_sel, QT), 0)
    tt = i * QT + lax.broadcasted_iota(jnp.int32, (n_sel, QT), 1)
    cur = lax.shift_right_logical(tt, 6)
    forced = (jj == 0) | (jj == cur) | (jj == cur - 1)
    future = jj > cur

    def online_step(qs, k_ref, vt_ref, kt, bias, first):
        kk = k_ref[pl.ds(pl.multiple_of(kt * KT, KT), KT), :]
        s = lax.dot_general(kk, qs, nt, preferred_element_type=F32) + bias
        mt = jnp.max(s, axis=0, keepdims=True)
        if first:
            m_new = mt
        else:
            m_old = m_ref[...]
            m_new = jnp.maximum(m_old, mt)
            alpha = jnp.exp(m_old - m_new)
        p = jnp.exp(s - m_new)
        pv = jnp.dot(vt_ref[kt], p.astype(BF16), preferred_element_type=F32)
        ps = jnp.sum(p, axis=0, keepdims=True)
        if first:
            l_ref[...] = ps
            acc_ref[...] = pv
        else:
            l_ref[...] = alpha * l_ref[...] + ps
            acc_ref[...] = alpha * acc_ref[...] + pv
        m_ref[...] = m_new

    def sel_mask(kt):
        rows = [jnp.broadcast_to(selneg_ref[pl.ds(kt * (KT // SEL_BLOCK) + b, 1), :], (SEL_BLOCK, QT))
                for b in range(KT // SEL_BLOCK)]
        mk = jnp.concatenate(rows, axis=0)
        return jnp.concatenate([mk] * GQA_R, axis=1)

    for g in range(N_KV):
        qs = jnp.concatenate([q_ref[0, :, (g * GQA_R + r) * LANES:(g * GQA_R + r + 1) * LANES]
                              for r in range(GQA_R)], axis=0)
        bc = biasc_ref[0, g]
        sc = lax.dot_general(ck_ref[0], qs, nt, preferred_element_type=F32) + bc
        e = jnp.exp(sc - jnp.max(sc, axis=0, keepdims=True))
        e = jnp.where(bc > 0.5 * NEG, e, 0.0)
        lsum = jnp.sum(e, axis=0, keepdims=True)
        pb = (e / jnp.where(lsum > 0.0, lsum, 1.0)).astype(BF16)
        o_c = jnp.dot(cvt_ref[0], pb, preferred_element_type=F32)
        p4 = jnp.concatenate([pb[:, r * QT:(r + 1) * QT] for r in range(GQA_R)], axis=0)
        imp = jnp.dot(ovt_ref[...], p4, preferred_element_type=F32)
        imp = jnp.where(forced, FORCE, imp)
        imp = jnp.where(future, NEG, imp)
        imp_ref[...] = imp

        def rank_body(ii, cnt):
            row = imp_ref[pl.ds(ii, 1), :]
            beats = (row > imp) | ((row == imp) & (jj > ii))
            return cnt + jnp.where(beats, 1.0, 0.0)

        cnt = lax.fori_loop(0, n_sel, rank_body, jnp.zeros((n_sel, QT), F32))
        selneg_ref[...] = jnp.where((cnt < N_SELECT) & jnp.logical_not(future), 0.0, NEG)
        online_step(qs, ksk_ref, ksvt_ref, i, near_ref[0, g] + sel_mask(i), True)

        @pl.when(i >= 1)
        def _():
            online_step(qs, ksk_ref, ksvt_ref, i - 1, near_ref[1, g] + sel_mask(i - 1), False)

        def far_body(kt, carry):
            online_step(qs, ksk_ref, ksvt_ref, kt, far_ref[g] + sel_mask(kt), False)
            return carry

        lax.fori_loop(0, jnp.maximum(i - 1, 0), far_body, 0)
        o_s = acc_ref[...] / l_ref[...]
        online_step(qs, kwk_ref, kwvt_ref, i, near_ref[0, g], True)

        @pl.when(i >= 1)
        def _():
            online_step(qs, kwk_ref, kwvt_ref, i - 1, near_ref[1, g], False)

        @pl.when(i >= 2)
        def _():
            online_step(qs, kwk_ref, kwvt_ref, i - 2, near_ref[2, g], False)

        o_w = acc_ref[...] / l_ref[...]
        for r in range(GQA_R):
            h = g * GQA_R + r
            rows = slice(g * HEAD_DIM, (g + 1) * HEAD_DIM)
            cols = slice(r * QT, (r + 1) * QT)
            outt_ref[h * HEAD_DIM:(h + 1) * HEAD_DIM, :] = (
                gsig[h:h + 1, :] * o_c[rows, cols]
                + gsig[N_HEADS + h:N_HEADS + h + 1, :] * o_s[rows, cols]
                + gsig[2 * N_HEADS + h:2 * N_HEADS + h + 1, :] * o_w[rows, cols])
    o_ref[0] = outt_ref[...].T.astype(BF16)


def _prompt_attn(q, gt, ck, cvt, ksk, ksvt, kwk, kwvt, tables, bsz, t_len):
    near, far, biasc, ovt4 = tables
    n_half = t_len // CMP_STRIDE
    n_sel = t_len // SEL_BLOCK
    nkt = t_len // KT
    per_b2 = lambda c: pl.BlockSpec((t_len, c), lambda b, i: (b, 0))
    vt_spec = pl.BlockSpec((nkt, LANES, KT), lambda b, i: (b, 0, 0))
    return pl.pallas_call(
        functools.partial(_prompt_attn_kernel, n_sel=n_sel),
        grid=(bsz, t_len // QT),
        in_specs=[pl.BlockSpec((1, QT, N_HEADS * LANES), lambda b, i: (b, i, 0)),
                  pl.BlockSpec((1, QT, LANES), lambda b, i: (b, i, 0)),
                  pl.BlockSpec((1, n_half, LANES), lambda b, i: (b, 0, 0)),
                  pl.BlockSpec((1, LANES, n_half), lambda b, i: (b, 0, 0)),
                  per_b2(LANES), vt_spec, per_b2(LANES), vt_spec,
                  pl.BlockSpec((1, N_KV, n_half, GQA_R * QT), lambda b, i: (i, 0, 0, 0)),
                  _const_spec(near.shape), _const_spec(far.shape), _const_spec(ovt4.shape)],
        out_specs=pl.BlockSpec((1, QT, NSA_W), lambda b, i: (b, i, 0)),
        out_shape=jax.ShapeDtypeStruct((bsz, t_len, NSA_W), BF16),
        scratch_shapes=[pltpu.VMEM((n_sel, QT), F32), pltpu.VMEM((n_sel, QT), F32),
                        pltpu.VMEM((1, GQA_R * QT), F32), pltpu.VMEM((1, GQA_R * QT), F32),
                        pltpu.VMEM((LANES, GQA_R * QT), F32), pltpu.VMEM((NSA_W, QT), F32)],
        compiler_params=_cparams("parallel", "arbitrary"),
        name="prompt_attn",
    )(q, gt, ck, cvt, ksk, ksvt, kwk, kwvt, biasc, near, far, ovt4)


def _oproj_kernel(x_ref, a_ref, b_ref, woa_ref, wob_ref, g_ref, wq_ref, x1_ref, q_ref):
    x1 = (x_ref[...] + jnp.dot(a_ref[...], woa_ref[...], preferred_element_type=F32)
          + jnp.dot(b_ref[...], wob_ref[...], preferred_element_type=F32))
    x1_ref[...] = x1
    h = _rms(x1, g_ref[...]).astype(BF16)
    q_ref[...] = jnp.dot(h, wq_ref[...], preferred_element_type=F32).astype(BF16)


def _oproj(x2d, a, b, w_oa, w_ob, g, w_xq):
    n = x2d.shape[0]
    row = lambda c: pl.BlockSpec((ROW_TILE, c), lambda i: (i, 0))
    hq = MEM_HEADS * MEM_HD
    return pl.pallas_call(
        _oproj_kernel,
        grid=(n // ROW_TILE,),
        in_specs=[row(D_MODEL), row(SGU_W), row(NSA_W), _const_spec(w_oa.shape), _const_spec(w_ob.shape),
                  _const_spec((1, D_MODEL)), _const_spec(w_xq.shape)],
        out_specs=[row(D_MODEL), row(hq)],
        out_shape=[jax.ShapeDtypeStruct((n, D_MODEL), F32), jax.ShapeDtypeStruct((n, hq), BF16)],
        compiler_params=_cparams("parallel"),
        name="oproj_xq",
    )(x2d, a, b, w_oa, w_ob, g, w_xq)


def _memkv_kernel(x_ref, g_ref, w_ref, o_ref, ob_ref):
    h = _rms(x_ref[...], g_ref[...]).astype(BF16)
    o = jnp.dot(h, w_ref[...], preferred_element_type=F32)
    o_ref[...] = o
    ob_ref[...] = o.astype(BF16)


def _memkv(mem2d, g, w):
    n = mem2d.shape[0]
    c = w.shape[1]
    row = lambda cc: pl.BlockSpec((ROW_TILE, cc), lambda i: (i, 0))
    return pl.pallas_call(
        _memkv_kernel,
        grid=(n // ROW_TILE,),
        in_specs=[row(D_MODEL), _const_spec((1, D_MODEL)), _const_spec(w.shape)],
        out_specs=[row(c), row(c)],
        out_shape=[jax.ShapeDtypeStruct((n, c), F32), jax.ShapeDtypeStruct((n, c), BF16)],
        compiler_params=_cparams("parallel"),
        name="memkv_proj",
    )(mem2d, g, w)


def _softmax_rows(s):
    e = jnp.exp(s - jnp.max(s, axis=-1, keepdims=True))
    return e / jnp.sum(e, axis=-1, keepdims=True)


def _xattn_tail(x1, o, wxo_ref, g_ref, wr_ref, br_ref, x2_ref, h_ref, te_ref, tg_ref):
    x2 = x1 + jnp.dot(o.astype(BF16), wxo_ref[...], preferred_element_type=F32)
    x2_ref[...] = x2
    hb = _rms(x2, g_ref[...]).astype(BF16)
    h_ref[...] = hb
    lt = lax.dot_general(wr_ref[...], hb, (((1,), (1,)), ((), ())), preferred_element_type=F32) + br_ref[...]
    eidx = lax.broadcasted_iota(jnp.int32, lt.shape, 0)
    tops, idxs = [], []
    for _ in range(TOP_K):
        m = jnp.max(lt, axis=0, keepdims=True)
        idx = jnp.min(jnp.where(lt == m, eidx, N_EXPERTS), axis=0, keepdims=True)
        tops.append(m)
        idxs.append(idx)
        lt = jnp.where(eidx == idx, -jnp.inf, lt)
    es = [jnp.exp(t - tops[0]) for t in tops]
    den = es[0] + es[1] + es[2] + es[3]
    te_ref[...] = jnp.concatenate(idxs, axis=0)
    tg_ref[...] = jnp.concatenate([e / den for e in es], axis=0)


def _xattn_prompt_kernel(x1_ref, q_ref, kv_ref, wxo_ref, g_ref, wr_ref, br_ref, x2_ref, h_ref, te_ref, tg_ref):
    outs = []
    for hh in range(MEM_HEADS):
        qh = q_ref[:, hh * MEM_HD:(hh + 1) * MEM_HD]
        kh = kv_ref[:, hh * MEM_HD:(hh + 1) * MEM_HD]
        vh = kv_ref[:, (MEM_HEADS + hh) * MEM_HD:(MEM_HEADS + hh + 1) * MEM_HD]
        s = lax.dot_general(qh, kh, (((1,), (1,)), ((), ())), preferred_element_type=F32) * (MEM_HD ** -0.5)
        outs.append(jnp.dot(_softmax_rows(s).astype(BF16), vh, preferred_element_type=F32))
    o = jnp.concatenate(outs, axis=1)
    _xattn_tail(x1_ref[...], o, wxo_ref, g_ref, wr_ref, br_ref, x2_ref, h_ref, te_ref, tg_ref)


def _xattn_sample_kernel(x1_ref, q_ref, kv_ref, wxo_ref, g_ref, wr_ref, br_ref, x2_ref, h_ref, te_ref, tg_ref,
                         o_scr, *, t_len):
    nb = q_ref.shape[0] // t_len
    qf = q_ref[...].astype(F32)
    for bb in range(nb):
        for hh in range(MEM_HEADS):
            qh = qf[bb * t_len:(bb + 1) * t_len, hh * MEM_HD:(hh + 1) * MEM_HD].astype(BF16)
            kh = kv_ref[bb, pl.ds(hh, MEM_LEN, stride=2 * MEM_HEADS), :].astype(BF16)
            vh = kv_ref[bb, pl.ds(MEM_HEADS + hh, MEM_LEN, stride=2 * MEM_HEADS), :].astype(BF16)
            s = lax.dot_general(qh, kh, (((1,), (1,)), ((), ())), preferred_element_type=F32) * (MEM_HD ** -0.5)
            o_scr[bb * t_len:(bb + 1) * t_len, hh * MEM_HD:(hh + 1) * MEM_HD] = jnp.dot(
                _softmax_rows(s).astype(BF16), vh, preferred_element_type=F32)
    _xattn_tail(x1_ref[...], o_scr[...], wxo_ref, g_ref, wr_ref, br_ref, x2_ref, h_ref, te_ref, tg_ref)


XS_BATCH = 16


def _xattn(x1, q, kv, w_xo, g, w_rt, b_r, t_len, prompt):
    n = x1.shape[0]
    hq = MEM_HEADS * MEM_HD
    if prompt:
        rows = ROW_TILE
        per_b = t_len // rows
        kv_spec = pl.BlockSpec((MEM_LEN, 2 * hq), lambda i: (i // per_b, 0))
        kern = _xattn_prompt_kernel
        scratch = []
    else:
        rows = XS_BATCH * t_len
        kv_spec = pl.BlockSpec((XS_BATCH, MEM_LEN * 2 * MEM_HEADS, MEM_HD), lambda i: (i, 0, 0))
        kern = functools.partial(_xattn_sample_kernel, t_len=t_len)
        scratch = [pltpu.VMEM((rows, hq), F32)]
    row = lambda c: pl.BlockSpec((rows, c), lambda i: (i, 0))
    col = pl.BlockSpec((TOP_K, rows), lambda i: (0, i))
    return pl.pallas_call(
        kern,
        grid=(n // rows,),
        in_specs=[row(D_MODEL), row(hq), kv_spec, _const_spec(w_xo.shape), _const_spec((1, D_MODEL)),
                  _const_spec(w_rt.shape), _const_spec(b_r.shape)],
        out_specs=[row(D_MODEL), row(D_MODEL), col, col],
        out_shape=[jax.ShapeDtypeStruct((n, D_MODEL), F32), jax.ShapeDtypeStruct((n, D_MODEL), BF16),
                   jax.ShapeDtypeStruct((TOP_K, n), jnp.int32), jax.ShapeDtypeStruct((TOP_K, n), F32)],
        scratch_shapes=scratch,
        compiler_params=_cparams("parallel"),
        name="xattn_router_prompt" if prompt else "xattn_router_sample",
    )(x1, q, kv, w_xo, g, w_rt, b_r)


MOE_BLOCK = 256


def _expert_kernel(blk_e_ref, n_used_ref, x_ref, wgu_ref, bgu_ref, wd_ref, bd_ref, y_ref):
    @pl.when(pl.program_id(0) < n_used_ref[0])
    def _():
        gu = jnp.dot(x_ref[...], wgu_ref[0], preferred_element_type=F32) + bgu_ref[0]
        glu = jnp.minimum(gu[:, :D_FF], SWIGLU_LIMIT)
        lin = jnp.clip(gu[:, D_FF:], -SWIGLU_LIMIT, SWIGLU_LIMIT)
        hdn = glu * jax.nn.sigmoid(SWIGLU_ALPHA * glu) * (lin + 1.0)
        y_ref[...] = jnp.dot(hdn.astype(BF16), wd_ref[0], preferred_element_type=F32) + bd_ref[0]

    @pl.when(pl.program_id(0) >= n_used_ref[0])
    def _():
        y_ref[...] = jnp.zeros_like(y_ref)


def _experts(blk_e, n_used, xs, w_gu, b_gu, w_down, b_down):
    n_rows = xs.shape[0]
    n_blocks = n_rows // MOE_BLOCK
    grid_spec = pltpu.PrefetchScalarGridSpec(
        num_scalar_prefetch=2,
        grid=(n_blocks,),
        in_specs=[pl.BlockSpec((MOE_BLOCK, D_MODEL), lambda i, be, nu: (i, 0)),
                  pl.BlockSpec((1, D_MODEL, 2 * D_FF), lambda i, be, nu: (be[i], 0, 0)),
                  pl.BlockSpec((1, 1, 2 * D_FF), lambda i, be, nu: (be[i], 0, 0)),
                  pl.BlockSpec((1, D_FF, D_MODEL), lambda i, be, nu: (be[i], 0, 0)),
                  pl.BlockSpec((1, 1, D_MODEL), lambda i, be, nu: (be[i], 0, 0))],
        out_specs=pl.BlockSpec((MOE_BLOCK, D_MODEL), lambda i, be, nu: (i, 0)),
    )
    return pl.pallas_call(
        _expert_kernel,
        grid_spec=grid_spec,
        out_shape=jax.ShapeDtypeStruct((n_rows, D_MODEL), F32),
        compiler_params=_cparams("arbitrary"),
        name="moe_experts",
    )(blk_e, n_used, xs, w_gu, b_gu, w_down, b_down)


def _moe_dispatch(te):
    n = te.shape[1]
    nk = TOP_K * n
    n_blocks = -(-(nk + N_EXPERTS * (MOE_BLOCK - 1)) // MOE_BLOCK)
    flat_e = te.reshape(nk)
    order = jnp.argsort(flat_e, stable=True)
    sorted_e = flat_e[order]
    counts = jnp.bincount(flat_e, length=N_EXPERTS)
    padded = ((counts + MOE_BLOCK - 1) // MOE_BLOCK) * MOE_BLOCK
    start = jnp.cumsum(counts) - counts
    pend = jnp.cumsum(padded)
    pstart = pend - padded
    dest = (pstart[sorted_e] + (jnp.arange(nk) - start[sorted_e])).astype(jnp.int32)
    row_tok = jnp.zeros((n_blocks * MOE_BLOCK,), jnp.int32).at[dest].set((order % n).astype(jnp.int32))
    inv = jnp.zeros((nk,), jnp.int32).at[order].set(dest).reshape(TOP_K, n)
    blk_e = jnp.minimum(jnp.searchsorted(pend, jnp.arange(n_blocks) * MOE_BLOCK, side='right'),
                        N_EXPERTS - 1).astype(jnp.int32)
    n_used = (pend[-1] // MOE_BLOCK).astype(jnp.int32).reshape(1)
    return row_tok, inv, blk_e, n_used


def _final_kernel(x_ref, y_ref, g_ref, fn_ref, o_ref):
    x = x_ref[...]
    for k in range(TOP_K):
        x = x + y_ref[k] * g_ref[:, k:k + 1]
    o_ref[...] = _rms(x, fn_ref[...])


def _final(x2, yk, gates, fnorm):
    n = x2.shape[0]
    rows = 256
    return pl.pallas_call(
        _final_kernel,
        grid=(n // rows,),
        in_specs=[pl.BlockSpec((rows, D_MODEL), lambda i: (i, 0)),
                  pl.BlockSpec((TOP_K, rows, D_MODEL), lambda i: (0, i, 0)),
                  pl.BlockSpec((rows, TOP_K), lambda i: (i, 0)), _const_spec((1, D_MODEL))],
        out_specs=pl.BlockSpec((rows, D_MODEL), lambda i: (i, 0)),
        out_shape=jax.ShapeDtypeStruct((n, D_MODEL), F32),
        compiler_params=_cparams("parallel"),
        name="moe_combine_final_norm",
    )(x2, yk, gates, fnorm)


def _page_copy(pool_ref, pt_ref, buf_ref, sem_ref, b, p, slot, n_pages, rows):
    return pltpu.make_async_copy(pool_ref.at[pt_ref[b * n_pages + p]],
                                 buf_ref.at[slot, pl.ds(p * rows, rows)], sem_ref.at[slot])


def _fetch_pages(pool_ref, pt_ref, buf_ref, sem_ref, b, slot, n_pages, rows):
    def body(p, c):
        _page_copy(pool_ref, pt_ref, buf_ref, sem_ref, b, p, slot, n_pages, rows).start()
        return c
    lax.fori_loop(0, n_pages, body, 0)


def _wait_pages(pool_ref, pt_ref, buf_ref, sem_ref, b, slot, n_pages, rows):
    def body(p, c):
        _page_copy(pool_ref, pt_ref, buf_ref, sem_ref, b, p, slot, n_pages, rows).wait()
        return c
    lax.fori_loop(0, n_pages, body, 0)


def _compress_sample_kernel(pt_ref, pool_ref, hi_ref, w_ref, b_ref, ck_ref, cv_ref, buf_ref, sem_ref,
                            *, n_pages, page_rows):
    b = pl.program_id(0)
    nb = pl.num_programs(0)
    slot = lax.rem(b, 2)
    rows = 2 * page_rows
    n_half = n_pages * page_rows // CMP_STRIDE

    @pl.when(b == 0)
    def _():
        _fetch_pages(pool_ref, pt_ref, buf_ref, sem_ref, b, slot, n_pages, rows)

    @pl.when(b + 1 < nb)
    def _():
        _fetch_pages(pool_ref, pt_ref, buf_ref, sem_ref, b + 1, 1 - slot, n_pages, rows)

    _wait_pages(pool_ref, pt_ref, buf_ref, sem_ref, b, slot, n_pages, rows)
    last = lax.broadcasted_iota(jnp.int32, (n_half, LANES), 0) == n_half - 1
    for c, out in enumerate((ck_ref, cv_ref)):
        acc = jnp.zeros((n_half, 2 * LANES), F32)
        for r in range(CMP_STRIDE):
            xr = buf_ref[slot, pl.ds(2 * r + c, n_half, stride=2 * CMP_STRIDE), :].astype(BF16)
            acc = acc + jnp.dot(xr, w_ref[c, r], preferred_element_type=F32)
        hi_next = pltpu.roll(acc[:, LANES:], n_half - 1, 0)
        hi_next = jnp.where(last, hi_ref[0, c:c + 1, :], hi_next)
        out[0] = (acc[:, :LANES] + hi_next + b_ref[c:c + 1, :]).astype(BF16)


def _hi_new_kernel(ak_ref, av_ref, w_ref, o_ref):
    for c, a_ref in enumerate((ak_ref, av_ref)):
        o_ref[:, c, :] = jnp.dot(a_ref[...].astype(BF16), w_ref[c], preferred_element_type=F32)


def _hi_new(kck_new, kcv_new, w_cmp, n_batch, t_len):
    w_hi = w_cmp[:, :t_len, :, LANES:].reshape(2, t_len * LANES, LANES)
    ak = kck_new.reshape(n_batch, t_len * LANES)
    av = kcv_new.reshape(n_batch, t_len * LANES)
    return pl.pallas_call(
        _hi_new_kernel,
        out_shape=jax.ShapeDtypeStruct((n_batch, 2, LANES), F32),
        name="compress_new_tokens",
    )(ak, av, w_hi)


def _compress_sample(page_table_flat, pool, hi_new, w_cmp, b_cmp, n_batch, n_pages, page_rows):
    n_half = n_pages * page_rows // CMP_STRIDE
    grid_spec = pltpu.PrefetchScalarGridSpec(
        num_scalar_prefetch=1,
        grid=(n_batch,),
        in_specs=[pl.BlockSpec(memory_space=pl.ANY),
                  pl.BlockSpec((1, 2, LANES), lambda b, pt: (b, 0, 0)),
                  pl.BlockSpec(w_cmp.shape, lambda b, pt: (0, 0, 0, 0)),
                  pl.BlockSpec(b_cmp.shape, lambda b, pt: (0, 0))],
        out_specs=[pl.BlockSpec((1, n_half, LANES), lambda b, pt: (b, 0, 0)),
                   pl.BlockSpec((1, n_half, LANES), lambda b, pt: (b, 0, 0))],
        scratch_shapes=[pltpu.VMEM((2, n_pages * 2 * page_rows, LANES), F32), pltpu.SemaphoreType.DMA((2,))],
    )
    return pl.pallas_call(
        functools.partial(_compress_sample_kernel, n_pages=n_pages, page_rows=page_rows),
        grid_spec=grid_spec,
        out_shape=[jax.ShapeDtypeStruct((n_batch, n_half, LANES), BF16)] * 2,
        compiler_params=_cparams("arbitrary"),
        name="compress_sample",
    )(page_table_flat, pool, hi_new, w_cmp, b_cmp)


N_SEL_PAD = 256


def _sample_tables(rel_bias, past, t_len):
    n_cmp = past // CMP_STRIDE
    n_sel = past // SEL_BLOCK + 1
    y = np.tile(np.arange(t_len), N_HEADS)[:, None]
    hh = np.repeat(np.arange(N_HEADS), t_len)[:, None]

    def tab(dist, valid):
        return jnp.where(valid, rel_bias[_rel_bucket_np(dist), hh], NEG)

    dc = past + y - (np.arange(n_cmp)[None, :] * CMP_STRIDE + CMP_LEN - 1)
    biasc = tab(dc, dc >= 0)
    dp = past + y - np.arange(past)[None, :]
    biasp = tab(dp, dp >= 0)
    dn = y - np.arange(t_len)[None, :]
    biasn = tab(dn, dn >= 0)
    dw = WINDOW + y - np.arange(WINDOW)[None, :]
    biasw = tab(dw, dw < WINDOW)
    cs = np.arange(n_cmp)[:, None] * CMP_STRIDE
    ss = np.arange(N_SEL_PAD)[None, :] * SEL_BLOCK
    ov = ((cs < ss + SEL_BLOCK) & (cs + CMP_LEN > ss) & (np.arange(N_SEL_PAD)[None, :] < n_sel))
    ov4 = jnp.asarray(np.tile(ov.astype(np.float32), (GQA_R, 1)), BF16)
    expand = (np.arange(past)[None, :] // SEL_BLOCK == np.arange(past // SEL_BLOCK)[:, None])
    return biasc, biasp, biasn, biasw, ov4, jnp.asarray(expand.astype(np.float32), BF16)


def _sample_attn_kernel(pt_ref, q_ref, gt_ref, ck_ref, cv_ref, pool_ref, win_ref, ksk_ref, ksv_ref, kwk_ref,
                        kwv_ref, biasc_ref, biasp_ref, biasn_ref, biasw_ref, ov_ref, exp_ref, o_ref,
                        buf_ref, sem_ref, s_ref, *, n_pages, page_tokens, t_len, cur_block):
    b = pl.program_id(0)
    nb = pl.num_programs(0)
    slot = lax.rem(b, 2)
    rows = 2 * LANES
    nt = (((1,), (1,)), ((), ()))
    n_rows = N_HEADS * t_len

    @pl.when(b == 0)
    def _():
        _fetch_pages(pool_ref, pt_ref, buf_ref, sem_ref, b, slot, n_pages, rows)

    @pl.when(b + 1 < nb)
    def _():
        _fetch_pages(pool_ref, pt_ref, buf_ref, sem_ref, b + 1, 1 - slot, n_pages, rows)

    qb = q_ref[0]
    bc = biasc_ref[...]
    sc = lax.dot_general(qb, ck_ref[0], nt, preferred_element_type=F32) + bc
    e = jnp.exp(sc - jnp.max(sc, axis=1, keepdims=True))
    e = jnp.where(bc > 0.5 * NEG, e, 0.0)
    lsum = jnp.sum(e, axis=1, keepdims=True)
    pb = (e / jnp.where(lsum > 0.0, lsum, 1.0)).astype(BF16)
    o_c = jnp.dot(pb, cv_ref[0], preferred_element_type=F32)
    p4 = jnp.concatenate(
        [jnp.concatenate([pb[(g * GQA_R + r) * t_len:(g * GQA_R + r + 1) * t_len, :] for r in range(GQA_R)], axis=1)
         for g in range(N_KV)], axis=0)
    imp = jnp.dot(p4, ov_ref[...], preferred_element_type=F32)
    lane = lax.broadcasted_iota(jnp.int32, imp.shape, 1)
    imp = jnp.where((lane == 0) | (lane == cur_block) | (lane == cur_block - 1), FORCE, imp)
    imp = jnp.where(lane > cur_block, NEG, imp)

    def rank_body(k, cnt):
        other = pltpu.roll(imp, k, 1)
        beats = (other > imp) | ((other == imp) & (lane >= k))
        return cnt + jnp.where(beats, 1.0, 0.0)

    cnt = lax.fori_loop(1, N_SEL_PAD, rank_body, jnp.zeros(imp.shape, F32))
    sel = jnp.where((cnt < N_SELECT) & (lane <= cur_block), 1.0, 0.0)
    sel_rows = jnp.concatenate([sel[g * t_len:(g + 1) * t_len, :] for g in range(N_KV) for _ in range(GQA_R)], axis=0)
    n_past_blocks = n_pages * page_tokens // SEL_BLOCK
    keep = jnp.dot(sel_rows[:, :n_past_blocks].astype(BF16), exp_ref[...], preferred_element_type=F32)

    def softmax_pair(s_old, s_new):
        m = jnp.maximum(jnp.max(s_old, axis=1, keepdims=True), jnp.max(s_new, axis=1, keepdims=True))
        p_old = jnp.exp(s_old - m)
        p_new = jnp.exp(s_new - m)
        inv = 1.0 / (jnp.sum(p_old, axis=1, keepdims=True) + jnp.sum(p_new, axis=1, keepdims=True))
        return p_old, p_new, inv

    _wait_pages(pool_ref, pt_ref, buf_ref, sem_ref, b, slot, n_pages, rows)
    for p in range(n_pages):
        kt_page = buf_ref[slot, p * rows:p * rows + LANES, :].astype(BF16)
        s_ref[:, p * page_tokens:(p + 1) * page_tokens] = jnp.dot(qb, kt_page, preferred_element_type=F32)
    s_past = jnp.where(keep > 0.5, s_ref[...] + biasp_ref[...], NEG)
    bn = biasn_ref[...]
    s_new = lax.dot_general(qb, ksk_ref[...].astype(BF16), nt, preferred_element_type=F32) + bn
    p_past, p_new, inv = softmax_pair(s_past, s_new)
    p_past = p_past.astype(BF16)
    o_s = jnp.dot(p_new.astype(BF16), ksv_ref[...].astype(BF16), preferred_element_type=F32)
    for p in range(n_pages):
        vt_page = buf_ref[slot, p * rows + LANES:(p + 1) * rows, :].astype(BF16)
        o_s = o_s + lax.dot_general(p_past[:, p * page_tokens:(p + 1) * page_tokens], vt_page, nt,
                                    preferred_element_type=F32)
    o_s = o_s * inv
    s_win = jnp.dot(qb, win_ref[0, :LANES, :].astype(BF16), preferred_element_type=F32) + biasw_ref[...]
    s_wnew = lax.dot_general(qb, kwk_ref[...].astype(BF16), nt, preferred_element_type=F32) + bn
    p_win, p_wnew, inv_w = softmax_pair(s_win, s_wnew)
    o_w = (lax.dot_general(p_win.astype(BF16), win_ref[0, LANES:, :].astype(BF16), nt, preferred_element_type=F32)
           + jnp.dot(p_wnew.astype(BF16), kwv_ref[...].astype(BF16), preferred_element_type=F32)) * inv_w
    gs = jax.nn.sigmoid(gt_ref[0])
    o = gs[:, 0:1] * o_c + gs[:, 1:2] * o_s + gs[:, 2:3] * o_w
    row = lax.broadcasted_iota(jnp.int32, (n_rows, HEAD_DIM), 0)
    o_ref[0] = jnp.where(row < GQA_R * t_len, o[:, :HEAD_DIM], o[:, HEAD_DIM:])


def _sample_attn(page_table_flat, q, gt, ck, cv, pool, win, ksk, ksv, kwk, kwv, tables, n_batch, n_pages, page_tokens,
                 t_len):
    biasc, biasp, biasn, biasw, ov4, expand = tables
    past = n_pages * page_tokens
    n_rows = N_HEADS * t_len
    cur_block = past // SEL_BLOCK
    assert past % SEL_BLOCK == 0 and t_len <= SEL_BLOCK and cur_block < N_SEL_PAD
    new_spec = pl.BlockSpec((t_len, LANES), lambda b, pt: (b, 0))
    cst = lambda a: pl.BlockSpec(a.shape, lambda b, pt: (0,) * a.ndim)
    grid_spec = pltpu.PrefetchScalarGridSpec(
        num_scalar_prefetch=1,
        grid=(n_batch,),
        in_specs=[pl.BlockSpec((1, n_rows, LANES), lambda b, pt: (b, 0, 0)),
                  pl.BlockSpec((1, n_rows, 3), lambda b, pt: (b, 0, 0)),
                  pl.BlockSpec((1, past // CMP_STRIDE, LANES), lambda b, pt: (b, 0, 0)),
                  pl.BlockSpec((1, past // CMP_STRIDE, LANES), lambda b, pt: (b, 0, 0)),
                  pl.BlockSpec(memory_space=pl.ANY),
                  pl.BlockSpec((1, 2 * LANES, WINDOW), lambda b, pt: (b, 0, 0)),
                  new_spec, new_spec, new_spec, new_spec,
                  cst(biasc), cst(biasp), cst(biasn), cst(biasw), cst(ov4), cst(expand)],
        out_specs=pl.BlockSpec((1, n_rows, HEAD_DIM), lambda b, pt: (b, 0, 0)),
        scratch_shapes=[pltpu.VMEM((2, n_pages * 2 * LANES, page_tokens), F32), pltpu.SemaphoreType.DMA((2,)),
                        pltpu.VMEM((n_rows, past), F32)],
    )
    return pl.pallas_call(
        functools.partial(_sample_attn_kernel, n_pages=n_pages, page_tokens=page_tokens, t_len=t_len,
                          cur_block=cur_block),
        grid_spec=grid_spec,
        out_shape=jax.ShapeDtypeStruct((n_batch, n_rows, HEAD_DIM), F32),
        compiler_params=_cparams("arbitrary"),
        name="sample_attn",
    )(page_table_flat, q, gt, ck, cv, pool, win, ksk, ksv, kwk, kwv, biasc, biasp, biasn, biasw, ov4, expand)


def kernel(x_prompt, x_sample, mem_prompt, cache_cmp_kv, cache_slc_kv, cache_win_kv, cache_mem_kv, page_table, norm_mix, w_in, sgu_norm_g, sgu_norm_b, sgu_w, sgu_b, cmp_w, cmp_b, w_o, rel_bias, norm_x, norm_mem, w_xq, w_mem_kv, w_xo, norm_ffn, w_router, b_router, w_gu, b_gu, w_down, b_down, final_norm):
    assert norm_mix.shape[0] == 1, "single-layer trunk"
    bsz, t_p, _ = x_prompt.shape
    dbs, t_s, _ = x_sample.shape
    n_phys, page_tokens = cache_cmp_kv.shape[1], cache_cmp_kv.shape[2]
    n_pages = page_table.shape[1]
    past = n_pages * page_tokens
    wb = cache_win_kv.shape[2]
    assert wb == WINDOW and t_p >= WINDOW and t_p >= CHUNK and t_s <= CHUNK
    n_p, n_s = bsz * t_p, dbs * t_s
    row1 = lambda v: v.reshape(1, -1)

    w_all = _build_w_in(w_in[0])
    mix_p, mixb_p = _build_sgu_mix(sgu_w[0], sgu_b[0], CHUNK, CHUNK)
    mix_s, mixb_s = _build_sgu_mix(sgu_w[0], sgu_b[0], t_s, ROW_TILE)
    w_cmp = _build_cmp_w(cmp_w[0])
    b_cmp = cmp_b[0].reshape(2, LANES)
    w_oa, w_ob = w_o[0, :SGU_W].astype(BF16), w_o[0, SGU_W:].astype(BF16)
    w_xq_b, w_xo_b = w_xq[0].astype(BF16), w_xo[0].astype(BF16)
    w_rt = w_router[0].T.astype(BF16)
    b_rt = b_router[0].reshape(N_EXPERTS, 1)
    in_args = (row1(norm_mix[0]), w_all, row1(sgu_norm_g[0]), row1(sgu_norm_b[0]))
    tail_args = (w_xo_b, row1(norm_ffn[0]), w_rt, b_rt)

    xp = x_prompt.reshape(n_p, D_MODEL)
    (a_p, v_p, q_p, gt_p, kck, kcv, ksk, ksv, kwk, kwv, kskb, ksvt, kwkb, kwvt) = _inproj(
        xp, *in_args, mix_p, mixb_p, True)
    ck_p, cvt_p = _compress_prompt(kck, kcv, w_cmp, b_cmp, bsz, t_p)
    b_p = _prompt_attn(q_p.reshape(bsz, t_p, N_HEADS * LANES), gt_p.reshape(bsz, t_p, LANES), ck_p, cvt_p,
                       kskb, ksvt, kwkb, kwvt, _attn_tables(rel_bias, t_p), bsz, t_p)
    x1_p, qm_p = _oproj(xp, a_p, b_p.reshape(n_p, NSA_W), w_oa, w_ob, row1(norm_x[0]), w_xq_b)
    memkv_f, memkv_b = _memkv(mem_prompt.reshape(bsz * MEM_LEN, D_MODEL), row1(norm_mem[0]), w_mem_kv[0].astype(BF16))
    x2_p, h_p, te_p, tg_p = _xattn(x1_p, qm_p, memkv_b, *tail_args, t_p, True)

    xs = x_sample.reshape(n_s, D_MODEL)
    (a_s, v_s, q_s, gt_s, kck_s, kcv_s, ksk_s, ksv_s, kwk_s, kwv_s) = _inproj(xs, *in_args, mix_s, mixb_s, False)
    pt_flat = page_table.reshape(-1)
    pool_c = cache_cmp_kv[0].reshape(n_phys, 2 * page_tokens, LANES)
    pool_s = cache_slc_kv[0].transpose(0, 2, 3, 4, 1).reshape(n_phys, 2 * LANES, page_tokens)
    win_t = cache_win_kv[0].transpose(0, 2, 3, 4, 1).reshape(dbs, 2 * LANES, wb)
    hi_new = _hi_new(kck_s, kcv_s, w_cmp, dbs, t_s)
    ck_s, cv_s = _compress_sample(pt_flat, pool_c, hi_new, w_cmp, b_cmp, dbs, n_pages, page_tokens)
    q_sb = q_s.reshape(dbs, t_s, N_HEADS, LANES).transpose(0, 2, 1, 3).reshape(dbs, N_HEADS * t_s, LANES)
    gt_sb = gt_s[:, :3 * N_HEADS].reshape(dbs, t_s, 3, N_HEADS).transpose(0, 3, 1, 2).reshape(dbs, N_HEADS * t_s, 3)
    bo_s = _sample_attn(pt_flat, q_sb, gt_sb, ck_s, cv_s, pool_s, win_t, ksk_s, ksv_s, kwk_s, kwv_s,
                        _sample_tables(rel_bias, past, t_s), dbs, n_pages, page_tokens, t_s)
    b_s = bo_s.reshape(dbs, N_HEADS, t_s, HEAD_DIM).transpose(0, 2, 1, 3).reshape(n_s, NSA_W).astype(BF16)
    x1_s, qm_s = _oproj(xs, a_s, b_s, w_oa, w_ob, row1(norm_x[0]), w_xq_b)
    memkv_s = cache_mem_kv[0].reshape(dbs, MEM_LEN * 2 * MEM_HEADS, MEM_HD)
    x2_s, h_s, te_s, tg_s = _xattn(x1_s, qm_s, memkv_s, *tail_args, t_s, False)

    row_tok, inv, blk_e, n_used = _moe_dispatch(jnp.concatenate([te_p, te_s], axis=1))
    xs_sorted = jnp.concatenate([h_p, h_s], axis=0)[row_tok]
    y = _experts(blk_e, n_used, xs_sorted, w_gu[0].astype(BF16), b_gu[0][:, None, :],
                 w_down[0].astype(BF16), b_down[0][:, None, :])
    fn = row1(final_norm)
    y_prompt = _final(x2_p, y[inv[:, :n_p]], tg_p.T, fn).reshape(bsz, t_p, D_MODEL)
    y_sample = _final(x2_s, y[inv[:, n_p:]], tg_s.T, fn).reshape(dbs, t_s, D_MODEL)

    kv6 = lambda k, v, b, t: jnp.concatenate([k, v], axis=1).reshape(1, b, t, 2, N_KV, HEAD_DIM)
    kw_p = kv6(kwk, kwv, bsz, t_p)
    kw_s = kv6(kwk_s, kwv_s, dbs, t_s)
    win_s = jnp.concatenate([cache_win_kv, kw_s], axis=2)[:, :, -wb:]
    return (y_prompt, y_sample, kv6(kck, kcv, bsz, t_p), kv6(ksk, ksv, bsz, t_p), kw_p[:, :, -WINDOW:],
            memkv_f.reshape(1, bsz, MEM_LEN, 2, MEM_HEADS, MEM_HD), v_p.reshape(1, bsz, t_p, SGU_W)[:, :, -CHUNK:],
            kv6(kck_s, kcv_s, dbs, t_s), kv6(ksk_s, ksv_s, dbs, t_s), win_s, v_s.reshape(1, dbs, t_s, SGU_W))
```

```python
import functools
import math

import numpy as np
import jax
import jax.numpy as jnp
from jax import lax
from jax.experimental import pallas as pl
from jax.experimental.pallas import tpu as pltpu

D_MODEL = 1024
SGU_W = 512
SGU_GROUPS = 4
SGU_GW = 128
CHUNK = 128
NSA_W = 512
HEAD_DIM = 64
N_HEADS = 8
N_KV = 2
GQA_R = 4
CMP_LEN = 32
CMP_STRIDE = 16
SEL_BLOCK = 64
N_SELECT = 16
WINDOW = 512
KV_W = 256
N_BUCKETS = 32
MAX_EXACT = 16
MAX_DIST = 128
MEM_LEN = 256
MEM_HEADS = 4
MEM_HD = 128
N_EXPERTS = 32
TOP_K = 4
D_FF = 1024
SWIGLU_LIMIT = 7.0
SWIGLU_ALPHA = 1.702
EPS = 1e-6
NEG = -1e30
FORCE = 1e9

LANES = 128
VMEM_LIMIT = 56 * 1024 * 1024

ROW_TILE = 512
QT = 256
KT = 256

F32 = jnp.float32
BF16 = jnp.bfloat16


def _cparams(*sem):
    return pltpu.CompilerParams(dimension_semantics=sem, vmem_limit_bytes=VMEM_LIMIT)


def _const_spec(shape):
    nd = len(shape)
    return pl.BlockSpec(shape, lambda *_: (0,) * nd)


def _rms(x, g):
    return x * lax.rsqrt(jnp.mean(x * x, axis=-1, keepdims=True) + EPS) * g


def _rel_bucket_np(dist):
    n = np.maximum(dist, 0)
    nf = np.maximum(n, 1).astype(np.float32)
    large = MAX_EXACT + (np.log(nf / np.float32(MAX_EXACT)) / np.float32(math.log(MAX_DIST / MAX_EXACT))
                         * np.float32(N_BUCKETS - MAX_EXACT)).astype(np.int32)
    large = np.minimum(large, N_BUCKETS - 1)
    return np.where(n < MAX_EXACT, n, large).astype(np.int32)


C_U, C_V, C_Q, C_KC, C_GT, C_END = 0, 512, 1024, 2048, 2816, 2944


def _inproj_kernel(x_ref, g_ref, w_ref, lng_ref, lnb_ref, mix_ref, mixb_ref, *out_refs, mix_block, attn_extras):
    a_ref, v_ref, q_ref, gt_ref = out_refs[:4]
    kv_refs = out_refs[4:10]
    x = x_ref[...]
    h = _rms(x, g_ref[...]).astype(BF16)

    def proj(lo, hi):
        return jnp.dot(h, w_ref[:, lo:hi], preferred_element_type=F32)

    q_ref[...] = proj(C_Q, C_KC).astype(BF16)
    gt_ref[...] = proj(C_GT, C_END)
    kv = [proj(C_KC + LANES * j, C_KC + LANES * (j + 1)) for j in range(6)]
    for r, val in zip(kv_refs, kv):
        r[...] = val
    if attn_extras:
        kskb_ref, ksvt_ref, kwkb_ref, kwvt_ref = out_refs[10:14]
        kskb_ref[...] = kv[2].astype(BF16)
        kwkb_ref[...] = kv[4].astype(BF16)
        for vt_ref, val in ((ksvt_ref, kv[3]), (kwvt_ref, kv[5])):
            vt = val.T.astype(BF16)
            for j in range(val.shape[0] // KT):
                vt_ref[j] = vt[:, j * KT:(j + 1) * KT]

    u = jax.nn.gelu(proj(C_U, C_V))
    v = jax.nn.gelu(proj(C_V, C_Q))
    mu = jnp.mean(v, axis=-1, keepdims=True)
    var = jnp.mean(jnp.square(v - mu), axis=-1, keepdims=True)
    v = (v - mu) * lax.rsqrt(var + EPS) * lng_ref[...] + lnb_ref[...]
    v_ref[...] = v
    vb = v.astype(BF16)
    rows = x.shape[0]
    for blk in range(rows // mix_block):
        r0 = blk * mix_block
        for g in range(SGU_GROUPS):
            c0 = g * SGU_GW
            mixed = jnp.dot(mix_ref[g], vb[r0:r0 + mix_block, c0:c0 + SGU_GW],
                            preferred_element_type=F32) + mixb_ref[:, c0:c0 + SGU_GW]
            a_ref[r0:r0 + mix_block, c0:c0 + SGU_GW] = (
                u[r0:r0 + mix_block, c0:c0 + SGU_GW] * mixed).astype(BF16)


def _inproj(x2d, g, w_all, ln_g, ln_b, mix, mixb, attn_extras):
    n = x2d.shape[0]
    mb = mix.shape[1]
    row = lambda c: pl.BlockSpec((ROW_TILE, c), lambda i: (i, 0))
    outs = [(SGU_W, BF16), (SGU_W, F32), (N_HEADS * LANES, BF16), (LANES, F32)] + [(LANES, F32)] * 6
    out_specs = [row(c) for c, _ in outs]
    out_shape = [jax.ShapeDtypeStruct((n, c), dt) for c, dt in outs]
    if attn_extras:
        tiles = ROW_TILE // KT
        vt_spec = pl.BlockSpec((tiles, LANES, KT), lambda i: (i, 0, 0))
        vt_shape = jax.ShapeDtypeStruct((n // KT, LANES, KT), BF16)
        out_specs += [row(LANES), vt_spec, row(LANES), vt_spec]
        out_shape += [jax.ShapeDtypeStruct((n, LANES), BF16), vt_shape,
                      jax.ShapeDtypeStruct((n, LANES), BF16), vt_shape]
    return pl.pallas_call(
        functools.partial(_inproj_kernel, mix_block=mb, attn_extras=attn_extras),
        grid=(n // ROW_TILE,),
        in_specs=[row(D_MODEL), _const_spec((1, D_MODEL)), _const_spec(w_all.shape),
                  _const_spec((1, SGU_W)), _const_spec((1, SGU_W)), _const_spec(mix.shape),
                  _const_spec(mixb.shape)],
        out_specs=out_specs,
        out_shape=out_shape,
        compiler_params=_cparams("parallel"),
        name="inproj_sgu",
    )(x2d, g, w_all, ln_g, ln_b, mix, mixb)


def _build_w_in(w_in):
    u = w_in[:, 0:512]
    v = w_in[:, 512:1024]
    q = w_in[:, 1024:1536].reshape(D_MODEL, N_KV, GQA_R, HEAD_DIM) * (HEAD_DIM ** -0.5)
    qp = jnp.zeros((D_MODEL, N_KV, GQA_R, N_KV, HEAD_DIM), F32)
    for g in range(N_KV):
        qp = qp.at[:, g, :, g, :].set(q[:, g])
    qp = qp.reshape(D_MODEL, N_HEADS * LANES)
    kv = w_in[:, 1536:2304]
    gt = jnp.pad(w_in[:, 2304:2328], ((0, 0), (0, LANES - 3 * N_HEADS)))
    return jnp.concatenate([u, v, qp, kv, gt], axis=1).astype(BF16)


def _build_sgu_mix(sgu_w, sgu_b, chunk, mix_block):
    tri = jnp.tril(jnp.ones((chunk, chunk), bool))
    w = jnp.where(tri, sgu_w[:, :chunk, :chunk], 0.0)
    reps = mix_block // chunk
    eye = jnp.eye(reps, dtype=F32)
    mix = jnp.einsum('ab,gst->gasbt', eye, w).reshape(SGU_GROUPS, mix_block, mix_block)
    b = jnp.tile(sgu_b[:, :chunk], (1, reps))
    mixb = jnp.repeat(b.T, SGU_GW, axis=1)
    return mix.astype(BF16), mixb


def _build_cmp_w(cmp_w):
    w = cmp_w.reshape(2, N_KV, 2, CMP_STRIDE, HEAD_DIM, HEAD_DIM)
    out = jnp.zeros((2, CMP_STRIDE, N_KV, HEAD_DIM, 2, N_KV, HEAD_DIM), F32)
    for g in range(N_KV):
        out = out.at[:, :, g, :, :, g, :].set(jnp.transpose(w[:, g], (0, 2, 3, 1, 4)))
    return out.reshape(2, CMP_STRIDE, LANES, 2 * LANES).astype(BF16)


def _compress_kernel(kk_ref, kv_ref, w_ref, b_ref, ck_ref, cvt_ref, *, n_half):
    for c, src in enumerate((kk_ref, kv_ref)):
        acc = jnp.zeros((n_half, 2 * LANES), F32)
        for r in range(CMP_STRIDE):
            xr = src[pl.ds(r, n_half, stride=CMP_STRIDE), :].astype(BF16)
            acc = acc + jnp.dot(xr, w_ref[c, r], preferred_element_type=F32)
        lo = acc[:, :LANES]
        hi_next = pltpu.roll(acc[:, LANES:], n_half - 1, 0)
        comp = lo + hi_next + b_ref[c:c + 1, :]
        if c == 0:
            ck_ref[0] = comp.astype(BF16)
        else:
            cvt_ref[0] = comp.T.astype(BF16)


def _compress_prompt(kck, kcv, w_cmp, b_cmp, bsz, t_len):
    n_half = t_len // CMP_STRIDE
    return pl.pallas_call(
        functools.partial(_compress_kernel, n_half=n_half),
        grid=(bsz,),
        in_specs=[pl.BlockSpec((t_len, LANES), lambda b: (b, 0)), pl.BlockSpec((t_len, LANES), lambda b: (b, 0)),
                  _const_spec(w_cmp.shape), _const_spec(b_cmp.shape)],
        out_specs=[pl.BlockSpec((1, n_half, LANES), lambda b: (b, 0, 0)),
                   pl.BlockSpec((1, LANES, n_half), lambda b: (b, 0, 0))],
        out_shape=[jax.ShapeDtypeStruct((bsz, n_half, LANES), BF16),
                   jax.ShapeDtypeStruct((bsz, LANES, n_half), BF16)],
        compiler_params=_cparams("parallel"),
        name="compress_prompt",
    )(kck, kcv, w_cmp, b_cmp)


def _toeplitz(rel_bias, rows, cols, off, valid_fn):
    span = rows + cols - 1
    d = np.arange(span) + off - (rows - 1)
    vec = jnp.where(valid_fn(d)[None, :], rel_bias[_rel_bucket_np(d)].T, NEG)
    flat = jnp.tile(vec, (1, rows + 1))[:, :rows * (span + 1)]
    skew = flat.reshape(N_HEADS, rows, span + 1)
    return skew[:, ::-1, :cols]


def _head_table(rel_bias, dist, valid, head):
    bucket = _rel_bucket_np(dist)
    need = valid & (bucket != N_BUCKETS - 1)
    out = jnp.broadcast_to(rel_bias[N_BUCKETS - 1][head[:, 0]][:, None], dist.shape)
    cols = np.nonzero(need.any(axis=0))[0]
    if cols.size:
        c0, c1 = int(cols[0]), int(cols[-1]) + 1
        mid = jnp.where(need[:, c0:c1], rel_bias[bucket[:, c0:c1], head], out[:, c0:c1])
        out = jnp.concatenate([out[:, :c0], mid, out[:, c1:]], axis=1)
    return jnp.where(valid, out, NEG)


def _attn_tables(rel_bias, t_len):
    n_half = t_len // CMP_STRIDE
    n_sel = t_len // SEL_BLOCK
    nq = t_len // QT
    c_far = rel_bias[N_BUCKETS - 1]
    to_lanes = lambda a: a.reshape(N_KV, GQA_R, a.shape[1], QT).transpose(0, 2, 1, 3).reshape(
        N_KV, a.shape[1], GQA_R * QT)
    near = jnp.stack([
        to_lanes(_toeplitz(rel_bias, KT, QT, 0, lambda d: d >= 0)),
        to_lanes(_toeplitz(rel_bias, KT, QT, KT, lambda d: d >= 0)),
        to_lanes(_toeplitz(rel_bias, KT, QT, 2 * KT, lambda d: d < WINDOW))])
    far = jnp.repeat(c_far.reshape(N_KV, 1, GQA_R), QT, axis=2).reshape(N_KV, 1, GQA_R * QT)
    per_q = QT // CMP_STRIDE
    band_rows = 2 * per_q
    lead = per_q // 2
    off = lead * CMP_STRIDE - (CMP_LEN - 1)
    band = _toeplitz(rel_bias, band_rows * CMP_STRIDE, QT, off, lambda d: d >= 0)[:, ::CMP_STRIDE]
    assert _rel_bucket_np(np.array([off + CMP_STRIDE])).item() == N_BUCKETS - 1
    tiles = []
    for i in range(nq):
        lo = i * per_q - lead
        parts = []
        if lo > 0:
            parts.append(jnp.broadcast_to(c_far[:, None, None], (N_HEADS, lo, QT)))
        b0, b1 = max(0, -lo), min(band_rows, n_half - lo)
        parts.append(band[:, b0:b1])
        rest = n_half - (lo + b1)
        if rest > 0:
            parts.append(jnp.full((N_HEADS, rest, QT), NEG, F32))
        tiles.append(to_lanes(jnp.concatenate(parts, axis=1)))
    biasc = jnp.stack(tiles)
    cs = np.arange(n_half - 1)[None, :] * CMP_STRIDE
    ss = np.arange(n_sel)[:, None] * SEL_BLOCK
    ov = np.zeros((n_sel, n_half), np.float32)
    ov[:, :n_half - 1] = (cs < ss + SEL_BLOCK) & (cs + CMP_LEN > ss)
    ovt4 = jnp.asarray(np.tile(ov, (1, GQA_R)), BF16)
    return near, far, biasc, ovt4


def _prompt_attn_kernel(q_ref, gt_ref, ck_ref, cvt_ref, ksk_ref, ksvt_ref, kwk_ref, kwvt_ref,
                        biasc_ref, near_ref, far_ref, ovt_ref, o_ref,
                        imp_ref, selneg_ref, m_ref, l_ref, acc_ref, outt_ref, *, n_sel):
    i = pl.program_id(1)
    nt = (((1,), (1,)), ((), ()))
    gsig = jax.nn.sigmoid(gt_ref[0]).T
    jj = lax.broadcasted_iota(jnp.int32, (n_sel, QT), 0)
    tt = i * QT + lax.broadcasted_iota(jnp.int32, (n_sel, QT), 1)
    cur = lax.shift_right_logical(tt, 6)
    forced = (jj == 0) | (jj == cur) | (jj == cur - 1)
    future = jj > cur

    def online_step(qs, k_ref, vt_ref, kt, bias, first):
        kk = k_ref[pl.ds(pl.multiple_of(kt * KT, KT), KT), :]
        s = lax.dot_general(kk, qs, nt, preferred_element_type=F32) + bias
        mt = jnp.max(s, axis=0, keepdims=True)
        if first:
            m_new = mt
        else:
            m_old = m_ref[...]
            m_new = jnp.maximum(m_old, mt)
            alpha = jnp.exp(m_old - m_new)
        p = jnp.exp(s - m_new)
        pv = jnp.dot(vt_ref[kt], p.astype(BF16), preferred_element_type=F32)
        ps = jnp.sum(p, axis=0, keepdims=True)
        if first:
            l_ref[...] = ps
            acc_ref[...] = pv
        else:
            l_ref[...] = alpha * l_ref[...] + ps
            acc_ref[...] = alpha * acc_ref[...] + pv
        m_ref[...] = m_new

    def sel_mask(kt):
        rows = [jnp.broadcast_to(selneg_ref[pl.ds(kt * (KT // SEL_BLOCK) + b, 1), :], (SEL_BLOCK, QT))
                for b in range(KT // SEL_BLOCK)]
        mk = jnp.concatenate(rows, axis=0)
        return jnp.concatenate([mk] * GQA_R, axis=1)

    for g in range(N_KV):
        qs = jnp.concatenate([q_ref[0, :, (g * GQA_R + r) * LANES:(g * GQA_R + r + 1) * LANES]
                              for r in range(GQA_R)], axis=0)
        bc = biasc_ref[0, g]
        sc = lax.dot_general(ck_ref[0], qs, nt, preferred_element_type=F32) + bc
        e = jnp.exp(sc - jnp.max(sc, axis=0, keepdims=True))
        e = jnp.where(bc > 0.5 * NEG, e, 0.0)
        lsum = jnp.sum(e, axis=0, keepdims=True)
        pb = (e / jnp.where(lsum > 0.0, lsum, 1.0)).astype(BF16)
        o_c = jnp.dot(cvt_ref[0], pb, preferred_element_type=F32)
        p4 = jnp.concatenate([pb[:, r * QT:(r + 1) * QT] for r in range(GQA_R)], axis=0)
        imp = jnp.dot(ovt_ref[...], p4, preferred_element_type=F32)
        imp = jnp.where(forced, FORCE, imp)
        imp = jnp.where(future, NEG, imp)
        imp_ref[...] = imp

        def rank_body(ii, cnt):
            row = imp_ref[pl.ds(ii, 1), :]
            beats = (row > imp) | ((row == imp) & (jj > ii))
            return cnt + jnp.where(beats, 1.0, 0.0)

        cnt = lax.fori_loop(0, n_sel, rank_body, jnp.zeros((n_sel, QT), F32), unroll=8)
        selneg_ref[...] = jnp.where((cnt < N_SELECT) & jnp.logical_not(future), 0.0, NEG)
        online_step(qs, ksk_ref, ksvt_ref, i, near_ref[0, g] + sel_mask(i), True)

        @pl.when(i >= 1)
        def _():
            online_step(qs, ksk_ref, ksvt_ref, i - 1, near_ref[1, g] + sel_mask(i - 1), False)

        def far_body(kt, carry):
            online_step(qs, ksk_ref, ksvt_ref, kt, far_ref[g] + sel_mask(kt), False)
            return carry

        lax.fori_loop(0, jnp.maximum(i - 1, 0), far_body, 0)
        o_s = acc_ref[...] / l_ref[...]
        online_step(qs, kwk_ref, kwvt_ref, i, near_ref[0, g], True)

        @pl.when(i >= 1)
        def _():
            online_step(qs, kwk_ref, kwvt_ref, i - 1, near_ref[1, g], False)

        @pl.when(i >= 2)
        def _():
            online_step(qs, kwk_ref, kwvt_ref, i - 2, near_ref[2, g], False)

        o_w = acc_ref[...] / l_ref[...]
        for r in range(GQA_R):
            h = g * GQA_R + r
            rows = slice(g * HEAD_DIM, (g + 1) * HEAD_DIM)
            cols = slice(r * QT, (r + 1) * QT)
            outt_ref[h * HEAD_DIM:(h + 1) * HEAD_DIM, :] = (
                gsig[h:h + 1, :] * o_c[rows, cols]
                + gsig[N_HEADS + h:N_HEADS + h + 1, :] * o_s[rows, cols]
                + gsig[2 * N_HEADS + h:2 * N_HEADS + h + 1, :] * o_w[rows, cols])
    o_ref[0] = outt_ref[...].T.astype(BF16)


def _prompt_attn(q, gt, ck, cvt, ksk, ksvt, kwk, kwvt, tables, bsz, t_len):
    near, far, biasc, ovt4 = tables
    n_half = t_len // CMP_STRIDE
    n_sel = t_len // SEL_BLOCK
    nkt = t_len // KT
    per_b2 = lambda c: pl.BlockSpec((t_len, c), lambda b, i: (b, 0))
    vt_spec = pl.BlockSpec((nkt, LANES, KT), lambda b, i: (b, 0, 0))
    return pl.pallas_call(
        functools.partial(_prompt_attn_kernel, n_sel=n_sel),
        grid=(bsz, t_len // QT),
        in_specs=[pl.BlockSpec((1, QT, N_HEADS * LANES), lambda b, i: (b, i, 0)),
                  pl.BlockSpec((1, QT, LANES), lambda b, i: (b, i, 0)),
                  pl.BlockSpec((1, n_half, LANES), lambda b, i: (b, 0, 0)),
                  pl.BlockSpec((1, LANES, n_half), lambda b, i: (b, 0, 0)),
                  per_b2(LANES), vt_spec, per_b2(LANES), vt_spec,
                  pl.BlockSpec((1, N_KV, n_half, GQA_R * QT), lambda b, i: (i, 0, 0, 0)),
                  _const_spec(near.shape), _const_spec(far.shape), _const_spec(ovt4.shape)],
        out_specs=pl.BlockSpec((1, QT, NSA_W), lambda b, i: (b, i, 0)),
        out_shape=jax.ShapeDtypeStruct((bsz, t_len, NSA_W), BF16),
        scratch_shapes=[pltpu.VMEM((n_sel, QT), F32), pltpu.VMEM((n_sel, QT), F32),
                        pltpu.VMEM((1, GQA_R * QT), F32), pltpu.VMEM((1, GQA_R * QT), F32),
                        pltpu.VMEM((LANES, GQA_R * QT), F32), pltpu.VMEM((NSA_W, QT), F32)],
        compiler_params=_cparams("parallel", "arbitrary"),
        name="prompt_attn",
    )(q, gt, ck, cvt, ksk, ksvt, kwk, kwvt, biasc, near, far, ovt4)


def _oproj_kernel(x_ref, a_ref, b_ref, woa_ref, wob_ref, g_ref, wq_ref, x1_ref, q_ref):
    x1 = (x_ref[...] + jnp.dot(a_ref[...], woa_ref[...], preferred_element_type=F32)
          + jnp.dot(b_ref[...], wob_ref[...], preferred_element_type=F32))
    x1_ref[...] = x1
    h = _rms(x1, g_ref[...]).astype(BF16)
    q_ref[...] = jnp.dot(h, wq_ref[...], preferred_element_type=F32).astype(BF16)


def _oproj(x2d, a, b, w_oa, w_ob, g, w_xq):
    n = x2d.shape[0]
    row = lambda c: pl.BlockSpec((ROW_TILE, c), lambda i: (i, 0))
    hq = MEM_HEADS * MEM_HD
    return pl.pallas_call(
        _oproj_kernel,
        grid=(n // ROW_TILE,),
        in_specs=[row(D_MODEL), row(SGU_W), row(NSA_W), _const_spec(w_oa.shape), _const_spec(w_ob.shape),
                  _const_spec((1, D_MODEL)), _const_spec(w_xq.shape)],
        out_specs=[row(D_MODEL), row(hq)],
        out_shape=[jax.ShapeDtypeStruct((n, D_MODEL), F32), jax.ShapeDtypeStruct((n, hq), BF16)],
        compiler_params=_cparams("parallel"),
        name="oproj_xq",
    )(x2d, a, b, w_oa, w_ob, g, w_xq)


def _memkv_kernel(x_ref, g_ref, w_ref, o_ref, ob_ref):
    h = _rms(x_ref[...], g_ref[...]).astype(BF16)
    o = jnp.dot(h, w_ref[...], preferred_element_type=F32)
    o_ref[...] = o
    ob_ref[...] = o.astype(BF16)


def _memkv(mem2d, g, w):
    n = mem2d.shape[0]
    c = w.shape[1]
    row = lambda cc: pl.BlockSpec((ROW_TILE, cc), lambda i: (i, 0))
    return pl.pallas_call(
        _memkv_kernel,
        grid=(n // ROW_TILE,),
        in_specs=[row(D_MODEL), _const_spec((1, D_MODEL)), _const_spec(w.shape)],
        out_specs=[row(c), row(c)],
        out_shape=[jax.ShapeDtypeStruct((n, c), F32), jax.ShapeDtypeStruct((n, c), BF16)],
        compiler_params=_cparams("parallel"),
        name="memkv_proj",
    )(mem2d, g, w)


def _softmax_rows(s):
    e = jnp.exp(s - jnp.max(s, axis=-1, keepdims=True))
    return e / jnp.sum(e, axis=-1, keepdims=True)


def _xattn_tail(x1, o, wxo_ref, g_ref, wr_ref, br_ref, x2_ref, h_ref, te_ref, tg_ref):
    x2 = x1 + jnp.dot(o.astype(BF16), wxo_ref[...], preferred_element_type=F32)
    x2_ref[...] = x2
    hb = _rms(x2, g_ref[...]).astype(BF16)
    h_ref[...] = hb
    lt = lax.dot_general(wr_ref[...], hb, (((1,), (1,)), ((), ())), preferred_element_type=F32) + br_ref[...]
    eidx = lax.broadcasted_iota(jnp.int32, lt.shape, 0)
    tops, idxs = [], []
    for _ in range(TOP_K):
        m = jnp.max(lt, axis=0, keepdims=True)
        idx = jnp.min(jnp.where(lt == m, eidx, N_EXPERTS), axis=0, keepdims=True)
        tops.append(m)
        idxs.append(idx)
        lt = jnp.where(eidx == idx, -jnp.inf, lt)
    es = [jnp.exp(t - tops[0]) for t in tops]
    den = es[0] + es[1] + es[2] + es[3]
    te_ref[...] = jnp.concatenate(idxs, axis=0)
    tg_ref[...] = jnp.concatenate([e / den for e in es], axis=0)


def _xattn_prompt_kernel(x1_ref, q_ref, kv_ref, wxo_ref, g_ref, wr_ref, br_ref, x2_ref, h_ref, te_ref, tg_ref):
    outs = []
    for hh in range(MEM_HEADS):
        qh = q_ref[:, hh * MEM_HD:(hh + 1) * MEM_HD]
        kh = kv_ref[:, hh * MEM_HD:(hh + 1) * MEM_HD]
        vh = kv_ref[:, (MEM_HEADS + hh) * MEM_HD:(MEM_HEADS + hh + 1) * MEM_HD]
        s = lax.dot_general(qh, kh, (((1,), (1,)), ((), ())), preferred_element_type=F32) * (MEM_HD ** -0.5)
        outs.append(jnp.dot(_softmax_rows(s).astype(BF16), vh, preferred_element_type=F32))
    o = jnp.concatenate(outs, axis=1)
    _xattn_tail(x1_ref[...], o, wxo_ref, g_ref, wr_ref, br_ref, x2_ref, h_ref, te_ref, tg_ref)


def _xattn_sample_kernel(x1_ref, q_ref, kv_ref, wxo_ref, g_ref, wr_ref, br_ref, x2_ref, h_ref, te_ref, tg_ref,
                         o_scr, *, t_len):
    nb = q_ref.shape[0] // t_len
    qf = q_ref[...].astype(F32)
    for bb in range(nb):
        for hh in range(MEM_HEADS):
            qh = qf[bb * t_len:(bb + 1) * t_len, hh * MEM_HD:(hh + 1) * MEM_HD].astype(BF16)
            kh = kv_ref[bb, pl.ds(hh, MEM_LEN, stride=2 * MEM_HEADS), :].astype(BF16)
            vh = kv_ref[bb, pl.ds(MEM_HEADS + hh, MEM_LEN, stride=2 * MEM_HEADS), :].astype(BF16)
            s = lax.dot_general(qh, kh, (((1,), (1,)), ((), ())), preferred_element_type=F32) * (MEM_HD ** -0.5)
            o_scr[bb * t_len:(bb + 1) * t_len, hh * MEM_HD:(hh + 1) * MEM_HD] = jnp.dot(
                _softmax_rows(s).astype(BF16), vh, preferred_element_type=F32)
    _xattn_tail(x1_ref[...], o_scr[...], wxo_ref, g_ref, wr_ref, br_ref, x2_ref, h_ref, te_ref, tg_ref)


XS_BATCH = 16


def _xattn(x1, q, kv, w_xo, g, w_rt, b_r, t_len, prompt):
    n = x1.shape[0]
    hq = MEM_HEADS * MEM_HD
    if prompt:
        rows = ROW_TILE
        per_b = t_len // rows
        kv_spec = pl.BlockSpec((MEM_LEN, 2 * hq), lambda i: (i // per_b, 0))
        kern = _xattn_prompt_kernel
        scratch = []
    else:
        rows = XS_BATCH * t_len
        kv_spec = pl.BlockSpec((XS_BATCH, MEM_LEN * 2 * MEM_HEADS, MEM_HD), lambda i: (i, 0, 0))
        kern = functools.partial(_xattn_sample_kernel, t_len=t_len)
        scratch = [pltpu.VMEM((rows, hq), F32)]
    row = lambda c: pl.BlockSpec((rows, c), lambda i: (i, 0))
    col = pl.BlockSpec((TOP_K, rows), lambda i: (0, i))
    return pl.pallas_call(
        kern,
        grid=(n // rows,),
        in_specs=[row(D_MODEL), row(hq), kv_spec, _const_spec(w_xo.shape), _const_spec((1, D_MODEL)),
                  _const_spec(w_rt.shape), _const_spec(b_r.shape)],
        out_specs=[row(D_MODEL), row(D_MODEL), col, col],
        out_shape=[jax.ShapeDtypeStruct((n, D_MODEL), F32), jax.ShapeDtypeStruct((n, D_MODEL), BF16),
                   jax.ShapeDtypeStruct((TOP_K, n), jnp.int32), jax.ShapeDtypeStruct((TOP_K, n), F32)],
        scratch_shapes=scratch,
        compiler_params=_cparams("parallel"),
        name="xattn_router_prompt" if prompt else "xattn_router_sample",
    )(x1, q, kv, w_xo, g, w_rt, b_r)


MOE_BLOCK = 256


def _expert_kernel(blk_e_ref, n_used_ref, x_ref, wgu_ref, bgu_ref, wd_ref, bd_ref, y_ref):
    @pl.when(pl.program_id(0) < n_used_ref[0])
    def _():
        gu = jnp.dot(x_ref[...], wgu_ref[0], preferred_element_type=F32) + bgu_ref[0]
        glu = jnp.minimum(gu[:, :D_FF], SWIGLU_LIMIT)
        lin = jnp.clip(gu[:, D_FF:], -SWIGLU_LIMIT, SWIGLU_LIMIT)
        hdn = glu * jax.nn.sigmoid(SWIGLU_ALPHA * glu) * (lin + 1.0)
        y_ref[...] = jnp.dot(hdn.astype(BF16), wd_ref[0], preferred_element_type=F32) + bd_ref[0]

    @pl.when(pl.program_id(0) >= n_used_ref[0])
    def _():
        y_ref[...] = jnp.zeros_like(y_ref)


def _experts(blk_e, n_used, xs, w_gu, b_gu, w_down, b_down):
    n_rows = xs.shape[0]
    n_blocks = n_rows // MOE_BLOCK
    grid_spec = pltpu.PrefetchScalarGridSpec(
        num_scalar_prefetch=2,
        grid=(n_blocks,),
        in_specs=[pl.BlockSpec((MOE_BLOCK, D_MODEL), lambda i, be, nu: (i, 0)),
                  pl.BlockSpec((1, D_MODEL, 2 * D_FF), lambda i, be, nu: (be[i], 0, 0)),
                  pl.BlockSpec((1, 1, 2 * D_FF), lambda i, be, nu: (be[i], 0, 0)),
                  pl.BlockSpec((1, D_FF, D_MODEL), lambda i, be, nu: (be[i], 0, 0)),
                  pl.BlockSpec((1, 1, D_MODEL), lambda i, be, nu: (be[i], 0, 0))],
        out_specs=pl.BlockSpec((MOE_BLOCK, D_MODEL), lambda i, be, nu: (i, 0)),
    )
    return pl.pallas_call(
        _expert_kernel,
        grid_spec=grid_spec,
        out_shape=jax.ShapeDtypeStruct((n_rows, D_MODEL), F32),
        compiler_params=_cparams("arbitrary"),
        name="moe_experts",
    )(blk_e, n_used, xs, w_gu, b_gu, w_down, b_down)


def _moe_dispatch(te):
    n = te.shape[1]
    nk = TOP_K * n
    n_blocks = -(-(nk + N_EXPERTS * (MOE_BLOCK - 1)) // MOE_BLOCK)
    flat_e = te.reshape(nk)
    sorted_e, order = lax.sort_key_val(flat_e, jnp.arange(nk, dtype=jnp.int32))
    counts = jnp.sum(flat_e[:, None] == jnp.arange(N_EXPERTS)[None, :], axis=0, dtype=jnp.int32)
    padded = ((counts + MOE_BLOCK - 1) // MOE_BLOCK) * MOE_BLOCK
    start = jnp.cumsum(counts) - counts
    pend = jnp.cumsum(padded)
    pstart = pend - padded
    dest = (pstart[sorted_e] + (jnp.arange(nk) - start[sorted_e])).astype(jnp.int32)
    inv = lax.sort_key_val(order, dest)[1].reshape(TOP_K, n)
    blk_row0 = jnp.arange(n_blocks, dtype=jnp.int32) * MOE_BLOCK
    n_used = (pend[-1] // MOE_BLOCK).astype(jnp.int32).reshape(1)
    blk_e = jnp.minimum(jnp.sum(pend[None, :] <= blk_row0[:, None], axis=1), N_EXPERTS - 1).astype(jnp.int32)
    base = jnp.where(blk_row0 < pend[-1], start[blk_e] + blk_row0 - pstart[blk_e], 0).astype(jnp.int32)
    tok_sorted = jnp.pad(order % n, (0, MOE_BLOCK))
    row_tok = jax.vmap(lambda b0: lax.dynamic_slice(tok_sorted, (b0,), (MOE_BLOCK,)))(base).reshape(-1)
    return row_tok, inv, blk_e, n_used


def _final_kernel(x_ref, y_ref, g_ref, fn_ref, o_ref):
    x = x_ref[...]
    for k in range(TOP_K):
        x = x + y_ref[k] * g_ref[:, k:k + 1]
    o_ref[...] = _rms(x, fn_ref[...])


def _final(x2, yk, gates, fnorm):
    n = x2.shape[0]
    rows = 256
    return pl.pallas_call(
        _final_kernel,
        grid=(n // rows,),
        in_specs=[pl.BlockSpec((rows, D_MODEL), lambda i: (i, 0)),
                  pl.BlockSpec((TOP_K, rows, D_MODEL), lambda i: (0, i, 0)),
                  pl.BlockSpec((rows, TOP_K), lambda i: (i, 0)), _const_spec((1, D_MODEL))],
        out_specs=pl.BlockSpec((rows, D_MODEL), lambda i: (i, 0)),
        out_shape=jax.ShapeDtypeStruct((n, D_MODEL), F32),
        compiler_params=_cparams("parallel"),
        name="moe_combine_final_norm",
    )(x2, yk, gates, fnorm)


def _page_copy(pool_ref, pt_ref, buf_ref, sem_ref, b, p, slot, n_pages, rows):
    return pltpu.make_async_copy(pool_ref.at[pt_ref[b * n_pages + p]],
                                 buf_ref.at[slot, pl.ds(p * rows, rows)], sem_ref.at[slot])


def _fetch_pages(pool_ref, pt_ref, buf_ref, sem_ref, b, slot, n_pages, rows):
    def body(p, c):
        _page_copy(pool_ref, pt_ref, buf_ref, sem_ref, b, p, slot, n_pages, rows).start()
        return c
    lax.fori_loop(0, n_pages, body, 0)


def _wait_pages(pool_ref, pt_ref, buf_ref, sem_ref, b, slot, n_pages, rows):
    def body(p, c):
        _page_copy(pool_ref, pt_ref, buf_ref, sem_ref, b, p, slot, n_pages, rows).wait()
        return c
    lax.fori_loop(0, n_pages, body, 0)


def _compress_sample_kernel(pt_ref, pool_ref, hi_ref, w_ref, b_ref, ck_ref, cv_ref, buf_ref, sem_ref, tok_ref,
                            *, n_pages, page_rows):
    b = pl.program_id(0)
    nb = pl.num_programs(0)
    slot = lax.rem(b, 2)
    rows = 2 * LANES
    n_half = n_pages * page_rows // CMP_STRIDE

    @pl.when(b == 0)
    def _():
        _fetch_pages(pool_ref, pt_ref, buf_ref, sem_ref, b, slot, n_pages, rows)

    @pl.when(b + 1 < nb)
    def _():
        _fetch_pages(pool_ref, pt_ref, buf_ref, sem_ref, b + 1, 1 - slot, n_pages, rows)

    _wait_pages(pool_ref, pt_ref, buf_ref, sem_ref, b, slot, n_pages, rows)

    def to_token_major(p, carry):
        for c in range(2):
            src = buf_ref[slot, pl.ds(pl.multiple_of(p * rows + c * LANES, LANES), LANES), :]
            tok_ref[c, pl.ds(pl.multiple_of(p * page_rows, page_rows), page_rows), :] = src.T
        return carry

    lax.fori_loop(0, n_pages, to_token_major, 0)
    last = lax.broadcasted_iota(jnp.int32, (n_half, LANES), 0) == n_half - 1
    for c, out in enumerate((ck_ref, cv_ref)):
        acc = jnp.zeros((n_half, 2 * LANES), F32)
        for r in range(CMP_STRIDE):
            xr = tok_ref[c, pl.ds(r, n_half, stride=CMP_STRIDE), :].astype(BF16)
            acc = acc + jnp.dot(xr, w_ref[c, r], preferred_element_type=F32)
        hi_next = pltpu.roll(acc[:, LANES:], n_half - 1, 0)
        hi_next = jnp.where(last, hi_ref[0, c:c + 1, :], hi_next)
        out[0] = (acc[:, :LANES] + hi_next + b_ref[c:c + 1, :]).astype(BF16)


def _hi_new_kernel(ak_ref, av_ref, w_ref, o_ref):
    for c, a_ref in enumerate((ak_ref, av_ref)):
        o_ref[:, c, :] = jnp.dot(a_ref[...].astype(BF16), w_ref[c], preferred_element_type=F32)


def _hi_new(kck_new, kcv_new, w_cmp, n_batch, t_len):
    w_hi = w_cmp[:, :t_len, :, LANES:].reshape(2, t_len * LANES, LANES)
    ak = kck_new.reshape(n_batch, t_len * LANES)
    av = kcv_new.reshape(n_batch, t_len * LANES)
    return pl.pallas_call(
        _hi_new_kernel,
        out_shape=jax.ShapeDtypeStruct((n_batch, 2, LANES), F32),
        name="compress_new_tokens",
    )(ak, av, w_hi)


def _compress_sample(page_table_flat, pool, hi_new, w_cmp, b_cmp, n_batch, n_pages, page_rows):
    assert page_rows == LANES
    n_half = n_pages * page_rows // CMP_STRIDE
    grid_spec = pltpu.PrefetchScalarGridSpec(
        num_scalar_prefetch=1,
        grid=(n_batch,),
        in_specs=[pl.BlockSpec(memory_space=pl.ANY),
                  pl.BlockSpec((1, 2, LANES), lambda b, pt: (b, 0, 0)),
                  pl.BlockSpec(w_cmp.shape, lambda b, pt: (0, 0, 0, 0)),
                  pl.BlockSpec(b_cmp.shape, lambda b, pt: (0, 0))],
        out_specs=[pl.BlockSpec((1, n_half, LANES), lambda b, pt: (b, 0, 0)),
                   pl.BlockSpec((1, n_half, LANES), lambda b, pt: (b, 0, 0))],
        scratch_shapes=[pltpu.VMEM((2, n_pages * 2 * LANES, page_rows), F32), pltpu.SemaphoreType.DMA((2,)),
                        pltpu.VMEM((2, n_pages * page_rows, LANES), F32)],
    )
    return pl.pallas_call(
        functools.partial(_compress_sample_kernel, n_pages=n_pages, page_rows=page_rows),
        grid_spec=grid_spec,
        out_shape=[jax.ShapeDtypeStruct((n_batch, n_half, LANES), BF16)] * 2,
        compiler_params=_cparams("arbitrary"),
        name="compress_sample",
    )(page_table_flat, pool, hi_new, w_cmp, b_cmp)


N_SEL_PAD = 256


def _sample_tables(rel_bias, past, t_len):
    n_cmp = past // CMP_STRIDE
    n_sel = past // SEL_BLOCK + 1
    y = np.tile(np.arange(t_len), N_HEADS)[:, None]
    hh = np.repeat(np.arange(N_HEADS), t_len)[:, None]

    def tab(dist, valid):
        return _head_table(rel_bias, dist, valid, hh)

    dc = past + y - (np.arange(n_cmp)[None, :] * CMP_STRIDE + CMP_LEN - 1)
    biasc = tab(dc, dc >= 0)
    dp = past + y - np.arange(past)[None, :]
    biasp = tab(dp, dp >= 0)
    dn = y - np.arange(t_len)[None, :]
    biasn = tab(dn, dn >= 0)
    dw = WINDOW + y - np.arange(WINDOW)[None, :]
    biasw = tab(dw, dw < WINDOW)
    cs = np.arange(n_cmp)[:, None] * CMP_STRIDE
    ss = np.arange(N_SEL_PAD)[None, :] * SEL_BLOCK
    ov = ((cs < ss + SEL_BLOCK) & (cs + CMP_LEN > ss) & (np.arange(N_SEL_PAD)[None, :] < n_sel))
    ov4 = jnp.asarray(np.tile(ov.astype(np.float32), (GQA_R, 1)), BF16)
    expand = (np.arange(past)[None, :] // SEL_BLOCK == np.arange(past // SEL_BLOCK)[:, None])
    return biasc, biasp, biasn, biasw, ov4, jnp.asarray(expand.astype(np.float32), BF16)


def _sample_attn_kernel(pt_ref, q_ref, gt_ref, ck_ref, cv_ref, pool_ref, win_ref, ksk_ref, ksv_ref, kwk_ref,
                        kwv_ref, biasc_ref, biasp_ref, biasn_ref, biasw_ref, ov_ref, exp_ref, o_ref,
                        buf_ref, sem_ref, s_ref, *, n_pages, page_tokens, t_len, cur_block):
    b = pl.program_id(0)
    nb = pl.num_programs(0)
    slot = lax.rem(b, 2)
    rows = 2 * LANES
    nt = (((1,), (1,)), ((), ()))
    n_rows = N_HEADS * t_len

    @pl.when(b == 0)
    def _():
        _fetch_pages(pool_ref, pt_ref, buf_ref, sem_ref, b, slot, n_pages, rows)

    @pl.when(b + 1 < nb)
    def _():
        _fetch_pages(pool_ref, pt_ref, buf_ref, sem_ref, b + 1, 1 - slot, n_pages, rows)

    qb = q_ref[0]
    bc = biasc_ref[...]
    sc = lax.dot_general(qb, ck_ref[0], nt, preferred_element_type=F32) + bc
    e = jnp.exp(sc - jnp.max(sc, axis=1, keepdims=True))
    e = jnp.where(bc > 0.5 * NEG, e, 0.0)
    lsum = jnp.sum(e, axis=1, keepdims=True)
    pb = (e / jnp.where(lsum > 0.0, lsum, 1.0)).astype(BF16)
    o_c = jnp.dot(pb, cv_ref[0], preferred_element_type=F32)
    p4 = jnp.concatenate(
        [jnp.concatenate([pb[(g * GQA_R + r) * t_len:(g * GQA_R + r + 1) * t_len, :] for r in range(GQA_R)], axis=1)
         for g in range(N_KV)], axis=0)
    imp = jnp.dot(p4, ov_ref[...], preferred_element_type=F32)
    lane = lax.broadcasted_iota(jnp.int32, imp.shape, 1)
    imp = jnp.where((lane == 0) | (lane == cur_block) | (lane == cur_block - 1), FORCE, imp)
    imp = jnp.where(lane > cur_block, NEG, imp)

    def rank_body(k, cnt):
        other = pltpu.roll(imp, k, 1)
        beats = (other > imp) | ((other == imp) & (lane >= k))
        return cnt + jnp.where(beats, 1.0, 0.0)

    cnt = lax.fori_loop(1, N_SEL_PAD, rank_body, jnp.zeros(imp.shape, F32), unroll=15)
    sel = jnp.where((cnt < N_SELECT) & (lane <= cur_block), 1.0, 0.0)
    sel_rows = jnp.concatenate([sel[g * t_len:(g + 1) * t_len, :] for g in range(N_KV) for _ in range(GQA_R)], axis=0)
    n_past_blocks = n_pages * page_tokens // SEL_BLOCK
    keep = jnp.dot(sel_rows[:, :n_past_blocks].astype(BF16), exp_ref[...], preferred_element_type=F32)

    def softmax_pair(s_old, s_new):
        m = jnp.maximum(jnp.max(s_old, axis=1, keepdims=True), jnp.max(s_new, axis=1, keepdims=True))
        p_old = jnp.exp(s_old - m)
        p_new = jnp.exp(s_new - m)
        inv = 1.0 / (jnp.sum(p_old, axis=1, keepdims=True) + jnp.sum(p_new, axis=1, keepdims=True))
        return p_old, p_new, inv

    _wait_pages(pool_ref, pt_ref, buf_ref, sem_ref, b, slot, n_pages, rows)
    for p in range(n_pages):
        kt_page = buf_ref[slot, p * rows:p * rows + LANES, :].astype(BF16)
        s_ref[:, p * page_tokens:(p + 1) * page_tokens] = jnp.dot(qb, kt_page, preferred_element_type=F32)
    s_past = jnp.where(keep > 0.5, s_ref[...] + biasp_ref[...], NEG)
    bn = biasn_ref[...]
    s_new = lax.dot_general(qb, ksk_ref[...].astype(BF16), nt, preferred_element_type=F32) + bn
    p_past, p_new, inv = softmax_pair(s_past, s_new)
    p_past = p_past.astype(BF16)
    o_s = jnp.dot(p_new.astype(BF16), ksv_ref[...].astype(BF16), preferred_element_type=F32)
    for p in range(n_pages):
        vt_page = buf_ref[slot, p * rows + LANES:(p + 1) * rows, :].astype(BF16)
        o_s = o_s + lax.dot_general(p_past[:, p * page_tokens:(p + 1) * page_tokens], vt_page, nt,
                                    preferred_element_type=F32)
    o_s = o_s * inv
    s_win = jnp.dot(qb, win_ref[0, :LANES, :].astype(BF16), preferred_element_type=F32) + biasw_ref[...]
    s_wnew = lax.dot_general(qb, kwk_ref[...].astype(BF16), nt, preferred_element_type=F32) + bn
    p_win, p_wnew, inv_w = softmax_pair(s_win, s_wnew)
    o_w = (lax.dot_general(p_win.astype(BF16), win_ref[0, LANES:, :].astype(BF16), nt, preferred_element_type=F32)
           + jnp.dot(p_wnew.astype(BF16), kwv_ref[...].astype(BF16), preferred_element_type=F32)) * inv_w
    gs = jax.nn.sigmoid(gt_ref[0])
    o = gs[:, 0:1] * o_c + gs[:, 1:2] * o_s + gs[:, 2:3] * o_w
    row = lax.broadcasted_iota(jnp.int32, (n_rows, HEAD_DIM), 0)
    o_ref[0] = jnp.where(row < GQA_R * t_len, o[:, :HEAD_DIM], o[:, HEAD_DIM:])


def _sample_attn(page_table_flat, q, gt, ck, cv, pool, win, ksk, ksv, kwk, kwv, tables, n_batch, n_pages, page_tokens,
                 t_len):
    biasc, biasp, biasn, biasw, ov4, expand = tables
    past = n_pages * page_tokens
    n_rows = N_HEADS * t_len
    cur_block = past // SEL_BLOCK
    assert past % SEL_BLOCK == 0 and t_len <= SEL_BLOCK and cur_block < N_SEL_PAD
    new_spec = pl.BlockSpec((t_len, LANES), lambda b, pt: (b, 0))
    cst = lambda a: pl.BlockSpec(a.shape, lambda b, pt: (0,) * a.ndim)
    grid_spec = pltpu.PrefetchScalarGridSpec(
        num_scalar_prefetch=1,
        grid=(n_batch,),
        in_specs=[pl.BlockSpec((1, n_rows, LANES), lambda b, pt: (b, 0, 0)),
                  pl.BlockSpec((1, n_rows, 3), lambda b, pt: (b, 0, 0)),
                  pl.BlockSpec((1, past // CMP_STRIDE, LANES), lambda b, pt: (b, 0, 0)),
                  pl.BlockSpec((1, past // CMP_STRIDE, LANES), lambda b, pt: (b, 0, 0)),
                  pl.BlockSpec(memory_space=pl.ANY),
                  pl.BlockSpec((1, 2 * LANES, WINDOW), lambda b, pt: (b, 0, 0)),
                  new_spec, new_spec, new_spec, new_spec,
                  cst(biasc), cst(biasp), cst(biasn), cst(biasw), cst(ov4), cst(expand)],
        out_specs=pl.BlockSpec((1, n_rows, HEAD_DIM), lambda b, pt: (b, 0, 0)),
        scratch_shapes=[pltpu.VMEM((2, n_pages * 2 * LANES, page_tokens), F32), pltpu.SemaphoreType.DMA((2,)),
                        pltpu.VMEM((n_rows, past), F32)],
    )
    return pl.pallas_call(
        functools.partial(_sample_attn_kernel, n_pages=n_pages, page_tokens=page_tokens, t_len=t_len,
                          cur_block=cur_block),
        grid_spec=grid_spec,
        out_shape=jax.ShapeDtypeStruct((n_batch, n_rows, HEAD_DIM), F32),
        compiler_params=_cparams("arbitrary"),
        name="sample_attn",
    )(page_table_flat, q, gt, ck, cv, pool, win, ksk, ksv, kwk, kwv, biasc, biasp, biasn, biasw, ov4, expand)


def kernel(x_prompt, x_sample, mem_prompt, cache_cmp_kv, cache_slc_kv, cache_win_kv, cache_mem_kv, page_table, norm_mix, w_in, sgu_norm_g, sgu_norm_b, sgu_w, sgu_b, cmp_w, cmp_b, w_o, rel_bias, norm_x, norm_mem, w_xq, w_mem_kv, w_xo, norm_ffn, w_router, b_router, w_gu, b_gu, w_down, b_down, final_norm):
    assert norm_mix.shape[0] == 1, "single-layer trunk"
    bsz, t_p, _ = x_prompt.shape
    dbs, t_s, _ = x_sample.shape
    n_phys, page_tokens = cache_cmp_kv.shape[1], cache_cmp_kv.shape[2]
    n_pages = page_table.shape[1]
    past = n_pages * page_tokens
    wb = cache_win_kv.shape[2]
    assert wb == WINDOW and t_p >= WINDOW and t_p >= CHUNK and t_s <= CHUNK
    n_p, n_s = bsz * t_p, dbs * t_s
    row1 = lambda v: v.reshape(1, -1)

    w_all = _build_w_in(w_in[0])
    mix_p, mixb_p = _build_sgu_mix(sgu_w[0], sgu_b[0], CHUNK, CHUNK)
    mix_s, mixb_s = _build_sgu_mix(sgu_w[0], sgu_b[0], t_s, ROW_TILE)
    w_cmp = _build_cmp_w(cmp_w[0])
    b_cmp = cmp_b[0].reshape(2, LANES)
    w_oa, w_ob = w_o[0, :SGU_W].astype(BF16), w_o[0, SGU_W:].astype(BF16)
    w_xq_b, w_xo_b = w_xq[0].astype(BF16), w_xo[0].astype(BF16)
    w_rt = w_router[0].T.astype(BF16)
    b_rt = b_router[0].reshape(N_EXPERTS, 1)
    in_args = (row1(norm_mix[0]), w_all, row1(sgu_norm_g[0]), row1(sgu_norm_b[0]))
    tail_args = (w_xo_b, row1(norm_ffn[0]), w_rt, b_rt)

    xp = x_prompt.reshape(n_p, D_MODEL)
    (a_p, v_p, q_p, gt_p, kck, kcv, ksk, ksv, kwk, kwv, kskb, ksvt, kwkb, kwvt) = _inproj(
        xp, *in_args, mix_p, mixb_p, True)
    ck_p, cvt_p = _compress_prompt(kck, kcv, w_cmp, b_cmp, bsz, t_p)
    b_p = _prompt_attn(q_p.reshape(bsz, t_p, N_HEADS * LANES), gt_p.reshape(bsz, t_p, LANES), ck_p, cvt_p,
                       kskb, ksvt, kwkb, kwvt, _attn_tables(rel_bias, t_p), bsz, t_p)
    x1_p, qm_p = _oproj(xp, a_p, b_p.reshape(n_p, NSA_W), w_oa, w_ob, row1(norm_x[0]), w_xq_b)
    memkv_f, memkv_b = _memkv(mem_prompt.reshape(bsz * MEM_LEN, D_MODEL), row1(norm_mem[0]), w_mem_kv[0].astype(BF16))
    x2_p, h_p, te_p, tg_p = _xattn(x1_p, qm_p, memkv_b, *tail_args, t_p, True)

    xs = x_sample.reshape(n_s, D_MODEL)
    (a_s, v_s, q_s, gt_s, kck_s, kcv_s, ksk_s, ksv_s, kwk_s, kwv_s) = _inproj(xs, *in_args, mix_s, mixb_s, False)
    pt_flat = page_table.reshape(-1)
    stored = lambda c: c[0].transpose(0, 2, 3, 4, 1).reshape(c.shape[1], 2 * LANES, c.shape[2])
    pool_c, pool_s = stored(cache_cmp_kv), stored(cache_slc_kv)
    win_t = cache_win_kv[0].transpose(0, 2, 3, 4, 1).reshape(dbs, 2 * LANES, wb)
    hi_new = _hi_new(kck_s, kcv_s, w_cmp, dbs, t_s)
    ck_s, cv_s = _compress_sample(pt_flat, pool_c, hi_new, w_cmp, b_cmp, dbs, n_pages, page_tokens)
    q_sb = q_s.reshape(dbs, t_s, N_HEADS, LANES).transpose(0, 2, 1, 3).reshape(dbs, N_HEADS * t_s, LANES)
    gt_sb = gt_s[:, :3 * N_HEADS].reshape(dbs, t_s, 3, N_HEADS).transpose(0, 3, 1, 2).reshape(dbs, N_HEADS * t_s, 3)
    bo_s = _sample_attn(pt_flat, q_sb, gt_sb, ck_s, cv_s, pool_s, win_t, ksk_s, ksv_s, kwk_s, kwv_s,
                        _sample_tables(rel_bias, past, t_s), dbs, n_pages, page_tokens, t_s)
    b_s = bo_s.reshape(dbs, N_HEADS, t_s, HEAD_DIM).transpose(0, 2, 1, 3).reshape(n_s, NSA_W).astype(BF16)
    x1_s, qm_s = _oproj(xs, a_s, b_s, w_oa, w_ob, row1(norm_x[0]), w_xq_b)
    memkv_s = cache_mem_kv[0].reshape(dbs, MEM_LEN * 2 * MEM_HEADS, MEM_HD)
    x2_s, h_s, te_s, tg_s = _xattn(x1_s, qm_s, memkv_s, *tail_args, t_s, False)

    row_tok, inv, blk_e, n_used = _moe_dispatch(jnp.concatenate([te_p, te_s], axis=1))
    xs_sorted = jnp.concatenate([h_p, h_s], axis=0)[row_tok]
    y = _experts(blk_e, n_used, xs_sorted, w_gu[0].astype(BF16), b_gu[0][:, None, :],
                 w_down[0].astype(BF16), b_down[0][:, None, :])
    fn = row1(final_norm)
    y_prompt = _final(x2_p, y[inv[:, :n_p]], tg_p.T, fn).reshape(bsz, t_p, D_MODEL)
    y_sample = _final(x2_s, y[inv[:, n_p:]], tg_s.T, fn).reshape(dbs, t_s, D_MODEL)

    kv6 = lambda k, v, b, t: jnp.concatenate([k, v], axis=1).reshape(1, b, t, 2, N_KV, HEAD_DIM)
    kw_p = kv6(kwk, kwv, bsz, t_p)
    kw_s = kv6(kwk_s, kwv_s, dbs, t_s)
    win_s = jnp.concatenate([cache_win_kv, kw_s], axis=2)[:, :, -wb:]
    return (y_prompt, y_sample, kv6(kck, kcv, bsz, t_p), kv6(ksk, ksv, bsz, t_p), kw_p[:, :, -WINDOW:],
            memkv_f.reshape(1, bsz, MEM_LEN, 2, MEM_HEADS, MEM_HD), v_p.reshape(1, bsz, t_p, SGU_W)[:, :, -CHUNK:],
            kv6(kck_s, kcv_s, dbs, t_s), kv6(ksk_s, ksv_s, dbs, t_s), win_s, v_s.reshape(1, dbs, t_s, SGU_W))
```

```python
import functools
import math

import numpy as np
import jax
import jax.numpy as jnp
from jax import lax
from jax.experimental import pallas as pl
from jax.experimental.pallas import tpu as pltpu

D_MODEL = 1024
SGU_W = 512
SGU_GROUPS = 4
SGU_GW = 128
CHUNK = 128
NSA_W = 512
HEAD_DIM = 64
N_HEADS = 8
N_KV = 2
GQA_R = 4
CMP_LEN = 32
CMP_STRIDE = 16
SEL_BLOCK = 64
N_SELECT = 16
WINDOW = 512
KV_W = 256
N_BUCKETS = 32
MAX_EXACT = 16
MAX_DIST = 128
MEM_LEN = 256
MEM_HEADS = 4
MEM_HD = 128
N_EXPERTS = 32
TOP_K = 4
D_FF = 1024
SWIGLU_LIMIT = 7.0
SWIGLU_ALPHA = 1.702
EPS = 1e-6
NEG = -1e30
FORCE = 1e9

LANES = 128
VMEM_LIMIT = 56 * 1024 * 1024

ROW_TILE = 512
QT = 256
KT = 256
CHAIN_HEADS = 4

F32 = jnp.float32
BF16 = jnp.bfloat16


def _cparams(*sem):
    return pltpu.CompilerParams(dimension_semantics=sem, vmem_limit_bytes=VMEM_LIMIT)


def _const_spec(shape):
    nd = len(shape)
    return pl.BlockSpec(shape, lambda *_: (0,) * nd)


def _rms(x, g):
    return x * lax.rsqrt(jnp.mean(x * x, axis=-1, keepdims=True) + EPS) * g


def _rel_bucket_np(dist):
    n = np.maximum(dist, 0)
    nf = np.maximum(n, 1).astype(np.float32)
    large = MAX_EXACT + (np.log(nf / np.float32(MAX_EXACT)) / np.float32(math.log(MAX_DIST / MAX_EXACT))
                         * np.float32(N_BUCKETS - MAX_EXACT)).astype(np.int32)
    large = np.minimum(large, N_BUCKETS - 1)
    return np.where(n < MAX_EXACT, n, large).astype(np.int32)


C_U, C_V, C_Q, C_KC, C_GT, C_END = 0, 512, 1024, 2048, 2816, 2944


def _inproj_kernel(x_ref, g_ref, w_ref, lng_ref, lnb_ref, mix_ref, mixb_ref, *out_refs, mix_block, attn_extras):
    a_ref, v_ref, q_ref, gt_ref = out_refs[:4]
    kv_refs = out_refs[4:10]
    x = x_ref[...]
    h = _rms(x, g_ref[...]).astype(BF16)

    def proj(lo, hi):
        return jnp.dot(h, w_ref[:, lo:hi], preferred_element_type=F32)

    q_ref[...] = proj(C_Q, C_KC).astype(BF16)
    gt_ref[...] = proj(C_GT, C_END)
    kv = [proj(C_KC + LANES * j, C_KC + LANES * (j + 1)) for j in range(6)]
    for r, val in zip(kv_refs, kv):
        r[...] = val
    if attn_extras:
        kskb_ref, ksvt_ref, kwkb_ref, kwvt_ref = out_refs[10:14]
        kskb_ref[...] = kv[2].astype(BF16)
        kwkb_ref[...] = kv[4].astype(BF16)
        for vt_ref, val in ((ksvt_ref, kv[3]), (kwvt_ref, kv[5])):
            vt = val.T.astype(BF16)
            for j in range(val.shape[0] // KT):
                vt_ref[j] = vt[:, j * KT:(j + 1) * KT]

    u = jax.nn.gelu(proj(C_U, C_V))
    v = jax.nn.gelu(proj(C_V, C_Q))
    mu = jnp.mean(v, axis=-1, keepdims=True)
    var = jnp.mean(jnp.square(v - mu), axis=-1, keepdims=True)
    v = (v - mu) * lax.rsqrt(var + EPS) * lng_ref[...] + lnb_ref[...]
    v_ref[...] = v
    vb = v.astype(BF16)
    rows = x.shape[0]
    for blk in range(rows // mix_block):
        r0 = blk * mix_block
        for g in range(SGU_GROUPS):
            c0 = g * SGU_GW
            mixed = jnp.dot(mix_ref[g], vb[r0:r0 + mix_block, c0:c0 + SGU_GW],
                            preferred_element_type=F32) + mixb_ref[:, c0:c0 + SGU_GW]
            a_ref[r0:r0 + mix_block, c0:c0 + SGU_GW] = (
                u[r0:r0 + mix_block, c0:c0 + SGU_GW] * mixed).astype(BF16)


def _inproj(x2d, g, w_all, ln_g, ln_b, mix, mixb, attn_extras):
    n = x2d.shape[0]
    mb = mix.shape[1]
    row = lambda c: pl.BlockSpec((ROW_TILE, c), lambda i: (i, 0))
    outs = [(SGU_W, BF16), (SGU_W, F32), (N_HEADS * LANES, BF16), (LANES, F32)] + [(LANES, F32)] * 6
    out_specs = [row(c) for c, _ in outs]
    out_shape = [jax.ShapeDtypeStruct((n, c), dt) for c, dt in outs]
    if attn_extras:
        tiles = ROW_TILE // KT
        vt_spec = pl.BlockSpec((tiles, LANES, KT), lambda i: (i, 0, 0))
        vt_shape = jax.ShapeDtypeStruct((n // KT, LANES, KT), BF16)
        out_specs += [row(LANES), vt_spec, row(LANES), vt_spec]
        out_shape += [jax.ShapeDtypeStruct((n, LANES), BF16), vt_shape,
                      jax.ShapeDtypeStruct((n, LANES), BF16), vt_shape]
    return pl.pallas_call(
        functools.partial(_inproj_kernel, mix_block=mb, attn_extras=attn_extras),
        grid=(n // ROW_TILE,),
        in_specs=[row(D_MODEL), _const_spec((1, D_MODEL)), _const_spec(w_all.shape),
                  _const_spec((1, SGU_W)), _const_spec((1, SGU_W)), _const_spec(mix.shape),
                  _const_spec(mixb.shape)],
        out_specs=out_specs,
        out_shape=out_shape,
        compiler_params=_cparams("parallel"),
        name="inproj_sgu",
    )(x2d, g, w_all, ln_g, ln_b, mix, mixb)


def _build_w_in(w_in):
    u = w_in[:, 0:512]
    v = w_in[:, 512:1024]
    q = w_in[:, 1024:1536].reshape(D_MODEL, N_KV, GQA_R, HEAD_DIM) * (HEAD_DIM ** -0.5)
    qp = jnp.zeros((D_MODEL, N_KV, GQA_R, N_KV, HEAD_DIM), F32)
    for g in range(N_KV):
        qp = qp.at[:, g, :, g, :].set(q[:, g])
    qp = qp.reshape(D_MODEL, N_HEADS * LANES)
    kv = w_in[:, 1536:2304]
    gt = jnp.pad(w_in[:, 2304:2328], ((0, 0), (0, LANES - 3 * N_HEADS)))
    return jnp.concatenate([u, v, qp, kv, gt], axis=1).astype(BF16)


def _build_sgu_mix(sgu_w, sgu_b, chunk, mix_block):
    tri = jnp.tril(jnp.ones((chunk, chunk), bool))
    w = jnp.where(tri, sgu_w[:, :chunk, :chunk], 0.0)
    reps = mix_block // chunk
    eye = jnp.eye(reps, dtype=F32)
    mix = jnp.einsum('ab,gst->gasbt', eye, w).reshape(SGU_GROUPS, mix_block, mix_block)
    b = jnp.tile(sgu_b[:, :chunk], (1, reps))
    mixb = jnp.repeat(b.T, SGU_GW, axis=1)
    return mix.astype(BF16), mixb


def _build_cmp_w(cmp_w):
    w = cmp_w.reshape(2, N_KV, 2, CMP_STRIDE, HEAD_DIM, HEAD_DIM)
    out = jnp.zeros((2, CMP_STRIDE, N_KV, HEAD_DIM, 2, N_KV, HEAD_DIM), F32)
    for g in range(N_KV):
        out = out.at[:, :, g, :, :, g, :].set(jnp.transpose(w[:, g], (0, 2, 3, 1, 4)))
    return out.reshape(2, CMP_STRIDE, LANES, 2 * LANES).astype(BF16)


def _compress_kernel(kk_ref, kv_ref, w_ref, b_ref, ck_ref, cvt_ref, *, n_half):
    for c, src in enumerate((kk_ref, kv_ref)):
        acc = jnp.zeros((n_half, 2 * LANES), F32)
        for r in range(CMP_STRIDE):
            xr = src[pl.ds(r, n_half, stride=CMP_STRIDE), :].astype(BF16)
            acc = acc + jnp.dot(xr, w_ref[c, r], preferred_element_type=F32)
        lo = acc[:, :LANES]
        hi_next = pltpu.roll(acc[:, LANES:], n_half - 1, 0)
        comp = lo + hi_next + b_ref[c:c + 1, :]
        if c == 0:
            ck_ref[0] = comp.astype(BF16)
        else:
            cvt_ref[0] = comp.T.astype(BF16)


def _compress_prompt(kck, kcv, w_cmp, b_cmp, bsz, t_len):
    n_half = t_len // CMP_STRIDE
    return pl.pallas_call(
        functools.partial(_compress_kernel, n_half=n_half),
        grid=(bsz,),
        in_specs=[pl.BlockSpec((t_len, LANES), lambda b: (b, 0)), pl.BlockSpec((t_len, LANES), lambda b: (b, 0)),
                  _const_spec(w_cmp.shape), _const_spec(b_cmp.shape)],
        out_specs=[pl.BlockSpec((1, n_half, LANES), lambda b: (b, 0, 0)),
                   pl.BlockSpec((1, LANES, n_half), lambda b: (b, 0, 0))],
        out_shape=[jax.ShapeDtypeStruct((bsz, n_half, LANES), BF16),
                   jax.ShapeDtypeStruct((bsz, LANES, n_half), BF16)],
        compiler_params=_cparams("parallel"),
        name="compress_prompt",
    )(kck, kcv, w_cmp, b_cmp)


def _toeplitz(rel_bias, rows, cols, off, valid_fn):
    span = rows + cols - 1
    d = np.arange(span) + off - (rows - 1)
    vec = jnp.where(valid_fn(d)[None, :], rel_bias[_rel_bucket_np(d)].T, NEG)
    flat = jnp.tile(vec, (1, rows + 1))[:, :rows * (span + 1)]
    skew = flat.reshape(N_HEADS, rows, span + 1)
    return skew[:, ::-1, :cols]


def _head_table(rel_bias, dist, valid, head):
    bucket = _rel_bucket_np(dist)
    need = valid & (bucket != N_BUCKETS - 1)
    out = jnp.broadcast_to(rel_bias[N_BUCKETS - 1][head[:, 0]][:, None], dist.shape)
    cols = np.nonzero(need.any(axis=0))[0]
    if cols.size:
        c0, c1 = int(cols[0]), int(cols[-1]) + 1
        mid = jnp.where(need[:, c0:c1], rel_bias[bucket[:, c0:c1], head], out[:, c0:c1])
        out = jnp.concatenate([out[:, :c0], mid, out[:, c1:]], axis=1)
    return jnp.where(valid, out, NEG)


def _attn_tables(rel_bias, t_len):
    n_half = t_len // CMP_STRIDE
    n_sel = t_len // SEL_BLOCK
    nq = t_len // QT
    c_far = rel_bias[N_BUCKETS - 1]
    to_lanes = lambda a: a.reshape(N_KV, GQA_R, a.shape[1], QT).transpose(0, 2, 1, 3).reshape(
        N_KV, a.shape[1], GQA_R * QT)
    near = jnp.stack([
        to_lanes(_toeplitz(rel_bias, KT, QT, 0, lambda d: d >= 0)),
        to_lanes(_toeplitz(rel_bias, KT, QT, KT, lambda d: d >= 0)),
        to_lanes(_toeplitz(rel_bias, KT, QT, 2 * KT, lambda d: d < WINDOW))])
    far = jnp.repeat(c_far.reshape(N_KV, 1, GQA_R), QT, axis=2).reshape(N_KV, 1, GQA_R * QT)
    per_q = QT // CMP_STRIDE
    band_rows = 2 * per_q
    lead = per_q // 2
    off = lead * CMP_STRIDE - (CMP_LEN - 1)
    band = _toeplitz(rel_bias, band_rows * CMP_STRIDE, QT, off, lambda d: d >= 0)[:, ::CMP_STRIDE]
    assert _rel_bucket_np(np.array([off + CMP_STRIDE])).item() == N_BUCKETS - 1
    tiles = []
    for i in range(nq):
        lo = i * per_q - lead
        parts = []
        if lo > 0:
            parts.append(jnp.broadcast_to(c_far[:, None, None], (N_HEADS, lo, QT)))
        b0, b1 = max(0, -lo), min(band_rows, n_half - lo)
        parts.append(band[:, b0:b1])
        rest = n_half - (lo + b1)
        if rest > 0:
            parts.append(jnp.full((N_HEADS, rest, QT), NEG, F32))
        tiles.append(to_lanes(jnp.concatenate(parts, axis=1)))
    biasc = jnp.stack(tiles)
    cs = np.arange(n_half - 1)[None, :] * CMP_STRIDE
    ss = np.arange(n_sel)[:, None] * SEL_BLOCK
    ov = np.zeros((n_sel, n_half), np.float32)
    ov[:, :n_half - 1] = (cs < ss + SEL_BLOCK) & (cs + CMP_LEN > ss)
    ovt4 = jnp.asarray(np.tile(ov, (1, GQA_R)), BF16)
    return near, far, biasc, ovt4


def _prompt_attn_kernel(q_ref, gt_ref, ck_ref, cvt_ref, ksk_ref, ksvt_ref, kwk_ref, kwvt_ref,
                        biasc_ref, near_ref, far_ref, ovt_ref, o_ref,
                        imp_ref, selneg_ref, m_ref, l_ref, acc_ref, oc_ref, os_ref, outt_ref, *, n_sel):
    i = pl.program_id(1)
    nt = (((1,), (1,)), ((), ()))
    groups = range(N_KV)
    own = lambda g: slice(g * HEAD_DIM, (g + 1) * HEAD_DIM)
    jj = lax.broadcasted_iota(jnp.int32, (n_sel, QT), 0)
    tt = i * QT + lax.broadcasted_iota(jnp.int32, (n_sel, QT), 1)
    cur = lax.shift_right_logical(tt, 6)
    forced = (jj == 0) | (jj == cur) | (jj == cur - 1)
    future = jj > cur
    qs = [jnp.concatenate([q_ref[0, :, (g * GQA_R + r) * LANES:(g * GQA_R + r + 1) * LANES]
                           for r in range(GQA_R)], axis=0) for g in groups]

    near = lambda idx: (lambda g, cols: near_ref[idx, g, :, cols])
    far = lambda g, cols: far_ref[g, :, cols]

    def online_step(k_ref, vt_ref, kt, bias_of, mask_of, shift_of, first):
        kk = k_ref[pl.ds(pl.multiple_of(kt * KT, KT), KT), :]
        vt = vt_ref[kt]
        for g in groups:
            mask = None if mask_of is None else jnp.concatenate([mask_of(g)] * CHAIN_HEADS, axis=1)
            for c0 in range(0, GQA_R, CHAIN_HEADS):
                cols = slice(c0 * QT, (c0 + CHAIN_HEADS) * QT)
                s = lax.dot_general(kk, qs[g][cols], nt, preferred_element_type=F32)
                if bias_of is not None:
                    s = s + bias_of(g, cols)
                if mask is not None:
                    s = s + mask
                mt = jnp.max(s, axis=0, keepdims=True)
                shift = None if shift_of is None else shift_of(g, cols)
                if shift is not None:
                    mt = mt + shift
                if first:
                    m_new = mt
                else:
                    m_old = m_ref[g, :, cols]
                    m_new = jnp.maximum(m_old, mt)
                    alpha = jnp.exp(m_old - m_new)
                p = jnp.exp(s - (m_new if shift is None else m_new - shift))
                pv = jnp.dot(vt, p.astype(BF16), preferred_element_type=F32)
                ps = jnp.sum(p, axis=0, keepdims=True)
                if first:
                    l_ref[g, :, cols] = ps
                    acc_ref[g, :, cols] = pv
                else:
                    l_ref[g, :, cols] = alpha * l_ref[g, :, cols] + ps
                    acc_ref[g, :, cols] = alpha * acc_ref[g, :, cols] + pv
                m_ref[g, :, cols] = m_new

    def sel_mask(g, kt):
        rows = [jnp.broadcast_to(selneg_ref[g, pl.ds(kt * (KT // SEL_BLOCK) + b, 1), :], (SEL_BLOCK, QT))
                for b in range(KT // SEL_BLOCK)]
        return jnp.concatenate(rows, axis=0)

    for g in groups:
        bc = biasc_ref[0, g]
        sc = lax.dot_general(ck_ref[0], qs[g], nt, preferred_element_type=F32) + bc
        e = jnp.exp(sc - jnp.max(sc, axis=0, keepdims=True))
        e = jnp.where(bc > 0.5 * NEG, e, 0.0)
        lsum = jnp.sum(e, axis=0, keepdims=True)
        pb = (e / jnp.where(lsum > 0.0, lsum, 1.0)).astype(BF16)
        oc_ref[g] = jnp.dot(cvt_ref[0], pb, preferred_element_type=F32)[own(g)]
        p4 = jnp.concatenate([pb[:, r * QT:(r + 1) * QT] for r in range(GQA_R)], axis=0)
        imp = jnp.dot(ovt_ref[...], p4, preferred_element_type=F32)
        imp = jnp.where(forced, FORCE, imp)
        imp = jnp.where(future, NEG, imp)
        imp_ref[...] = imp

        def rank_body(ii, cnt, imp=imp):
            row = imp_ref[pl.ds(ii, 1), :]
            beats = (row > imp) | ((row == imp) & (jj > ii))
            return cnt + jnp.where(beats, 1.0, 0.0)

        cnt = lax.fori_loop(0, n_sel, rank_body, jnp.zeros((n_sel, QT), F32), unroll=8)
        selneg_ref[g] = jnp.where((cnt < N_SELECT) & jnp.logical_not(future), 0.0, NEG)

    online_step(ksk_ref, ksvt_ref, i, near(0), lambda g: sel_mask(g, i), None, True)

    @pl.when(i >= 1)
    def _():
        online_step(ksk_ref, ksvt_ref, i - 1, near(1), lambda g: sel_mask(g, i - 1), None, False)

    def far_body(kt, carry):
        online_step(ksk_ref, ksvt_ref, kt, None, lambda g: sel_mask(g, kt), far, False)
        return carry

    lax.fori_loop(0, jnp.maximum(i - 1, 0), far_body, 0)
    for g in groups:
        os_ref[g] = acc_ref[g, own(g), :] / l_ref[g]
    online_step(kwk_ref, kwvt_ref, i, near(0), None, None, True)

    @pl.when(i >= 1)
    def _():
        online_step(kwk_ref, kwvt_ref, i - 1, near(1), None, None, False)

    @pl.when(i >= 2)
    def _():
        online_step(kwk_ref, kwvt_ref, i - 2, near(2), None, None, False)

    gsig = jax.nn.sigmoid(gt_ref[0]).T
    for g in groups:
        o_c, o_s, o_w = oc_ref[g], os_ref[g], acc_ref[g, own(g), :] / l_ref[g]
        for r in range(GQA_R):
            h = g * GQA_R + r
            cols = slice(r * QT, (r + 1) * QT)
            outt_ref[h * HEAD_DIM:(h + 1) * HEAD_DIM, :] = (
                gsig[h:h + 1, :] * o_c[:, cols]
                + gsig[N_HEADS + h:N_HEADS + h + 1, :] * o_s[:, cols]
                + gsig[2 * N_HEADS + h:2 * N_HEADS + h + 1, :] * o_w[:, cols])
    o_ref[0] = outt_ref[...].T.astype(BF16)


def _prompt_attn(q, gt, ck, cvt, ksk, ksvt, kwk, kwvt, tables, bsz, t_len):
    near, far, biasc, ovt4 = tables
    n_half = t_len // CMP_STRIDE
    n_sel = t_len // SEL_BLOCK
    nkt = t_len // KT
    per_b2 = lambda c: pl.BlockSpec((t_len, c), lambda b, i: (b, 0))
    vt_spec = pl.BlockSpec((nkt, LANES, KT), lambda b, i: (b, 0, 0))
    return pl.pallas_call(
        functools.partial(_prompt_attn_kernel, n_sel=n_sel),
        grid=(bsz, t_len // QT),
        in_specs=[pl.BlockSpec((1, QT, N_HEADS * LANES), lambda b, i: (b, i, 0)),
                  pl.BlockSpec((1, QT, LANES), lambda b, i: (b, i, 0)),
                  pl.BlockSpec((1, n_half, LANES), lambda b, i: (b, 0, 0)),
                  pl.BlockSpec((1, LANES, n_half), lambda b, i: (b, 0, 0)),
                  per_b2(LANES), vt_spec, per_b2(LANES), vt_spec,
                  pl.BlockSpec((1, N_KV, n_half, GQA_R * QT), lambda b, i: (i, 0, 0, 0)),
                  _const_spec(near.shape), _const_spec(far.shape), _const_spec(ovt4.shape)],
        out_specs=pl.BlockSpec((1, QT, NSA_W), lambda b, i: (b, i, 0)),
        out_shape=jax.ShapeDtypeStruct((bsz, t_len, NSA_W), BF16),
        scratch_shapes=[pltpu.VMEM((n_sel, QT), F32), pltpu.VMEM((N_KV, n_sel, QT), F32),
                        pltpu.VMEM((N_KV, 1, GQA_R * QT), F32), pltpu.VMEM((N_KV, 1, GQA_R * QT), F32),
                        pltpu.VMEM((N_KV, LANES, GQA_R * QT), F32),
                        pltpu.VMEM((N_KV, HEAD_DIM, GQA_R * QT), F32), pltpu.VMEM((N_KV, HEAD_DIM, GQA_R * QT), F32),
                        pltpu.VMEM((NSA_W, QT), F32)],
        compiler_params=_cparams("parallel", "arbitrary"),
        name="prompt_attn",
    )(q, gt, ck, cvt, ksk, ksvt, kwk, kwvt, biasc, near, far, ovt4)


def _oproj_kernel(x_ref, a_ref, b_ref, woa_ref, wob_ref, g_ref, wq_ref, x1_ref, q_ref):
    x1 = (x_ref[...] + jnp.dot(a_ref[...], woa_ref[...], preferred_element_type=F32)
          + jnp.dot(b_ref[...], wob_ref[...], preferred_element_type=F32))
    x1_ref[...] = x1
    h = _rms(x1, g_ref[...]).astype(BF16)
    q_ref[...] = jnp.dot(h, wq_ref[...], preferred_element_type=F32).astype(BF16)


def _oproj(x2d, a, b, w_oa, w_ob, g, w_xq):
    n = x2d.shape[0]
    row = lambda c: pl.BlockSpec((ROW_TILE, c), lambda i: (i, 0))
    hq = MEM_HEADS * MEM_HD
    return pl.pallas_call(
        _oproj_kernel,
        grid=(n // ROW_TILE,),
        in_specs=[row(D_MODEL), row(SGU_W), row(NSA_W), _const_spec(w_oa.shape), _const_spec(w_ob.shape),
                  _const_spec((1, D_MODEL)), _const_spec(w_xq.shape)],
        out_specs=[row(D_MODEL), row(hq)],
        out_shape=[jax.ShapeDtypeStruct((n, D_MODEL), F32), jax.ShapeDtypeStruct((n, hq), BF16)],
        compiler_params=_cparams("parallel"),
        name="oproj_xq",
    )(x2d, a, b, w_oa, w_ob, g, w_xq)


def _memkv_kernel(x_ref, g_ref, w_ref, o_ref, ob_ref):
    h = _rms(x_ref[...], g_ref[...]).astype(BF16)
    o = jnp.dot(h, w_ref[...], preferred_element_type=F32)
    o_ref[...] = o
    ob_ref[...] = o.astype(BF16)


def _memkv(mem2d, g, w):
    n = mem2d.shape[0]
    c = w.shape[1]
    row = lambda cc: pl.BlockSpec((ROW_TILE, cc), lambda i: (i, 0))
    return pl.pallas_call(
        _memkv_kernel,
        grid=(n // ROW_TILE,),
        in_specs=[row(D_MODEL), _const_spec((1, D_MODEL)), _const_spec(w.shape)],
        out_specs=[row(c), row(c)],
        out_shape=[jax.ShapeDtypeStruct((n, c), F32), jax.ShapeDtypeStruct((n, c), BF16)],
        compiler_params=_cparams("parallel"),
        name="memkv_proj",
    )(mem2d, g, w)


def _softmax_rows(s):
    e = jnp.exp(s - jnp.max(s, axis=-1, keepdims=True))
    return e / jnp.sum(e, axis=-1, keepdims=True)


def _xattn_tail(x1, o, wxo_ref, g_ref, wr_ref, br_ref, x2_ref, h_ref, te_ref, tg_ref):
    x2 = x1 + jnp.dot(o.astype(BF16), wxo_ref[...], preferred_element_type=F32)
    x2_ref[...] = x2
    hb = _rms(x2, g_ref[...]).astype(BF16)
    h_ref[...] = hb
    lt = lax.dot_general(wr_ref[...], hb, (((1,), (1,)), ((), ())), preferred_element_type=F32) + br_ref[...]
    eidx = lax.broadcasted_iota(jnp.int32, lt.shape, 0)
    tops, idxs = [], []
    for _ in range(TOP_K):
        m = jnp.max(lt, axis=0, keepdims=True)
        idx = jnp.min(jnp.where(lt == m, eidx, N_EXPERTS), axis=0, keepdims=True)
        tops.append(m)
        idxs.append(idx)
        lt = jnp.where(eidx == idx, -jnp.inf, lt)
    es = [jnp.exp(t - tops[0]) for t in tops]
    den = es[0] + es[1] + es[2] + es[3]
    te_ref[...] = jnp.concatenate(idxs, axis=0)
    tg_ref[...] = jnp.concatenate([e / den for e in es], axis=0)


def _xattn_prompt_kernel(x1_ref, q_ref, kv_ref, wxo_ref, g_ref, wr_ref, br_ref, x2_ref, h_ref, te_ref, tg_ref):
    outs = []
    for hh in range(MEM_HEADS):
        qh = q_ref[:, hh * MEM_HD:(hh + 1) * MEM_HD]
        kh = kv_ref[:, hh * MEM_HD:(hh + 1) * MEM_HD]
        vh = kv_ref[:, (MEM_HEADS + hh) * MEM_HD:(MEM_HEADS + hh + 1) * MEM_HD]
        s = lax.dot_general(qh, kh, (((1,), (1,)), ((), ())), preferred_element_type=F32) * (MEM_HD ** -0.5)
        outs.append(jnp.dot(_softmax_rows(s).astype(BF16), vh, preferred_element_type=F32))
    o = jnp.concatenate(outs, axis=1)
    _xattn_tail(x1_ref[...], o, wxo_ref, g_ref, wr_ref, br_ref, x2_ref, h_ref, te_ref, tg_ref)


def _xattn_sample_kernel(x1_ref, q_ref, kv_ref, wxo_ref, g_ref, wr_ref, br_ref, x2_ref, h_ref, te_ref, tg_ref,
                         o_scr, *, t_len):
    nb = q_ref.shape[0] // t_len
    qf = q_ref[...].astype(F32)
    for bb in range(nb):
        for hh in range(MEM_HEADS):
            qh = qf[bb * t_len:(bb + 1) * t_len, hh * MEM_HD:(hh + 1) * MEM_HD].astype(BF16)
            kh = kv_ref[bb, pl.ds(hh, MEM_LEN, stride=2 * MEM_HEADS), :].astype(BF16)
            vh = kv_ref[bb, pl.ds(MEM_HEADS + hh, MEM_LEN, stride=2 * MEM_HEADS), :].astype(BF16)
            s = lax.dot_general(qh, kh, (((1,), (1,)), ((), ())), preferred_element_type=F32) * (MEM_HD ** -0.5)
            o_scr[bb * t_len:(bb + 1) * t_len, hh * MEM_HD:(hh + 1) * MEM_HD] = jnp.dot(
                _softmax_rows(s).astype(BF16), vh, preferred_element_type=F32)
    _xattn_tail(x1_ref[...], o_scr[...], wxo_ref, g_ref, wr_ref, br_ref, x2_ref, h_ref, te_ref, tg_ref)


XS_BATCH = 16


def _xattn(x1, q, kv, w_xo, g, w_rt, b_r, t_len, prompt):
    n = x1.shape[0]
    hq = MEM_HEADS * MEM_HD
    if prompt:
        rows = ROW_TILE
        per_b = t_len // rows
        kv_spec = pl.BlockSpec((MEM_LEN, 2 * hq), lambda i: (i // per_b, 0))
        kern = _xattn_prompt_kernel
        scratch = []
    else:
        rows = XS_BATCH * t_len
        kv_spec = pl.BlockSpec((XS_BATCH, MEM_LEN * 2 * MEM_HEADS, MEM_HD), lambda i: (i, 0, 0))
        kern = functools.partial(_xattn_sample_kernel, t_len=t_len)
        scratch = [pltpu.VMEM((rows, hq), F32)]
    row = lambda c: pl.BlockSpec((rows, c), lambda i: (i, 0))
    col = pl.BlockSpec((TOP_K, rows), lambda i: (0, i))
    return pl.pallas_call(
        kern,
        grid=(n // rows,),
        in_specs=[row(D_MODEL), row(hq), kv_spec, _const_spec(w_xo.shape), _const_spec((1, D_MODEL)),
                  _const_spec(w_rt.shape), _const_spec(b_r.shape)],
        out_specs=[row(D_MODEL), row(D_MODEL), col, col],
        out_shape=[jax.ShapeDtypeStruct((n, D_MODEL), F32), jax.ShapeDtypeStruct((n, D_MODEL), BF16),
                   jax.ShapeDtypeStruct((TOP_K, n), jnp.int32), jax.ShapeDtypeStruct((TOP_K, n), F32)],
        scratch_shapes=scratch,
        compiler_params=_cparams("parallel"),
        name="xattn_router_prompt" if prompt else "xattn_router_sample",
    )(x1, q, kv, w_xo, g, w_rt, b_r)


MOE_BLOCK = 256


def _expert_kernel(blk_ref, exp_ref, lo_ref, hi_ref, x_ref, wgu_ref, bgu_ref, wd_ref, bd_ref, y_ref,
                   wgu_b, wd_b):
    w = pl.program_id(0)
    lo, hi, row0 = lo_ref[w], hi_ref[w], blk_ref[w] * MOE_BLOCK

    @pl.when((w == 0) | (exp_ref[w] != exp_ref[jnp.maximum(w - 1, 0)]))
    def _():
        wgu_b[...] = wgu_ref[0].astype(BF16)
        wd_b[...] = wd_ref[0].astype(BF16)

    @pl.when(hi > lo)
    def _():
        gu = jnp.dot(x_ref[...], wgu_b[...], preferred_element_type=F32) + bgu_ref[0]
        glu = jnp.minimum(gu[:, :D_FF], SWIGLU_LIMIT)
        lin = jnp.clip(gu[:, D_FF:], -SWIGLU_LIMIT, SWIGLU_LIMIT)
        hdn = glu * jax.nn.sigmoid(SWIGLU_ALPHA * glu) * (lin + 1.0)
        y = jnp.dot(hdn.astype(BF16), wd_b[...], preferred_element_type=F32) + bd_ref[0]
        row = row0 + lax.broadcasted_iota(jnp.int32, (MOE_BLOCK, 1), 0)
        mine = (row >= lo) & (row < hi)

        @pl.when(lo == row0)
        def _():
            y_ref[...] = jnp.where(mine, y, 0.0)

        @pl.when(lo != row0)
        def _():
            y_ref[...] = jnp.where(mine, y, y_ref[...])


def _experts(items, xs, w_gu, b_gu, w_down, b_down):
    blk, exp, lo, hi = items
    n_rows = xs.shape[0]
    im = lambda f: (lambda w, blk, exp, lo, hi: f(blk[w], exp[w]))
    grid_spec = pltpu.PrefetchScalarGridSpec(
        num_scalar_prefetch=4,
        grid=(blk.shape[0],),
        in_specs=[pl.BlockSpec((MOE_BLOCK, D_MODEL), im(lambda b, e: (b, 0))),
                  pl.BlockSpec((1, D_MODEL, 2 * D_FF), im(lambda b, e: (e, 0, 0))),
                  pl.BlockSpec((1, 1, 2 * D_FF), im(lambda b, e: (e, 0, 0))),
                  pl.BlockSpec((1, D_FF, D_MODEL), im(lambda b, e: (e, 0, 0))),
                  pl.BlockSpec((1, 1, D_MODEL), im(lambda b, e: (e, 0, 0)))],
        out_specs=pl.BlockSpec((MOE_BLOCK, D_MODEL), im(lambda b, e: (b, 0))),
        scratch_shapes=[pltpu.VMEM((D_MODEL, 2 * D_FF), BF16), pltpu.VMEM((D_FF, D_MODEL), BF16)],
    )
    return pl.pallas_call(
        _expert_kernel,
        grid_spec=grid_spec,
        out_shape=jax.ShapeDtypeStruct((n_rows, D_MODEL), F32),
        compiler_params=_cparams("arbitrary"),
        name="moe_experts",
    )(blk, exp, lo, hi, xs, w_gu, b_gu, w_down, b_down)


def _moe_dispatch(te):
    n = te.shape[1]
    nk = TOP_K * n
    assert nk % MOE_BLOCK == 0
    n_blk = nk // MOE_BLOCK
    iota = jnp.arange(nk, dtype=jnp.int32)
    flat_e = te.reshape(nk)
    _, order = lax.sort_key_val(flat_e, iota)
    pos = lax.sort_key_val(order, iota)[1].reshape(TOP_K, n)
    counts = jnp.sum(flat_e[:, None] == jnp.arange(N_EXPERTS)[None, :], axis=0, dtype=jnp.int32)
    start = jnp.cumsum(counts) - counts
    lo = jnp.sort(jnp.concatenate([jnp.arange(n_blk, dtype=jnp.int32) * MOE_BLOCK, start[1:]]))
    hi = jnp.concatenate([lo[1:], jnp.full((1,), nk, jnp.int32)])
    blk = jnp.minimum(lo // MOE_BLOCK, n_blk - 1)
    exp = jnp.sum(start[None, 1:] <= lo[:, None], axis=1, dtype=jnp.int32)
    return order % n, pos, (blk, exp, lo, hi)


def _final_kernel(x_ref, y_ref, g_ref, fn_ref, o_ref):
    x = x_ref[...]
    for k in range(TOP_K):
        x = x + y_ref[k] * g_ref[:, k:k + 1]
    o_ref[...] = _rms(x, fn_ref[...])


def _final(x2, yk, gates, fnorm):
    n = x2.shape[0]
    rows = 256
    return pl.pallas_call(
        _final_kernel,
        grid=(n // rows,),
        in_specs=[pl.BlockSpec((rows, D_MODEL), lambda i: (i, 0)),
                  pl.BlockSpec((TOP_K, rows, D_MODEL), lambda i: (0, i, 0)),
                  pl.BlockSpec((rows, TOP_K), lambda i: (i, 0)), _const_spec((1, D_MODEL))],
        out_specs=pl.BlockSpec((rows, D_MODEL), lambda i: (i, 0)),
        out_shape=jax.ShapeDtypeStruct((n, D_MODEL), F32),
        compiler_params=_cparams("parallel"),
        name="moe_combine_final_norm",
    )(x2, yk, gates, fnorm)


def _page_copy(pool_ref, pt_ref, buf_ref, sem_ref, b, p, slot, n_pages, rows):
    return pltpu.make_async_copy(pool_ref.at[pt_ref[b * n_pages + p]],
                                 buf_ref.at[slot, pl.ds(p * rows, rows)], sem_ref.at[slot])


def _fetch_pages(pool_ref, pt_ref, buf_ref, sem_ref, b, slot, n_pages, rows):
    def body(p, c):
        _page_copy(pool_ref, pt_ref, buf_ref, sem_ref, b, p, slot, n_pages, rows).start()
        return c
    lax.fori_loop(0, n_pages, body, 0)


def _wait_pages(pool_ref, pt_ref, buf_ref, sem_ref, b, slot, n_pages, rows):
    def body(p, c):
        _page_copy(pool_ref, pt_ref, buf_ref, sem_ref, b, p, slot, n_pages, rows).wait()
        return c
    lax.fori_loop(0, n_pages, body, 0)


def _compress_sample_kernel(pt_ref, pool_ref, hi_ref, w_ref, b_ref, perm_ref, ck_ref, cv_ref, buf_ref, sem_ref, tok_ref,
                            *, n_pages, page_rows):
    b = pl.program_id(0)
    nb = pl.num_programs(0)
    slot = lax.rem(b, 2)
    rows = 2 * LANES
    n_half = n_pages * page_rows // CMP_STRIDE

    @pl.when(b == 0)
    def _():
        _fetch_pages(pool_ref, pt_ref, buf_ref, sem_ref, b, slot, n_pages, rows)

    @pl.when(b + 1 < nb)
    def _():
        _fetch_pages(pool_ref, pt_ref, buf_ref, sem_ref, b + 1, 1 - slot, n_pages, rows)

    _wait_pages(pool_ref, pt_ref, buf_ref, sem_ref, b, slot, n_pages, rows)

    per_page = page_rows // CMP_STRIDE

    def regroup(p, carry):
        page = buf_ref[slot, pl.ds(pl.multiple_of(p * rows, rows), rows), :].astype(BF16)
        t = lax.dot_general(perm_ref[...], page, (((1,), (1,)), ((), ())), preferred_element_type=F32)
        blocks = pl.ds(pl.multiple_of(p * per_page, per_page), per_page)
        for r in range(CMP_STRIDE):
            for c in range(2):
                tok_ref[c, r // 2, blocks, (r % 2) * LANES:(r % 2 + 1) * LANES] = (
                    t[r * per_page:(r + 1) * per_page, c * LANES:(c + 1) * LANES])
        return carry

    lax.fori_loop(0, n_pages, regroup, 0, unroll=8)
    last = lax.broadcasted_iota(jnp.int32, (n_half, LANES), 0) == n_half - 1
    for c, out in enumerate((ck_ref, cv_ref)):
        acc = jnp.zeros((n_half, 2 * LANES), F32)
        for q in range(CMP_STRIDE // 2):
            acc = acc + jnp.dot(tok_ref[c, q].astype(BF16), w_ref[c, q], preferred_element_type=F32)
        hi_next = pltpu.roll(acc[:, LANES:], n_half - 1, 0)
        hi_next = jnp.where(last, hi_ref[0, c:c + 1, :], hi_next)
        out[0] = (acc[:, :LANES] + hi_next + b_ref[c:c + 1, :]).astype(BF16)


def _hi_new_kernel(ak_ref, av_ref, w_ref, o_ref):
    for c, a_ref in enumerate((ak_ref, av_ref)):
        o_ref[:, c, :] = jnp.dot(a_ref[...].astype(BF16), w_ref[c], preferred_element_type=F32)


def _hi_new(kck_new, kcv_new, w_cmp, n_batch, t_len):
    w_hi = w_cmp[:, :t_len, :, LANES:].reshape(2, t_len * LANES, LANES)
    ak = kck_new.reshape(n_batch, t_len * LANES)
    av = kcv_new.reshape(n_batch, t_len * LANES)
    return pl.pallas_call(
        _hi_new_kernel,
        out_shape=jax.ShapeDtypeStruct((n_batch, 2, LANES), F32),
        name="compress_new_tokens",
    )(ak, av, w_hi)


def _compress_sample(page_table_flat, pool, hi_new, w_cmp, b_cmp, n_batch, n_pages, page_rows):
    assert page_rows == LANES
    tok = np.arange(page_rows)
    slot_of = (tok % CMP_STRIDE) * (page_rows // CMP_STRIDE) + tok // CMP_STRIDE
    perm = jnp.asarray(np.arange(page_rows)[:, None] == slot_of[None, :], BF16)
    w_cmp = w_cmp.reshape(2, CMP_STRIDE // 2, 2 * LANES, 2 * LANES)
    n_half = n_pages * page_rows // CMP_STRIDE
    grid_spec = pltpu.PrefetchScalarGridSpec(
        num_scalar_prefetch=1,
        grid=(n_batch,),
        in_specs=[pl.BlockSpec(memory_space=pl.ANY),
                  pl.BlockSpec((1, 2, LANES), lambda b, pt: (b, 0, 0)),
                  pl.BlockSpec(w_cmp.shape, lambda b, pt: (0, 0, 0, 0)),
                  pl.BlockSpec(b_cmp.shape, lambda b, pt: (0, 0)),
                  pl.BlockSpec(perm.shape, lambda b, pt: (0, 0))],
        out_specs=[pl.BlockSpec((1, n_half, LANES), lambda b, pt: (b, 0, 0)),
                   pl.BlockSpec((1, n_half, LANES), lambda b, pt: (b, 0, 0))],
        scratch_shapes=[pltpu.VMEM((2, n_pages * 2 * LANES, page_rows), F32), pltpu.SemaphoreType.DMA((2,)),
                        pltpu.VMEM((2, CMP_STRIDE // 2, n_half, 2 * LANES), F32)],
    )
    return pl.pallas_call(
        functools.partial(_compress_sample_kernel, n_pages=n_pages, page_rows=page_rows),
        grid_spec=grid_spec,
        out_shape=[jax.ShapeDtypeStruct((n_batch, n_half, LANES), BF16)] * 2,
        compiler_params=_cparams("arbitrary"),
        name="compress_sample",
    )(page_table_flat, pool, hi_new, w_cmp, b_cmp, perm)


N_SEL_PAD = 256


def _sample_tables(rel_bias, past, t_len):
    n_cmp = past // CMP_STRIDE
    n_sel = past // SEL_BLOCK + 1
    y = np.tile(np.arange(t_len), N_HEADS)[:, None]
    hh = np.repeat(np.arange(N_HEADS), t_len)[:, None]

    def tab(dist, valid):
        return _head_table(rel_bias, dist, valid, hh)

    dc = past + y - (np.arange(n_cmp)[None, :] * CMP_STRIDE + CMP_LEN - 1)
    biasc = tab(dc, dc >= 0)
    dp = past + y - np.arange(past)[None, :]
    biasp = tab(dp, dp >= 0)
    dn = y - np.arange(t_len)[None, :]
    biasn = tab(dn, dn >= 0)
    dw = WINDOW + y - np.arange(WINDOW)[None, :]
    biasw = tab(dw, dw < WINDOW)
    cs = np.arange(n_cmp)[:, None] * CMP_STRIDE
    ss = np.arange(N_SEL_PAD)[None, :] * SEL_BLOCK
    ov = ((cs < ss + SEL_BLOCK) & (cs + CMP_LEN > ss) & (np.arange(N_SEL_PAD)[None, :] < n_sel))
    ov4 = jnp.asarray(np.tile(ov.astype(np.float32), (GQA_R, 1)), BF16)
    expand = (np.arange(past)[None, :] // SEL_BLOCK == np.arange(past // SEL_BLOCK)[:, None])
    return biasc, biasp, biasn, biasw, ov4, jnp.asarray(expand.astype(np.float32), BF16)


def _sample_attn_kernel(pt_ref, q_ref, gt_ref, ck_ref, cv_ref, pool_ref, win_ref, ksk_ref, ksv_ref, kwk_ref,
                        kwv_ref, biasc_ref, biasp_ref, biasn_ref, biasw_ref, ov_ref, exp_ref, o_ref,
                        buf_ref, sem_ref, s_ref, *, n_pages, page_tokens, t_len, cur_block):
    b = pl.program_id(0)
    nb = pl.num_programs(0)
    slot = lax.rem(b, 2)
    rows = 2 * LANES
    nt = (((1,), (1,)), ((), ()))
    n_rows = N_HEADS * t_len

    @pl.when(b == 0)
    def _():
        _fetch_pages(pool_ref, pt_ref, buf_ref, sem_ref, b, slot, n_pages, rows)

    @pl.when(b + 1 < nb)
    def _():
        _fetch_pages(pool_ref, pt_ref, buf_ref, sem_ref, b + 1, 1 - slot, n_pages, rows)

    qb = q_ref[0]
    bc = biasc_ref[...]
    sc = lax.dot_general(qb, ck_ref[0], nt, preferred_element_type=F32) + bc
    e = jnp.exp(sc - jnp.max(sc, axis=1, keepdims=True))
    e = jnp.where(bc > 0.5 * NEG, e, 0.0)
    lsum = jnp.sum(e, axis=1, keepdims=True)
    pb = (e / jnp.where(lsum > 0.0, lsum, 1.0)).astype(BF16)
    o_c = jnp.dot(pb, cv_ref[0], preferred_element_type=F32)
    p4 = jnp.concatenate(
        [jnp.concatenate([pb[(g * GQA_R + r) * t_len:(g * GQA_R + r + 1) * t_len, :] for r in range(GQA_R)], axis=1)
         for g in range(N_KV)], axis=0)
    imp = jnp.dot(p4, ov_ref[...], preferred_element_type=F32)
    lane = lax.broadcasted_iota(jnp.int32, imp.shape, 1)
    imp = jnp.where((lane == 0) | (lane == cur_block) | (lane == cur_block - 1), FORCE, imp)
    imp = jnp.where(lane > cur_block, NEG, imp)

    def rank_body(k, cnt):
        other = pltpu.roll(imp, k, 1)
        beats = (other > imp) | ((other == imp) & (lane >= k))
        return cnt + jnp.where(beats, 1.0, 0.0)

    cnt = lax.fori_loop(1, N_SEL_PAD, rank_body, jnp.zeros(imp.shape, F32), unroll=15)
    sel = jnp.where((cnt < N_SELECT) & (lane <= cur_block), 1.0, 0.0)
    sel_rows = jnp.concatenate([sel[g * t_len:(g + 1) * t_len, :] for g in range(N_KV) for _ in range(GQA_R)], axis=0)
    n_past_blocks = n_pages * page_tokens // SEL_BLOCK
    keep = jnp.dot(sel_rows[:, :n_past_blocks].astype(BF16), exp_ref[...], preferred_element_type=F32)

    def softmax_pair(s_old, s_new):
        m = jnp.maximum(jnp.max(s_old, axis=1, keepdims=True), jnp.max(s_new, axis=1, keepdims=True))
        p_old = jnp.exp(s_old - m)
        p_new = jnp.exp(s_new - m)
        inv = 1.0 / (jnp.sum(p_old, axis=1, keepdims=True) + jnp.sum(p_new, axis=1, keepdims=True))
        return p_old, p_new, inv

    _wait_pages(pool_ref, pt_ref, buf_ref, sem_ref, b, slot, n_pages, rows)
    for p in range(n_pages):
        kt_page = buf_ref[slot, p * rows:p * rows + LANES, :].astype(BF16)
        s_ref[:, p * page_tokens:(p + 1) * page_tokens] = jnp.dot(qb, kt_page, preferred_element_type=F32)
    s_past = jnp.where(keep > 0.5, s_ref[...] + biasp_ref[...], NEG)
    bn = biasn_ref[...]
    s_new = lax.dot_general(qb, ksk_ref[...].astype(BF16), nt, preferred_element_type=F32) + bn
    p_past, p_new, inv = softmax_pair(s_past, s_new)
    p_past = p_past.astype(BF16)
    o_s = jnp.dot(p_new.astype(BF16), ksv_ref[...].astype(BF16), preferred_element_type=F32)
    for p in range(n_pages):
        vt_page = buf_ref[slot, p * rows + LANES:(p + 1) * rows, :].astype(BF16)
        o_s = o_s + lax.dot_general(p_past[:, p * page_tokens:(p + 1) * page_tokens], vt_page, nt,
                                    preferred_element_type=F32)
    o_s = o_s * inv
    s_win = jnp.dot(qb, win_ref[0, :LANES, :].astype(BF16), preferred_element_type=F32) + biasw_ref[...]
    s_wnew = lax.dot_general(qb, kwk_ref[...].astype(BF16), nt, preferred_element_type=F32) + bn
    p_win, p_wnew, inv_w = softmax_pair(s_win, s_wnew)
    o_w = (lax.dot_general(p_win.astype(BF16), win_ref[0, LANES:, :].astype(BF16), nt, preferred_element_type=F32)
           + jnp.dot(p_wnew.astype(BF16), kwv_ref[...].astype(BF16), preferred_element_type=F32)) * inv_w
    gs = jax.nn.sigmoid(gt_ref[0])
    o = gs[:, 0:1] * o_c + gs[:, 1:2] * o_s + gs[:, 2:3] * o_w
    row = lax.broadcasted_iota(jnp.int32, (n_rows, HEAD_DIM), 0)
    o_ref[0] = jnp.where(row < GQA_R * t_len, o[:, :HEAD_DIM], o[:, HEAD_DIM:])


def _sample_attn(page_table_flat, q, gt, ck, cv, pool, win, ksk, ksv, kwk, kwv, tables, n_batch, n_pages, page_tokens,
                 t_len):
    biasc, biasp, biasn, biasw, ov4, expand = tables
    past = n_pages * page_tokens
    n_rows = N_HEADS * t_len
    cur_block = past // SEL_BLOCK
    assert past % SEL_BLOCK == 0 and t_len <= SEL_BLOCK and cur_block < N_SEL_PAD
    new_spec = pl.BlockSpec((t_len, LANES), lambda b, pt: (b, 0))
    cst = lambda a: pl.BlockSpec(a.shape, lambda b, pt: (0,) * a.ndim)
    grid_spec = pltpu.PrefetchScalarGridSpec(
        num_scalar_prefetch=1,
        grid=(n_batch,),
        in_specs=[pl.BlockSpec((1, n_rows, LANES), lambda b, pt: (b, 0, 0)),
                  pl.BlockSpec((1, n_rows, 3), lambda b, pt: (b, 0, 0)),
                  pl.BlockSpec((1, past // CMP_STRIDE, LANES), lambda b, pt: (b, 0, 0)),
                  pl.BlockSpec((1, past // CMP_STRIDE, LANES), lambda b, pt: (b, 0, 0)),
                  pl.BlockSpec(memory_space=pl.ANY),
                  pl.BlockSpec((1, 2 * LANES, WINDOW), lambda b, pt: (b, 0, 0)),
                  new_spec, new_spec, new_spec, new_spec,
                  cst(biasc), cst(biasp), cst(biasn), cst(biasw), cst(ov4), cst(expand)],
        out_specs=pl.BlockSpec((1, n_rows, HEAD_DIM), lambda b, pt: (b, 0, 0)),
        scratch_shapes=[pltpu.VMEM((2, n_pages * 2 * LANES, page_tokens), F32), pltpu.SemaphoreType.DMA((2,)),
                        pltpu.VMEM((n_rows, past), F32)],
    )
    return pl.pallas_call(
        functools.partial(_sample_attn_kernel, n_pages=n_pages, page_tokens=page_tokens, t_len=t_len,
                          cur_block=cur_block),
        grid_spec=grid_spec,
        out_shape=jax.ShapeDtypeStruct((n_batch, n_rows, HEAD_DIM), F32),
        compiler_params=_cparams("arbitrary"),
        name="sample_attn",
    )(page_table_flat, q, gt, ck, cv, pool, win, ksk, ksv, kwk, kwv, biasc, biasp, biasn, biasw, ov4, expand)


def kernel(x_prompt, x_sample, mem_prompt, cache_cmp_kv, cache_slc_kv, cache_win_kv, cache_mem_kv, page_table, norm_mix, w_in, sgu_norm_g, sgu_norm_b, sgu_w, sgu_b, cmp_w, cmp_b, w_o, rel_bias, norm_x, norm_mem, w_xq, w_mem_kv, w_xo, norm_ffn, w_router, b_router, w_gu, b_gu, w_down, b_down, final_norm):
    assert norm_mix.shape[0] == 1, "single-layer trunk"
    bsz, t_p, _ = x_prompt.shape
    dbs, t_s, _ = x_sample.shape
    n_phys, page_tokens = cache_cmp_kv.shape[1], cache_cmp_kv.shape[2]
    n_pages = page_table.shape[1]
    past = n_pages * page_tokens
    wb = cache_win_kv.shape[2]
    assert wb == WINDOW and t_p >= WINDOW and t_p >= CHUNK and t_s <= CHUNK
    n_p, n_s = bsz * t_p, dbs * t_s
    row1 = lambda v: v.reshape(1, -1)

    w_all = _build_w_in(w_in[0])
    mix_p, mixb_p = _build_sgu_mix(sgu_w[0], sgu_b[0], CHUNK, CHUNK)
    mix_s, mixb_s = _build_sgu_mix(sgu_w[0], sgu_b[0], t_s, ROW_TILE)
    w_cmp = _build_cmp_w(cmp_w[0])
    b_cmp = cmp_b[0].reshape(2, LANES)
    w_oa, w_ob = w_o[0, :SGU_W].astype(BF16), w_o[0, SGU_W:].astype(BF16)
    w_xq_b, w_xo_b = w_xq[0].astype(BF16), w_xo[0].astype(BF16)
    w_rt = w_router[0].T.astype(BF16)
    b_rt = b_router[0].reshape(N_EXPERTS, 1)
    in_args = (row1(norm_mix[0]), w_all, row1(sgu_norm_g[0]), row1(sgu_norm_b[0]))
    tail_args = (w_xo_b, row1(norm_ffn[0]), w_rt, b_rt)

    xp = x_prompt.reshape(n_p, D_MODEL)
    (a_p, v_p, q_p, gt_p, kck, kcv, ksk, ksv, kwk, kwv, kskb, ksvt, kwkb, kwvt) = _inproj(
        xp, *in_args, mix_p, mixb_p, True)
    ck_p, cvt_p = _compress_prompt(kck, kcv, w_cmp, b_cmp, bsz, t_p)
    b_p = _prompt_attn(q_p.reshape(bsz, t_p, N_HEADS * LANES), gt_p.reshape(bsz, t_p, LANES), ck_p, cvt_p,
                       kskb, ksvt, kwkb, kwvt, _attn_tables(rel_bias, t_p), bsz, t_p)
    x1_p, qm_p = _oproj(xp, a_p, b_p.reshape(n_p, NSA_W), w_oa, w_ob, row1(norm_x[0]), w_xq_b)
    memkv_f, memkv_b = _memkv(mem_prompt.reshape(bsz * MEM_LEN, D_MODEL), row1(norm_mem[0]), w_mem_kv[0].astype(BF16))
    x2_p, h_p, te_p, tg_p = _xattn(x1_p, qm_p, memkv_b, *tail_args, t_p, True)

    xs = x_sample.reshape(n_s, D_MODEL)
    (a_s, v_s, q_s, gt_s, kck_s, kcv_s, ksk_s, ksv_s, kwk_s, kwv_s) = _inproj(xs, *in_args, mix_s, mixb_s, False)
    pt_flat = page_table.reshape(-1)
    stored = lambda c: c[0].transpose(0, 2, 3, 4, 1).reshape(c.shape[1], 2 * LANES, c.shape[2])
    pool_c, pool_s = stored(cache_cmp_kv), stored(cache_slc_kv)
    win_t = cache_win_kv[0].transpose(0, 2, 3, 4, 1).reshape(dbs, 2 * LANES, wb)
    hi_new = _hi_new(kck_s, kcv_s, w_cmp, dbs, t_s)
    ck_s, cv_s = _compress_sample(pt_flat, pool_c, hi_new, w_cmp, b_cmp, dbs, n_pages, page_tokens)
    q_sb = q_s.reshape(dbs, t_s, N_HEADS, LANES).transpose(0, 2, 1, 3).reshape(dbs, N_HEADS * t_s, LANES)
    gt_sb = gt_s[:, :3 * N_HEADS].reshape(dbs, t_s, 3, N_HEADS).transpose(0, 3, 1, 2).reshape(dbs, N_HEADS * t_s, 3)
    bo_s = _sample_attn(pt_flat, q_sb, gt_sb, ck_s, cv_s, pool_s, win_t, ksk_s, ksv_s, kwk_s, kwv_s,
                        _sample_tables(rel_bias, past, t_s), dbs, n_pages, page_tokens, t_s)
    b_s = bo_s.reshape(dbs, N_HEADS, t_s, HEAD_DIM).transpose(0, 2, 1, 3).reshape(n_s, NSA_W).astype(BF16)
    x1_s, qm_s = _oproj(xs, a_s, b_s, w_oa, w_ob, row1(norm_x[0]), w_xq_b)
    memkv_s = cache_mem_kv[0].reshape(dbs, MEM_LEN * 2 * MEM_HEADS, MEM_HD)
    x2_s, h_s, te_s, tg_s = _xattn(x1_s, qm_s, memkv_s, *tail_args, t_s, False)

    row_tok, pos, items = _moe_dispatch(jnp.concatenate([te_p, te_s], axis=1))
    xs_sorted = jnp.concatenate([h_p, h_s], axis=0)[row_tok]
    y = _experts(items, xs_sorted, w_gu[0], b_gu[0][:, None, :], w_down[0], b_down[0][:, None, :])
    fn = row1(final_norm)
    y_prompt = _final(x2_p, y[pos[:, :n_p]], tg_p.T, fn).reshape(bsz, t_p, D_MODEL)
    y_sample = _final(x2_s, y[pos[:, n_p:]], tg_s.T, fn).reshape(dbs, t_s, D_MODEL)

    kv6 = lambda k, v, b, t: jnp.concatenate([k, v], axis=1).reshape(1, b, t, 2, N_KV, HEAD_DIM)
    kw_p = kv6(kwk, kwv, bsz, t_p)
    kw_s = kv6(kwk_s, kwv_s, dbs, t_s)
    win_s = jnp.concatenate([cache_win_kv, kw_s], axis=2)[:, :, -wb:]
    return (y_prompt, y_sample, kv6(kck, kcv, bsz, t_p), kv6(ksk, ksv, bsz, t_p), kw_p[:, :, -WINDOW:],
            memkv_f.reshape(1, bsz, MEM_LEN, 2, MEM_HEADS, MEM_HD), v_p.reshape(1, bsz, t_p, SGU_W)[:, :, -CHUNK:],
            kv6(kck_s, kcv_s, dbs, t_s), kv6(ksk_s, ksv_s, dbs, t_s), win_s, v_s.reshape(1, dbs, t_s, SGU_W))
```

```python
import functools
import math

import numpy as np
import jax
import jax.numpy as jnp
from jax import lax
from jax.experimental import pallas as pl
from jax.experimental.pallas import tpu as pltpu

D_MODEL = 1024
SGU_W = 512
SGU_GROUPS = 4
SGU_GW = 128
CHUNK = 128
NSA_W = 512
HEAD_DIM = 64
N_HEADS = 8
N_KV = 2
GQA_R = 4
CMP_LEN = 32
CMP_STRIDE = 16
SEL_BLOCK = 64
N_SELECT = 16
WINDOW = 512
KV_W = 256
N_BUCKETS = 32
MAX_EXACT = 16
MAX_DIST = 128
MEM_LEN = 256
MEM_HEADS = 4
MEM_HD = 128
N_EXPERTS = 32
TOP_K = 4
D_FF = 1024
SWIGLU_LIMIT = 7.0
SWIGLU_ALPHA = 1.702
EPS = 1e-6
NEG = -1e30
FORCE = 1e9

LANES = 128
VMEM_LIMIT = 56 * 1024 * 1024

ROW_TILE = 512
QT = 256
KT = 256
CHAIN_HEADS = 4

F32 = jnp.float32
BF16 = jnp.bfloat16


def _cparams(*sem):
    return pltpu.CompilerParams(dimension_semantics=sem, vmem_limit_bytes=VMEM_LIMIT)


def _const_spec(shape):
    nd = len(shape)
    return pl.BlockSpec(shape, lambda *_: (0,) * nd)


def _rms(x, g):
    return x * lax.rsqrt(jnp.mean(x * x, axis=-1, keepdims=True) + EPS) * g


def _rel_bucket_np(dist):
    n = np.maximum(dist, 0)
    nf = np.maximum(n, 1).astype(np.float32)
    large = MAX_EXACT + (np.log(nf / np.float32(MAX_EXACT)) / np.float32(math.log(MAX_DIST / MAX_EXACT))
                         * np.float32(N_BUCKETS - MAX_EXACT)).astype(np.int32)
    large = np.minimum(large, N_BUCKETS - 1)
    return np.where(n < MAX_EXACT, n, large).astype(np.int32)


C_U, C_V, C_Q, C_KC, C_GT, C_END = 0, 512, 1024, 2048, 2816, 2944


def _inproj_kernel(x_ref, g_ref, w_ref, lng_ref, lnb_ref, mix_ref, mixb_ref, *out_refs, mix_block, attn_extras):
    a_ref, v_ref, q_ref, gt_ref = out_refs[:4]
    x = x_ref[...]
    h = _rms(x, g_ref[...]).astype(BF16)

    def proj(lo, hi):
        return jnp.dot(h, w_ref[:, lo:hi], preferred_element_type=F32)

    q_ref[...] = proj(C_Q, C_KC).astype(BF16)
    gt_ref[...] = proj(C_GT, C_END)
    kv = [proj(C_KC + LANES * j, C_KC + LANES * (j + 1)) for j in range(6)]
    if attn_extras:
        kvt_refs = out_refs[4:7]
        kskb_ref, ksvt_ref, kwkb_ref, kwvt_ref = out_refs[7:11]
        kskb_ref[...] = kv[2].astype(BF16)
        kwkb_ref[...] = kv[4].astype(BF16)
        for j, (kvt_ref, vtile_ref) in enumerate(zip(kvt_refs, (None, ksvt_ref, kwvt_ref))):
            kvt_ref[0, :LANES, :] = kv[2 * j].T
            vt = kv[2 * j + 1].T
            kvt_ref[0, LANES:, :] = vt
            if vtile_ref is not None:
                vtb = vt.astype(BF16)
                for t in range(vt.shape[1] // KT):
                    vtile_ref[t] = vtb[:, t * KT:(t + 1) * KT]
    else:
        for r, val in zip(out_refs[4:10], kv):
            r[...] = val

    u = jax.nn.gelu(proj(C_U, C_V))
    v = jax.nn.gelu(proj(C_V, C_Q))
    mu = jnp.mean(v, axis=-1, keepdims=True)
    var = jnp.mean(jnp.square(v - mu), axis=-1, keepdims=True)
    v = (v - mu) * lax.rsqrt(var + EPS) * lng_ref[...] + lnb_ref[...]
    v_ref[...] = v
    vb = v.astype(BF16)
    rows = x.shape[0]
    for blk in range(rows // mix_block):
        r0 = blk * mix_block
        for g in range(SGU_GROUPS):
            c0 = g * SGU_GW
            mixed = jnp.dot(mix_ref[g], vb[r0:r0 + mix_block, c0:c0 + SGU_GW],
                            preferred_element_type=F32) + mixb_ref[:, c0:c0 + SGU_GW]
            a_ref[r0:r0 + mix_block, c0:c0 + SGU_GW] = (
                u[r0:r0 + mix_block, c0:c0 + SGU_GW] * mixed).astype(BF16)


def _inproj(x2d, g, w_all, ln_g, ln_b, mix, mixb, t_len, attn_extras):
    n = x2d.shape[0]
    mb = mix.shape[1]
    row = lambda c: pl.BlockSpec((ROW_TILE, c), lambda i: (i, 0))
    outs = [(SGU_W, BF16), (SGU_W, F32), (N_HEADS * LANES, BF16), (LANES, F32)]
    if not attn_extras:
        outs += [(LANES, F32)] * 6
    out_specs = [row(c) for c, _ in outs]
    out_shape = [jax.ShapeDtypeStruct((n, c), dt) for c, dt in outs]
    if attn_extras:
        per_b = t_len // ROW_TILE
        kvt_spec = pl.BlockSpec((1, 2 * LANES, ROW_TILE), lambda i: (i // per_b, 0, i % per_b))
        kvt_shape = jax.ShapeDtypeStruct((n // t_len, 2 * LANES, t_len), F32)
        tiles = ROW_TILE // KT
        vt_spec = pl.BlockSpec((tiles, LANES, KT), lambda i: (i, 0, 0))
        vt_shape = jax.ShapeDtypeStruct((n // KT, LANES, KT), BF16)
        out_specs += [kvt_spec] * 3 + [row(LANES), vt_spec, row(LANES), vt_spec]
        out_shape += [kvt_shape] * 3 + [jax.ShapeDtypeStruct((n, LANES), BF16), vt_shape,
                                        jax.ShapeDtypeStruct((n, LANES), BF16), vt_shape]
    return pl.pallas_call(
        functools.partial(_inproj_kernel, mix_block=mb, attn_extras=attn_extras),
        grid=(n // ROW_TILE,),
        in_specs=[row(D_MODEL), _const_spec((1, D_MODEL)), _const_spec(w_all.shape),
                  _const_spec((1, SGU_W)), _const_spec((1, SGU_W)), _const_spec(mix.shape),
                  _const_spec(mixb.shape)],
        out_specs=out_specs,
        out_shape=out_shape,
        compiler_params=_cparams("parallel"),
        name="inproj_sgu",
    )(x2d, g, w_all, ln_g, ln_b, mix, mixb)


def _build_w_in(w_in):
    u = w_in[:, 0:512]
    v = w_in[:, 512:1024]
    q = w_in[:, 1024:1536].reshape(D_MODEL, N_KV, GQA_R, HEAD_DIM) * (HEAD_DIM ** -0.5)
    qp = jnp.zeros((D_MODEL, N_KV, GQA_R, N_KV, HEAD_DIM), F32)
    for g in range(N_KV):
        qp = qp.at[:, g, :, g, :].set(q[:, g])
    qp = qp.reshape(D_MODEL, N_HEADS * LANES)
    kv = w_in[:, 1536:2304]
    gt = jnp.pad(w_in[:, 2304:2328], ((0, 0), (0, LANES - 3 * N_HEADS)))
    return jnp.concatenate([u, v, qp, kv, gt], axis=1).astype(BF16)


def _build_sgu_mix(sgu_w, sgu_b, chunk, mix_block):
    tri = jnp.tril(jnp.ones((chunk, chunk), bool))
    w = jnp.where(tri, sgu_w[:, :chunk, :chunk], 0.0)
    reps = mix_block // chunk
    eye = jnp.eye(reps, dtype=F32)
    mix = jnp.einsum('ab,gst->gasbt', eye, w).reshape(SGU_GROUPS, mix_block, mix_block)
    b = jnp.tile(sgu_b[:, :chunk], (1, reps))
    mixb = jnp.repeat(b.T, SGU_GW, axis=1)
    return mix.astype(BF16), mixb


def _build_cmp_w(cmp_w):
    w = cmp_w.reshape(2, N_KV, 2, CMP_STRIDE, HEAD_DIM, HEAD_DIM)
    out = jnp.zeros((2, CMP_STRIDE, N_KV, HEAD_DIM, 2, N_KV, HEAD_DIM), F32)
    for g in range(N_KV):
        out = out.at[:, :, g, :, :, g, :].set(jnp.transpose(w[:, g], (0, 2, 3, 1, 4)))
    return out.reshape(2, CMP_STRIDE, LANES, 2 * LANES).astype(BF16)


PAGE = 128
PER_PAGE = PAGE // CMP_STRIDE


def _regroup_perm():
    tok = np.arange(PAGE)
    return jnp.asarray(np.arange(PAGE)[:, None] == ((tok % CMP_STRIDE) * PER_PAGE + tok // CMP_STRIDE)[None, :], BF16)


def _regroup_page(page, perm_ref, tok_ref, block0):
    t = lax.dot_general(perm_ref[...], page.astype(BF16), (((1,), (1,)), ((), ())), preferred_element_type=F32)
    for r in range(CMP_STRIDE):
        for c in range(2):
            tok_ref[c, r // 2, pl.ds(block0, PER_PAGE), (r % 2) * LANES:(r % 2 + 1) * LANES] = (
                t[r * PER_PAGE:(r + 1) * PER_PAGE, c * LANES:(c + 1) * LANES])


def _compress_halves(tok_ref, w_ref, c):
    acc = jnp.dot(tok_ref[c, 0].astype(BF16), w_ref[c, 0], preferred_element_type=F32)
    for q in range(1, CMP_STRIDE // 2):
        acc = acc + jnp.dot(tok_ref[c, q].astype(BF16), w_ref[c, q], preferred_element_type=F32)
    return acc


def _compress_kernel(kct_ref, w_ref, b_ref, perm_ref, ck_ref, cvt_ref, tok_ref, *, n_half):
    for p in range(n_half // PER_PAGE):
        _regroup_page(kct_ref[0, :, p * PAGE:(p + 1) * PAGE], perm_ref, tok_ref, p * PER_PAGE)
    for c in range(2):
        acc = _compress_halves(tok_ref, w_ref, c)
        hi_next = pltpu.roll(acc[:, LANES:], n_half - 1, 0)
        comp = acc[:, :LANES] + hi_next + b_ref[c:c + 1, :]
        if c == 0:
            ck_ref[0] = comp.astype(BF16)
        else:
            cvt_ref[0] = comp.T.astype(BF16)


def _compress_prompt(kct, w_cmp, b_cmp, bsz, t_len):
    n_half = t_len // CMP_STRIDE
    perm = _regroup_perm()
    w2 = w_cmp.reshape(2, CMP_STRIDE // 2, 2 * LANES, 2 * LANES)
    return pl.pallas_call(
        functools.partial(_compress_kernel, n_half=n_half),
        grid=(bsz,),
        in_specs=[pl.BlockSpec((1, 2 * LANES, t_len), lambda b: (b, 0, 0)),
                  _const_spec(w2.shape), _const_spec(b_cmp.shape), _const_spec(perm.shape)],
        out_specs=[pl.BlockSpec((1, n_half, LANES), lambda b: (b, 0, 0)),
                   pl.BlockSpec((1, LANES, n_half), lambda b: (b, 0, 0))],
        out_shape=[jax.ShapeDtypeStruct((bsz, n_half, LANES), BF16),
                   jax.ShapeDtypeStruct((bsz, LANES, n_half), BF16)],
        scratch_shapes=[pltpu.VMEM((2, CMP_STRIDE // 2, n_half, 2 * LANES), F32)],
        compiler_params=_cparams("parallel"),
        name="compress_prompt",
    )(kct, w2, b_cmp, perm)


def _toeplitz(rel_bias, rows, cols, off, valid_fn):
    span = rows + cols - 1
    d = np.arange(span) + off - (rows - 1)
    vec = jnp.where(valid_fn(d)[None, :], rel_bias[_rel_bucket_np(d)].T, NEG)
    flat = jnp.tile(vec, (1, rows + 1))[:, :rows * (span + 1)]
    skew = flat.reshape(N_HEADS, rows, span + 1)
    return skew[:, ::-1, :cols]


def _head_table(rel_bias, dist, valid, head):
    bucket = _rel_bucket_np(dist)
    need = valid & (bucket != N_BUCKETS - 1)
    out = jnp.broadcast_to(rel_bias[N_BUCKETS - 1][head[:, 0]][:, None], dist.shape)
    cols = np.nonzero(need.any(axis=0))[0]
    if cols.size:
        c0, c1 = int(cols[0]), int(cols[-1]) + 1
        mid = jnp.where(need[:, c0:c1], rel_bias[bucket[:, c0:c1], head], out[:, c0:c1])
        out = jnp.concatenate([out[:, :c0], mid, out[:, c1:]], axis=1)
    return jnp.where(valid, out, NEG)


def _attn_tables(rel_bias, t_len):
    n_half = t_len // CMP_STRIDE
    n_sel = t_len // SEL_BLOCK
    nq = t_len // QT
    c_far = rel_bias[N_BUCKETS - 1]
    to_lanes = lambda a: a.reshape(N_KV, GQA_R, a.shape[1], QT).transpose(0, 2, 1, 3).reshape(
        N_KV, a.shape[1], GQA_R * QT)
    near = jnp.stack([
        to_lanes(_toeplitz(rel_bias, KT, QT, 0, lambda d: d >= 0)),
        to_lanes(_toeplitz(rel_bias, KT, QT, KT, lambda d: d >= 0)),
        to_lanes(_toeplitz(rel_bias, KT, QT, 2 * KT, lambda d: d < WINDOW))])
    far = jnp.repeat(c_far.reshape(N_KV, 1, GQA_R), QT, axis=2).reshape(N_KV, 1, GQA_R * QT)
    per_q = QT // CMP_STRIDE
    band_rows = 2 * per_q
    lead = per_q // 2
    off = lead * CMP_STRIDE - (CMP_LEN - 1)
    band = _toeplitz(rel_bias, band_rows * CMP_STRIDE, QT, off, lambda d: d >= 0)[:, ::CMP_STRIDE]
    assert _rel_bucket_np(np.array([off + CMP_STRIDE])).item() == N_BUCKETS - 1
    tiles = []
    for i in range(nq):
        lo = i * per_q - lead
        parts = []
        if lo > 0:
            parts.append(jnp.broadcast_to(c_far[:, None, None], (N_HEADS, lo, QT)))
        b0, b1 = max(0, -lo), min(band_rows, n_half - lo)
        parts.append(band[:, b0:b1])
        rest = n_half - (lo + b1)
        if rest > 0:
            parts.append(jnp.full((N_HEADS, rest, QT), NEG, F32))
        tiles.append(to_lanes(jnp.concatenate(parts, axis=1)))
    biasc = jnp.stack(tiles)
    cs = np.arange(n_half - 1)[None, :] * CMP_STRIDE
    ss = np.arange(n_sel)[:, None] * SEL_BLOCK
    ov = np.zeros((n_sel, n_half), np.float32)
    ov[:, :n_half - 1] = (cs < ss + SEL_BLOCK) & (cs + CMP_LEN > ss)
    ovt4 = jnp.asarray(np.tile(ov, (1, GQA_R)), BF16)
    return near, far, biasc, ovt4


def _prompt_attn_kernel(q_ref, gt_ref, ck_ref, cvt_ref, ksk_ref, ksvt_ref, kwk_ref, kwvt_ref,
                        biasc_ref, near_ref, far_ref, ovt_ref, o_ref,
                        imp_ref, selneg_ref, m_ref, l_ref, acc_ref, oc_ref, os_ref, outt_ref, *, n_sel):
    i = pl.program_id(1)
    nt = (((1,), (1,)), ((), ()))
    groups = range(N_KV)
    own = lambda g: slice(g * HEAD_DIM, (g + 1) * HEAD_DIM)
    jj = lax.broadcasted_iota(jnp.int32, (n_sel, QT), 0)
    tt = i * QT + lax.broadcasted_iota(jnp.int32, (n_sel, QT), 1)
    cur = lax.shift_right_logical(tt, 6)
    forced = (jj == 0) | (jj == cur) | (jj == cur - 1)
    future = jj > cur
    qs = [jnp.concatenate([q_ref[0, :, (g * GQA_R + r) * LANES:(g * GQA_R + r + 1) * LANES]
                           for r in range(GQA_R)], axis=0) for g in groups]

    near = lambda idx: (lambda g, cols: near_ref[idx, g, :, cols])
    far = lambda g, cols: far_ref[g, :, cols]

    def online_step(k_ref, vt_ref, kt, bias_of, mask_of, shift_of, first):
        kk = k_ref[pl.ds(pl.multiple_of(kt * KT, KT), KT), :]
        vt = vt_ref[kt]
        for g in groups:
            mask = None if mask_of is None else jnp.concatenate([mask_of(g)] * CHAIN_HEADS, axis=1)
            for c0 in range(0, GQA_R, CHAIN_HEADS):
                cols = slice(c0 * QT, (c0 + CHAIN_HEADS) * QT)
                s = lax.dot_general(kk, qs[g][cols], nt, preferred_element_type=F32)
                if bias_of is not None:
                    s = s + bias_of(g, cols)
                if mask is not None:
                    s = s + mask
                mt = jnp.max(s, axis=0, keepdims=True)
                shift = None if shift_of is None else shift_of(g, cols)
                if shift is not None:
                    mt = mt + shift
                if first:
                    m_new = mt
                else:
                    m_old = m_ref[g, :, cols]
                    m_new = jnp.maximum(m_old, mt)
                    alpha = jnp.exp(m_old - m_new)
                p = jnp.exp(s - (m_new if shift is None else m_new - shift))
                pv = jnp.dot(vt, p.astype(BF16), preferred_element_type=F32)
                ps = jnp.sum(p, axis=0, keepdims=True)
                if first:
                    l_ref[g, :, cols] = ps
                    acc_ref[g, :, cols] = pv
                else:
                    l_ref[g, :, cols] = alpha * l_ref[g, :, cols] + ps
                    acc_ref[g, :, cols] = alpha * acc_ref[g, :, cols] + pv
                m_ref[g, :, cols] = m_new

    def sel_mask(g, kt):
        rows = [jnp.broadcast_to(selneg_ref[g, pl.ds(kt * (KT // SEL_BLOCK) + b, 1), :], (SEL_BLOCK, QT))
                for b in range(KT // SEL_BLOCK)]
        return jnp.concatenate(rows, axis=0)

    for g in groups:
        bc = biasc_ref[0, g]
        sc = lax.dot_general(ck_ref[0], qs[g], nt, preferred_element_type=F32) + bc
        e = jnp.exp(sc - jnp.max(sc, axis=0, keepdims=True))
        e = jnp.where(bc > 0.5 * NEG, e, 0.0)
        lsum = jnp.sum(e, axis=0, keepdims=True)
        pb = (e / jnp.where(lsum > 0.0, lsum, 1.0)).astype(BF16)
        oc_ref[g] = jnp.dot(cvt_ref[0], pb, preferred_element_type=F32)[own(g)]
        p4 = jnp.concatenate([pb[:, r * QT:(r + 1) * QT] for r in range(GQA_R)], axis=0)
        imp = jnp.dot(ovt_ref[...], p4, preferred_element_type=F32)
        imp = jnp.where(forced, FORCE, imp)
        imp = jnp.where(future, NEG, imp)
        imp_ref[...] = imp

        def rank_body(ii, cnt, imp=imp):
            row = imp_ref[pl.ds(ii, 1), :]
            beats = (row > imp) | ((row == imp) & (jj > ii))
            return cnt + jnp.where(beats, 1.0, 0.0)

        cnt = lax.fori_loop(0, n_sel, rank_body, jnp.zeros((n_sel, QT), F32), unroll=8)
        selneg_ref[g] = jnp.where((cnt < N_SELECT) & jnp.logical_not(future), 0.0, NEG)

    online_step(ksk_ref, ksvt_ref, i, near(0), lambda g: sel_mask(g, i), None, True)

    @pl.when(i >= 1)
    def _():
        online_step(ksk_ref, ksvt_ref, i - 1, near(1), lambda g: sel_mask(g, i - 1), None, False)

    def far_body(kt, carry):
        online_step(ksk_ref, ksvt_ref, kt, None, lambda g: sel_mask(g, kt), far, False)
        return carry

    lax.fori_loop(0, jnp.maximum(i - 1, 0), far_body, 0)
    for g in groups:
        os_ref[g] = acc_ref[g, own(g), :] / l_ref[g]
    online_step(kwk_ref, kwvt_ref, i, near(0), None, None, True)

    @pl.when(i >= 1)
    def _():
        online_step(kwk_ref, kwvt_ref, i - 1, near(1), None, None, False)

    @pl.when(i >= 2)
    def _():
        online_step(kwk_ref, kwvt_ref, i - 2, near(2), None, None, False)

    gsig = jax.nn.sigmoid(gt_ref[0]).T
    for g in groups:
        o_c, o_s, o_w = oc_ref[g], os_ref[g], acc_ref[g, own(g), :] / l_ref[g]
        for r in range(GQA_R):
            h = g * GQA_R + r
            cols = slice(r * QT, (r + 1) * QT)
            outt_ref[h * HEAD_DIM:(h + 1) * HEAD_DIM, :] = (
                gsig[h:h + 1, :] * o_c[:, cols]
                + gsig[N_HEADS + h:N_HEADS + h + 1, :] * o_s[:, cols]
                + gsig[2 * N_HEADS + h:2 * N_HEADS + h + 1, :] * o_w[:, cols])
    o_ref[0] = outt_ref[...].T.astype(BF16)


def _prompt_attn(q, gt, ck, cvt, ksk, ksvt, kwk, kwvt, tables, bsz, t_len):
    near, far, biasc, ovt4 = tables
    n_half = t_len // CMP_STRIDE
    n_sel = t_len // SEL_BLOCK
    nkt = t_len // KT
    per_b2 = lambda c: pl.BlockSpec((t_len, c), lambda b, i: (b, 0))
    vt_spec = pl.BlockSpec((nkt, LANES, KT), lambda b, i: (b, 0, 0))
    return pl.pallas_call(
        functools.partial(_prompt_attn_kernel, n_sel=n_sel),
        grid=(bsz, t_len // QT),
        in_specs=[pl.BlockSpec((1, QT, N_HEADS * LANES), lambda b, i: (b, i, 0)),
                  pl.BlockSpec((1, QT, LANES), lambda b, i: (b, i, 0)),
                  pl.BlockSpec((1, n_half, LANES), lambda b, i: (b, 0, 0)),
                  pl.BlockSpec((1, LANES, n_half), lambda b, i: (b, 0, 0)),
                  per_b2(LANES), vt_spec, per_b2(LANES), vt_spec,
                  pl.BlockSpec((1, N_KV, n_half, GQA_R * QT), lambda b, i: (i, 0, 0, 0)),
                  _const_spec(near.shape), _const_spec(far.shape), _const_spec(ovt4.shape)],
        out_specs=pl.BlockSpec((1, QT, NSA_W), lambda b, i: (b, i, 0)),
        out_shape=jax.ShapeDtypeStruct((bsz, t_len, NSA_W), BF16),
        scratch_shapes=[pltpu.VMEM((n_sel, QT), F32), pltpu.VMEM((N_KV, n_sel, QT), F32),
                        pltpu.VMEM((N_KV, 1, GQA_R * QT), F32), pltpu.VMEM((N_KV, 1, GQA_R * QT), F32),
                        pltpu.VMEM((N_KV, LANES, GQA_R * QT), F32),
                        pltpu.VMEM((N_KV, HEAD_DIM, GQA_R * QT), F32), pltpu.VMEM((N_KV, HEAD_DIM, GQA_R * QT), F32),
                        pltpu.VMEM((NSA_W, QT), F32)],
        compiler_params=_cparams("parallel", "arbitrary"),
        name="prompt_attn",
    )(q, gt, ck, cvt, ksk, ksvt, kwk, kwvt, biasc, near, far, ovt4)


def _oproj_kernel(x_ref, a_ref, b_ref, woa_ref, wob_ref, g_ref, wq_ref, x1_ref, q_ref):
    x1 = (x_ref[...] + jnp.dot(a_ref[...], woa_ref[...], preferred_element_type=F32)
          + jnp.dot(b_ref[...], wob_ref[...], preferred_element_type=F32))
    x1_ref[...] = x1
    h = _rms(x1, g_ref[...]).astype(BF16)
    q_ref[...] = jnp.dot(h, wq_ref[...], preferred_element_type=F32).astype(BF16)


def _oproj(x2d, a, b, w_oa, w_ob, g, w_xq):
    n = x2d.shape[0]
    row = lambda c: pl.BlockSpec((ROW_TILE, c), lambda i: (i, 0))
    hq = MEM_HEADS * MEM_HD
    return pl.pallas_call(
        _oproj_kernel,
        grid=(n // ROW_TILE,),
        in_specs=[row(D_MODEL), row(SGU_W), row(NSA_W), _const_spec(w_oa.shape), _const_spec(w_ob.shape),
                  _const_spec((1, D_MODEL)), _const_spec(w_xq.shape)],
        out_specs=[row(D_MODEL), row(hq)],
        out_shape=[jax.ShapeDtypeStruct((n, D_MODEL), F32), jax.ShapeDtypeStruct((n, hq), BF16)],
        compiler_params=_cparams("parallel"),
        name="oproj_xq",
    )(x2d, a, b, w_oa, w_ob, g, w_xq)


def _memkv_kernel(x_ref, g_ref, w_ref, o_ref, ob_ref):
    h = _rms(x_ref[...], g_ref[...]).astype(BF16)
    o = jnp.dot(h, w_ref[...], preferred_element_type=F32)
    o_ref[...] = o
    ob_ref[...] = o.astype(BF16)


def _memkv(mem2d, g, w):
    n = mem2d.shape[0]
    c = w.shape[1]
    row = lambda cc: pl.BlockSpec((ROW_TILE, cc), lambda i: (i, 0))
    return pl.pallas_call(
        _memkv_kernel,
        grid=(n // ROW_TILE,),
        in_specs=[row(D_MODEL), _const_spec((1, D_MODEL)), _const_spec(w.shape)],
        out_specs=[row(c), row(c)],
        out_shape=[jax.ShapeDtypeStruct((n, c), F32), jax.ShapeDtypeStruct((n, c), BF16)],
        compiler_params=_cparams("parallel"),
        name="memkv_proj",
    )(mem2d, g, w)


def _softmax_rows(s):
    e = jnp.exp(s - jnp.max(s, axis=-1, keepdims=True))
    return e / jnp.sum(e, axis=-1, keepdims=True)


def _xattn_tail(x1, o, wxo_ref, g_ref, wr_ref, br_ref, x2_ref, h_ref, te_ref, tg_ref):
    x2 = x1 + jnp.dot(o.astype(BF16), wxo_ref[...], preferred_element_type=F32)
    x2_ref[...] = x2
    hb = _rms(x2, g_ref[...]).astype(BF16)
    h_ref[...] = hb
    lt = lax.dot_general(wr_ref[...], hb, (((1,), (1,)), ((), ())), preferred_element_type=F32) + br_ref[...]
    eidx = lax.broadcasted_iota(jnp.int32, lt.shape, 0)
    tops, idxs = [], []
    for _ in range(TOP_K):
        m = jnp.max(lt, axis=0, keepdims=True)
        idx = jnp.min(jnp.where(lt == m, eidx, N_EXPERTS), axis=0, keepdims=True)
        tops.append(m)
        idxs.append(idx)
        lt = jnp.where(eidx == idx, -jnp.inf, lt)
    es = [jnp.exp(t - tops[0]) for t in tops]
    den = es[0] + es[1] + es[2] + es[3]
    te_ref[...] = jnp.concatenate(idxs, axis=0)
    tg_ref[...] = jnp.concatenate([e / den for e in es], axis=0)


def _xattn_prompt_kernel(x1_ref, q_ref, kv_ref, wxo_ref, g_ref, wr_ref, br_ref, x2_ref, h_ref, te_ref, tg_ref):
    outs = []
    for hh in range(MEM_HEADS):
        qh = q_ref[:, hh * MEM_HD:(hh + 1) * MEM_HD]
        kh = kv_ref[:, hh * MEM_HD:(hh + 1) * MEM_HD]
        vh = kv_ref[:, (MEM_HEADS + hh) * MEM_HD:(MEM_HEADS + hh + 1) * MEM_HD]
        s = lax.dot_general(qh, kh, (((1,), (1,)), ((), ())), preferred_element_type=F32) * (MEM_HD ** -0.5)
        outs.append(jnp.dot(_softmax_rows(s).astype(BF16), vh, preferred_element_type=F32))
    o = jnp.concatenate(outs, axis=1)
    _xattn_tail(x1_ref[...], o, wxo_ref, g_ref, wr_ref, br_ref, x2_ref, h_ref, te_ref, tg_ref)


def _xattn_sample_kernel(x1_ref, q_ref, kv_ref, wxo_ref, g_ref, wr_ref, br_ref, x2_ref, h_ref, te_ref, tg_ref,
                         o_scr, *, t_len):
    nb = q_ref.shape[0] // t_len
    qf = q_ref[...].astype(F32)
    for bb in range(nb):
        for hh in range(MEM_HEADS):
            qh = qf[bb * t_len:(bb + 1) * t_len, hh * MEM_HD:(hh + 1) * MEM_HD].astype(BF16)
            kh = kv_ref[bb, pl.ds(hh, MEM_LEN, stride=2 * MEM_HEADS), :].astype(BF16)
            vh = kv_ref[bb, pl.ds(MEM_HEADS + hh, MEM_LEN, stride=2 * MEM_HEADS), :].astype(BF16)
            s = lax.dot_general(qh, kh, (((1,), (1,)), ((), ())), preferred_element_type=F32) * (MEM_HD ** -0.5)
            o_scr[bb * t_len:(bb + 1) * t_len, hh * MEM_HD:(hh + 1) * MEM_HD] = jnp.dot(
                _softmax_rows(s).astype(BF16), vh, preferred_element_type=F32)
    _xattn_tail(x1_ref[...], o_scr[...], wxo_ref, g_ref, wr_ref, br_ref, x2_ref, h_ref, te_ref, tg_ref)


XS_BATCH = 16


def _xattn(x1, q, kv, w_xo, g, w_rt, b_r, t_len, prompt):
    n = x1.shape[0]
    hq = MEM_HEADS * MEM_HD
    if prompt:
        rows = ROW_TILE
        per_b = t_len // rows
        kv_spec = pl.BlockSpec((MEM_LEN, 2 * hq), lambda i: (i // per_b, 0))
        kern = _xattn_prompt_kernel
        scratch = []
    else:
        rows = XS_BATCH * t_len
        kv_spec = pl.BlockSpec((XS_BATCH, MEM_LEN * 2 * MEM_HEADS, MEM_HD), lambda i: (i, 0, 0))
        kern = functools.partial(_xattn_sample_kernel, t_len=t_len)
        scratch = [pltpu.VMEM((rows, hq), F32)]
    row = lambda c: pl.BlockSpec((rows, c), lambda i: (i, 0))
    col = pl.BlockSpec((TOP_K, rows), lambda i: (0, i))
    return pl.pallas_call(
        kern,
        grid=(n // rows,),
        in_specs=[row(D_MODEL), row(hq), kv_spec, _const_spec(w_xo.shape), _const_spec((1, D_MODEL)),
                  _const_spec(w_rt.shape), _const_spec(b_r.shape)],
        out_specs=[row(D_MODEL), row(D_MODEL), col, col],
        out_shape=[jax.ShapeDtypeStruct((n, D_MODEL), F32), jax.ShapeDtypeStruct((n, D_MODEL), BF16),
                   jax.ShapeDtypeStruct((TOP_K, n), jnp.int32), jax.ShapeDtypeStruct((TOP_K, n), F32)],
        scratch_shapes=scratch,
        compiler_params=_cparams("parallel"),
        name="xattn_router_prompt" if prompt else "xattn_router_sample",
    )(x1, q, kv, w_xo, g, w_rt, b_r)


MOE_BLOCK = 256


def _expert_kernel(blk_ref, exp_ref, lo_ref, hi_ref, x_ref, wgu_ref, bgu_ref, wd_ref, bd_ref, y_ref,
                   wgu_b, wd_b):
    w = pl.program_id(0)
    lo, hi, row0 = lo_ref[w], hi_ref[w], blk_ref[w] * MOE_BLOCK

    @pl.when((w == 0) | (exp_ref[w] != exp_ref[jnp.maximum(w - 1, 0)]))
    def _():
        wgu_b[...] = wgu_ref[0].astype(BF16)
        wd_b[...] = wd_ref[0].astype(BF16)

    @pl.when(hi > lo)
    def _():
        gu = jnp.dot(x_ref[...], wgu_b[...], preferred_element_type=F32) + bgu_ref[0]
        glu = jnp.minimum(gu[:, :D_FF], SWIGLU_LIMIT)
        lin = jnp.clip(gu[:, D_FF:], -SWIGLU_LIMIT, SWIGLU_LIMIT)
        hdn = glu * jax.nn.sigmoid(SWIGLU_ALPHA * glu) * (lin + 1.0)
        y = jnp.dot(hdn.astype(BF16), wd_b[...], preferred_element_type=F32) + bd_ref[0]
        row = row0 + lax.broadcasted_iota(jnp.int32, (MOE_BLOCK, 1), 0)
        mine = (row >= lo) & (row < hi)

        @pl.when(lo == row0)
        def _():
            y_ref[...] = jnp.where(mine, y, 0.0)

        @pl.when(lo != row0)
        def _():
            y_ref[...] = jnp.where(mine, y, y_ref[...])


def _experts(items, xs, w_gu, b_gu, w_down, b_down):
    blk, exp, lo, hi = items
    n_rows = xs.shape[0]
    im = lambda f: (lambda w, blk, exp, lo, hi: f(blk[w], exp[w]))
    grid_spec = pltpu.PrefetchScalarGridSpec(
        num_scalar_prefetch=4,
        grid=(blk.shape[0],),
        in_specs=[pl.BlockSpec((MOE_BLOCK, D_MODEL), im(lambda b, e: (b, 0))),
                  pl.BlockSpec((1, D_MODEL, 2 * D_FF), im(lambda b, e: (e, 0, 0))),
                  pl.BlockSpec((1, 1, 2 * D_FF), im(lambda b, e: (e, 0, 0))),
                  pl.BlockSpec((1, D_FF, D_MODEL), im(lambda b, e: (e, 0, 0))),
                  pl.BlockSpec((1, 1, D_MODEL), im(lambda b, e: (e, 0, 0)))],
        out_specs=pl.BlockSpec((MOE_BLOCK, D_MODEL), im(lambda b, e: (b, 0))),
        scratch_shapes=[pltpu.VMEM((D_MODEL, 2 * D_FF), BF16), pltpu.VMEM((D_FF, D_MODEL), BF16)],
    )
    return pl.pallas_call(
        _expert_kernel,
        grid_spec=grid_spec,
        out_shape=jax.ShapeDtypeStruct((n_rows, D_MODEL), F32),
        compiler_params=_cparams("arbitrary"),
        name="moe_experts",
    )(blk, exp, lo, hi, xs, w_gu, b_gu, w_down, b_down)


def _moe_dispatch(te):
    n = te.shape[1]
    nk = TOP_K * n
    assert nk % MOE_BLOCK == 0
    n_blk = nk // MOE_BLOCK
    iota = jnp.arange(nk, dtype=jnp.int32)
    flat_e = te.reshape(nk)
    _, order = lax.sort_key_val(flat_e, iota)
    pos = lax.sort_key_val(order, iota)[1].reshape(TOP_K, n)
    counts = jnp.sum(flat_e[:, None] == jnp.arange(N_EXPERTS)[None, :], axis=0, dtype=jnp.int32)
    start = jnp.cumsum(counts) - counts
    lo = jnp.sort(jnp.concatenate([jnp.arange(n_blk, dtype=jnp.int32) * MOE_BLOCK, start[1:]]))
    hi = jnp.concatenate([lo[1:], jnp.full((1,), nk, jnp.int32)])
    blk = jnp.minimum(lo // MOE_BLOCK, n_blk - 1)
    exp = jnp.sum(start[None, 1:] <= lo[:, None], axis=1, dtype=jnp.int32)
    return order % n, pos, (blk, exp, lo, hi)


def _final_kernel(x_ref, y_ref, g_ref, fn_ref, o_ref):
    x = x_ref[...]
    for k in range(TOP_K):
        x = x + y_ref[k] * g_ref[:, k:k + 1]
    o_ref[...] = _rms(x, fn_ref[...])


def _final(x2, yk, gates, fnorm):
    n = x2.shape[0]
    rows = 256
    return pl.pallas_call(
        _final_kernel,
        grid=(n // rows,),
        in_specs=[pl.BlockSpec((rows, D_MODEL), lambda i: (i, 0)),
                  pl.BlockSpec((TOP_K, rows, D_MODEL), lambda i: (0, i, 0)),
                  pl.BlockSpec((rows, TOP_K), lambda i: (i, 0)), _const_spec((1, D_MODEL))],
        out_specs=pl.BlockSpec((rows, D_MODEL), lambda i: (i, 0)),
        out_shape=jax.ShapeDtypeStruct((n, D_MODEL), F32),
        compiler_params=_cparams("parallel"),
        name="moe_combine_final_norm",
    )(x2, yk, gates, fnorm)


def _page_copy(pool_ref, pt_ref, buf_ref, sem_ref, b, p, slot, n_pages, rows):
    return pltpu.make_async_copy(pool_ref.at[pt_ref[b * n_pages + p]],
                                 buf_ref.at[slot, pl.ds(p * rows, rows)], sem_ref.at[slot])


def _fetch_pages(pool_ref, pt_ref, buf_ref, sem_ref, b, slot, n_pages, rows):
    def body(p, c):
        _page_copy(pool_ref, pt_ref, buf_ref, sem_ref, b, p, slot, n_pages, rows).start()
        return c
    lax.fori_loop(0, n_pages, body, 0)


def _wait_pages(pool_ref, pt_ref, buf_ref, sem_ref, b, slot, n_pages, rows):
    def body(p, c):
        _page_copy(pool_ref, pt_ref, buf_ref, sem_ref, b, p, slot, n_pages, rows).wait()
        return c
    lax.fori_loop(0, n_pages, body, 0)


def _compress_sample_kernel(pt_ref, pool_ref, hi_ref, w_ref, b_ref, perm_ref, ck_ref, cv_ref, buf_ref, sem_ref, tok_ref,
                            *, n_pages, page_rows):
    b = pl.program_id(0)
    nb = pl.num_programs(0)
    slot = lax.rem(b, 2)
    rows = 2 * LANES
    n_half = n_pages * page_rows // CMP_STRIDE

    @pl.when(b == 0)
    def _():
        _fetch_pages(pool_ref, pt_ref, buf_ref, sem_ref, b, slot, n_pages, rows)

    @pl.when(b + 1 < nb)
    def _():
        _fetch_pages(pool_ref, pt_ref, buf_ref, sem_ref, b + 1, 1 - slot, n_pages, rows)

    _wait_pages(pool_ref, pt_ref, buf_ref, sem_ref, b, slot, n_pages, rows)

    def regroup(p, carry):
        page = buf_ref[slot, pl.ds(pl.multiple_of(p * rows, rows), rows), :]
        _regroup_page(page, perm_ref, tok_ref, pl.multiple_of(p * PER_PAGE, PER_PAGE))
        return carry

    lax.fori_loop(0, n_pages, regroup, 0, unroll=8)
    last = lax.broadcasted_iota(jnp.int32, (n_half, LANES), 0) == n_half - 1
    for c, out in enumerate((ck_ref, cv_ref)):
        acc = _compress_halves(tok_ref, w_ref, c)
        hi_next = pltpu.roll(acc[:, LANES:], n_half - 1, 0)
        hi_next = jnp.where(last, hi_ref[0, c:c + 1, :], hi_next)
        out[0] = (acc[:, :LANES] + hi_next + b_ref[c:c + 1, :]).astype(BF16)


def _hi_new_kernel(ak_ref, av_ref, w_ref, o_ref):
    for c, a_ref in enumerate((ak_ref, av_ref)):
        o_ref[:, c, :] = jnp.dot(a_ref[...].astype(BF16), w_ref[c], preferred_element_type=F32)


def _hi_new(kck_new, kcv_new, w_cmp, n_batch, t_len):
    w_hi = w_cmp[:, :t_len, :, LANES:].reshape(2, t_len * LANES, LANES)
    ak = kck_new.reshape(n_batch, t_len * LANES)
    av = kcv_new.reshape(n_batch, t_len * LANES)
    return pl.pallas_call(
        _hi_new_kernel,
        out_shape=jax.ShapeDtypeStruct((n_batch, 2, LANES), F32),
        name="compress_new_tokens",
    )(ak, av, w_hi)


def _compress_sample(page_table_flat, pool, hi_new, w_cmp, b_cmp, n_batch, n_pages, page_rows):
    assert page_rows == PAGE
    perm = _regroup_perm()
    w_cmp = w_cmp.reshape(2, CMP_STRIDE // 2, 2 * LANES, 2 * LANES)
    n_half = n_pages * page_rows // CMP_STRIDE
    grid_spec = pltpu.PrefetchScalarGridSpec(
        num_scalar_prefetch=1,
        grid=(n_batch,),
        in_specs=[pl.BlockSpec(memory_space=pl.ANY),
                  pl.BlockSpec((1, 2, LANES), lambda b, pt: (b, 0, 0)),
                  pl.BlockSpec(w_cmp.shape, lambda b, pt: (0, 0, 0, 0)),
                  pl.BlockSpec(b_cmp.shape, lambda b, pt: (0, 0)),
                  pl.BlockSpec(perm.shape, lambda b, pt: (0, 0))],
        out_specs=[pl.BlockSpec((1, n_half, LANES), lambda b, pt: (b, 0, 0)),
                   pl.BlockSpec((1, n_half, LANES), lambda b, pt: (b, 0, 0))],
        scratch_shapes=[pltpu.VMEM((2, n_pages * 2 * LANES, page_rows), F32), pltpu.SemaphoreType.DMA((2,)),
                        pltpu.VMEM((2, CMP_STRIDE // 2, n_half, 2 * LANES), F32)],
    )
    return pl.pallas_call(
        functools.partial(_compress_sample_kernel, n_pages=n_pages, page_rows=page_rows),
        grid_spec=grid_spec,
        out_shape=[jax.ShapeDtypeStruct((n_batch, n_half, LANES), BF16)] * 2,
        compiler_params=_cparams("arbitrary"),
        name="compress_sample",
    )(page_table_flat, pool, hi_new, w_cmp, b_cmp, perm)


N_SEL_PAD = 256


def _sample_tables(rel_bias, past, t_len):
    n_cmp = past // CMP_STRIDE
    n_sel = past // SEL_BLOCK + 1
    y = np.tile(np.arange(t_len), N_HEADS)[:, None]
    hh = np.repeat(np.arange(N_HEADS), t_len)[:, None]

    def tab(dist, valid):
        return _head_table(rel_bias, dist, valid, hh)

    dc = past + y - (np.arange(n_cmp)[None, :] * CMP_STRIDE + CMP_LEN - 1)
    biasc = tab(dc, dc >= 0)
    dp = past + y - np.arange(past)[None, :]
    biasp = tab(dp, dp >= 0)
    dn = y - np.arange(t_len)[None, :]
    biasn = tab(dn, dn >= 0)
    dw = WINDOW + y - np.arange(WINDOW)[None, :]
    biasw = tab(dw, dw < WINDOW)
    cs = np.arange(n_cmp)[:, None] * CMP_STRIDE
    ss = np.arange(N_SEL_PAD)[None, :] * SEL_BLOCK
    ov = ((cs < ss + SEL_BLOCK) & (cs + CMP_LEN > ss) & (np.arange(N_SEL_PAD)[None, :] < n_sel))
    ov4 = jnp.asarray(np.tile(ov.astype(np.float32), (GQA_R, 1)), BF16)
    expand = (np.arange(past)[None, :] // SEL_BLOCK == np.arange(past // SEL_BLOCK)[:, None])
    return biasc, biasp, biasn, biasw, ov4, jnp.asarray(expand.astype(np.float32), BF16)


def _sample_attn_kernel(pt_ref, q_ref, gt_ref, ck_ref, cv_ref, pool_ref, win_ref, ksk_ref, ksv_ref, kwk_ref,
                        kwv_ref, biasc_ref, biasp_ref, biasn_ref, biasw_ref, ov_ref, exp_ref, o_ref,
                        buf_ref, sem_ref, s_ref, *, n_pages, page_tokens, t_len, cur_block):
    b = pl.program_id(0)
    nb = pl.num_programs(0)
    slot = lax.rem(b, 2)
    rows = 2 * LANES
    nt = (((1,), (1,)), ((), ()))
    n_rows = N_HEADS * t_len

    @pl.when(b == 0)
    def _():
        _fetch_pages(pool_ref, pt_ref, buf_ref, sem_ref, b, slot, n_pages, rows)

    @pl.when(b + 1 < nb)
    def _():
        _fetch_pages(pool_ref, pt_ref, buf_ref, sem_ref, b + 1, 1 - slot, n_pages, rows)

    qb = q_ref[0]
    bc = biasc_ref[...]
    sc = lax.dot_general(qb, ck_ref[0], nt, preferred_element_type=F32) + bc
    e = jnp.exp(sc - jnp.max(sc, axis=1, keepdims=True))
    e = jnp.where(bc > 0.5 * NEG, e, 0.0)
    lsum = jnp.sum(e, axis=1, keepdims=True)
    pb = (e / jnp.where(lsum > 0.0, lsum, 1.0)).astype(BF16)
    o_c = jnp.dot(pb, cv_ref[0], preferred_element_type=F32)
    p4 = jnp.concatenate(
        [jnp.concatenate([pb[(g * GQA_R + r) * t_len:(g * GQA_R + r + 1) * t_len, :] for r in range(GQA_R)], axis=1)
         for g in range(N_KV)], axis=0)
    imp = jnp.dot(p4, ov_ref[...], preferred_element_type=F32)
    lane = lax.broadcasted_iota(jnp.int32, imp.shape, 1)
    imp = jnp.where((lane == 0) | (lane == cur_block) | (lane == cur_block - 1), FORCE, imp)
    imp = jnp.where(lane > cur_block, NEG, imp)

    def rank_body(k, cnt):
        other = pltpu.roll(imp, k, 1)
        beats = (other > imp) | ((other == imp) & (lane >= k))
        return cnt + jnp.where(beats, 1.0, 0.0)

    cnt = lax.fori_loop(1, N_SEL_PAD, rank_body, jnp.zeros(imp.shape, F32), unroll=15)
    sel = jnp.where((cnt < N_SELECT) & (lane <= cur_block), 1.0, 0.0)
    sel_rows = jnp.concatenate([sel[g * t_len:(g + 1) * t_len, :] for g in range(N_KV) for _ in range(GQA_R)], axis=0)
    n_past_blocks = n_pages * page_tokens // SEL_BLOCK
    keep = jnp.dot(sel_rows[:, :n_past_blocks].astype(BF16), exp_ref[...], preferred_element_type=F32)

    def softmax_pair(s_old, s_new):
        m = jnp.maximum(jnp.max(s_old, axis=1, keepdims=True), jnp.max(s_new, axis=1, keepdims=True))
        p_old = jnp.exp(s_old - m)
        p_new = jnp.exp(s_new - m)
        inv = 1.0 / (jnp.sum(p_old, axis=1, keepdims=True) + jnp.sum(p_new, axis=1, keepdims=True))
        return p_old, p_new, inv

    _wait_pages(pool_ref, pt_ref, buf_ref, sem_ref, b, slot, n_pages, rows)
    for p in range(n_pages):
        kt_page = buf_ref[slot, p * rows:p * rows + LANES, :].astype(BF16)
        s_ref[:, p * page_tokens:(p + 1) * page_tokens] = jnp.dot(qb, kt_page, preferred_element_type=F32)
    s_past = jnp.where(keep > 0.5, s_ref[...] + biasp_ref[...], NEG)
    bn = biasn_ref[...]
    s_new = lax.dot_general(qb, ksk_ref[...].astype(BF16), nt, preferred_element_type=F32) + bn
    p_past, p_new, inv = softmax_pair(s_past, s_new)
    p_past = p_past.astype(BF16)
    o_s = jnp.dot(p_new.astype(BF16), ksv_ref[...].astype(BF16), preferred_element_type=F32)
    for p in range(n_pages):
        vt_page = buf_ref[slot, p * rows + LANES:(p + 1) * rows, :].astype(BF16)
        o_s = o_s + lax.dot_general(p_past[:, p * page_tokens:(p + 1) * page_tokens], vt_page, nt,
                                    preferred_element_type=F32)
    o_s = o_s * inv
    s_win = jnp.dot(qb, win_ref[0, :LANES, :].astype(BF16), preferred_element_type=F32) + biasw_ref[...]
    s_wnew = lax.dot_general(qb, kwk_ref[...].astype(BF16), nt, preferred_element_type=F32) + bn
    p_win, p_wnew, inv_w = softmax_pair(s_win, s_wnew)
    o_w = (lax.dot_general(p_win.astype(BF16), win_ref[0, LANES:, :].astype(BF16), nt, preferred_element_type=F32)
           + jnp.dot(p_wnew.astype(BF16), kwv_ref[...].astype(BF16), preferred_element_type=F32)) * inv_w
    gs = jax.nn.sigmoid(gt_ref[0])
    o = gs[:, 0:1] * o_c + gs[:, 1:2] * o_s + gs[:, 2:3] * o_w
    row = lax.broadcasted_iota(jnp.int32, (n_rows, HEAD_DIM), 0)
    o_ref[0] = jnp.where(row < GQA_R * t_len, o[:, :HEAD_DIM], o[:, HEAD_DIM:])


def _sample_attn(page_table_flat, q, gt, ck, cv, pool, win, ksk, ksv, kwk, kwv, tables, n_batch, n_pages, page_tokens,
                 t_len):
    biasc, biasp, biasn, biasw, ov4, expand = tables
    past = n_pages * page_tokens
    n_rows = N_HEADS * t_len
    cur_block = past // SEL_BLOCK
    assert past % SEL_BLOCK == 0 and t_len <= SEL_BLOCK and cur_block < N_SEL_PAD
    new_spec = pl.BlockSpec((t_len, LANES), lambda b, pt: (b, 0))
    cst = lambda a: pl.BlockSpec(a.shape, lambda b, pt: (0,) * a.ndim)
    grid_spec = pltpu.PrefetchScalarGridSpec(
        num_scalar_prefetch=1,
        grid=(n_batch,),
        in_specs=[pl.BlockSpec((1, n_rows, LANES), lambda b, pt: (b, 0, 0)),
                  pl.BlockSpec((1, n_rows, 3), lambda b, pt: (b, 0, 0)),
                  pl.BlockSpec((1, past // CMP_STRIDE, LANES), lambda b, pt: (b, 0, 0)),
                  pl.BlockSpec((1, past // CMP_STRIDE, LANES), lambda b, pt: (b, 0, 0)),
                  pl.BlockSpec(memory_space=pl.ANY),
                  pl.BlockSpec((1, 2 * LANES, WINDOW), lambda b, pt: (b, 0, 0)),
                  new_spec, new_spec, new_spec, new_spec,
                  cst(biasc), cst(biasp), cst(biasn), cst(biasw), cst(ov4), cst(expand)],
        out_specs=pl.BlockSpec((1, n_rows, HEAD_DIM), lambda b, pt: (b, 0, 0)),
        scratch_shapes=[pltpu.VMEM((2, n_pages * 2 * LANES, page_tokens), F32), pltpu.SemaphoreType.DMA((2,)),
                        pltpu.VMEM((n_rows, past), F32)],
    )
    return pl.pallas_call(
        functools.partial(_sample_attn_kernel, n_pages=n_pages, page_tokens=page_tokens, t_len=t_len,
                          cur_block=cur_block),
        grid_spec=grid_spec,
        out_shape=jax.ShapeDtypeStruct((n_batch, n_rows, HEAD_DIM), F32),
        compiler_params=_cparams("arbitrary"),
        name="sample_attn",
    )(page_table_flat, q, gt, ck, cv, pool, win, ksk, ksv, kwk, kwv, biasc, biasp, biasn, biasw, ov4, expand)


def kernel(x_prompt, x_sample, mem_prompt, cache_cmp_kv, cache_slc_kv, cache_win_kv, cache_mem_kv, page_table, norm_mix, w_in, sgu_norm_g, sgu_norm_b, sgu_w, sgu_b, cmp_w, cmp_b, w_o, rel_bias, norm_x, norm_mem, w_xq, w_mem_kv, w_xo, norm_ffn, w_router, b_router, w_gu, b_gu, w_down, b_down, final_norm):
    assert norm_mix.shape[0] == 1, "single-layer trunk"
    bsz, t_p, _ = x_prompt.shape
    dbs, t_s, _ = x_sample.shape
    n_phys, page_tokens = cache_cmp_kv.shape[1], cache_cmp_kv.shape[2]
    n_pages = page_table.shape[1]
    past = n_pages * page_tokens
    wb = cache_win_kv.shape[2]
    assert wb == WINDOW and t_p >= WINDOW and t_p >= CHUNK and t_s <= CHUNK
    n_p, n_s = bsz * t_p, dbs * t_s
    row1 = lambda v: v.reshape(1, -1)

    w_all = _build_w_in(w_in[0])
    mix_p, mixb_p = _build_sgu_mix(sgu_w[0], sgu_b[0], CHUNK, CHUNK)
    mix_s, mixb_s = _build_sgu_mix(sgu_w[0], sgu_b[0], t_s, ROW_TILE)
    w_cmp = _build_cmp_w(cmp_w[0])
    b_cmp = cmp_b[0].reshape(2, LANES)
    w_oa, w_ob = w_o[0, :SGU_W].astype(BF16), w_o[0, SGU_W:].astype(BF16)
    w_xq_b, w_xo_b = w_xq[0].astype(BF16), w_xo[0].astype(BF16)
    w_rt = w_router[0].T.astype(BF16)
    b_rt = b_router[0].reshape(N_EXPERTS, 1)
    in_args = (row1(norm_mix[0]), w_all, row1(sgu_norm_g[0]), row1(sgu_norm_b[0]))
    tail_args = (w_xo_b, row1(norm_ffn[0]), w_rt, b_rt)

    xp = x_prompt.reshape(n_p, D_MODEL)
    (a_p, v_p, q_p, gt_p, kct, kst, kwt, kskb, ksvt, kwkb, kwvt) = _inproj(
        xp, *in_args, mix_p, mixb_p, t_p, True)
    ck_p, cvt_p = _compress_prompt(kct, w_cmp, b_cmp, bsz, t_p)
    b_p = _prompt_attn(q_p.reshape(bsz, t_p, N_HEADS * LANES), gt_p.reshape(bsz, t_p, LANES), ck_p, cvt_p,
                       kskb, ksvt, kwkb, kwvt, _attn_tables(rel_bias, t_p), bsz, t_p)
    x1_p, qm_p = _oproj(xp, a_p, b_p.reshape(n_p, NSA_W), w_oa, w_ob, row1(norm_x[0]), w_xq_b)
    memkv_f, memkv_b = _memkv(mem_prompt.reshape(bsz * MEM_LEN, D_MODEL), row1(norm_mem[0]), w_mem_kv[0].astype(BF16))
    x2_p, h_p, te_p, tg_p = _xattn(x1_p, qm_p, memkv_b, *tail_args, t_p, True)

    xs = x_sample.reshape(n_s, D_MODEL)
    (a_s, v_s, q_s, gt_s, kck_s, kcv_s, ksk_s, ksv_s, kwk_s, kwv_s) = _inproj(
        xs, *in_args, mix_s, mixb_s, t_s, False)
    pt_flat = page_table.reshape(-1)
    stored = lambda c: c[0].transpose(0, 2, 3, 4, 1).reshape(c.shape[1], 2 * LANES, c.shape[2])
    pool_c, pool_s = stored(cache_cmp_kv), stored(cache_slc_kv)
    win_t = cache_win_kv[0].transpose(0, 2, 3, 4, 1).reshape(dbs, 2 * LANES, wb)
    hi_new = _hi_new(kck_s, kcv_s, w_cmp, dbs, t_s)
    ck_s, cv_s = _compress_sample(pt_flat, pool_c, hi_new, w_cmp, b_cmp, dbs, n_pages, page_tokens)
    q_sb = q_s.reshape(dbs, t_s, N_HEADS, LANES).transpose(0, 2, 1, 3).reshape(dbs, N_HEADS * t_s, LANES)
    gt_sb = gt_s[:, :3 * N_HEADS].reshape(dbs, t_s, 3, N_HEADS).transpose(0, 3, 1, 2).reshape(dbs, N_HEADS * t_s, 3)
    bo_s = _sample_attn(pt_flat, q_sb, gt_sb, ck_s, cv_s, pool_s, win_t, ksk_s, ksv_s, kwk_s, kwv_s,
                        _sample_tables(rel_bias, past, t_s), dbs, n_pages, page_tokens, t_s)
    b_s = bo_s.reshape(dbs, N_HEADS, t_s, HEAD_DIM).transpose(0, 2, 1, 3).reshape(n_s, NSA_W).astype(BF16)
    x1_s, qm_s = _oproj(xs, a_s, b_s, w_oa, w_ob, row1(norm_x[0]), w_xq_b)
    memkv_s = cache_mem_kv[0].reshape(dbs, MEM_LEN * 2 * MEM_HEADS, MEM_HD)
    x2_s, h_s, te_s, tg_s = _xattn(x1_s, qm_s, memkv_s, *tail_args, t_s, False)

    row_tok, pos, items = _moe_dispatch(jnp.concatenate([te_p, te_s], axis=1))
    xs_sorted = jnp.concatenate([h_p, h_s], axis=0)[row_tok]
    y = _experts(items, xs_sorted, w_gu[0], b_gu[0][:, None, :], w_down[0], b_down[0][:, None, :])
    fn = row1(final_norm)
    y_prompt = _final(x2_p, y[pos[:, :n_p]], tg_p.T, fn).reshape(bsz, t_p, D_MODEL)
    y_sample = _final(x2_s, y[pos[:, n_p:]], tg_s.T, fn).reshape(dbs, t_s, D_MODEL)

    kv6 = lambda k, v, b, t: jnp.concatenate([k, v], axis=1).reshape(1, b, t, 2, N_KV, HEAD_DIM)
    unstored = lambda a: a.reshape(1, a.shape[0], 2, N_KV, HEAD_DIM, a.shape[2]).transpose(0, 1, 5, 2, 3, 4)
    kw_s = kv6(kwk_s, kwv_s, dbs, t_s)
    win_s = jnp.concatenate([cache_win_kv, kw_s], axis=2)[:, :, -wb:]
    return (y_prompt, y_sample, unstored(kct), unstored(kst), unstored(kwt[:, :, t_p - WINDOW:]),
            memkv_f.reshape(1, bsz, MEM_LEN, 2, MEM_HEADS, MEM_HD), v_p.reshape(1, bsz, t_p, SGU_W)[:, :, -CHUNK:],
            kv6(kck_s, kcv_s, dbs, t_s), kv6(ksk_s, ksv_s, dbs, t_s), win_s, v_s.reshape(1, dbs, t_s, SGU_W))
```

```python
import functools
import math

import numpy as np
import jax
import jax.numpy as jnp
from jax import lax
from jax.experimental import pallas as pl
from jax.experimental.pallas import tpu as pltpu

D_MODEL = 1024
SGU_W = 512
SGU_GROUPS = 4
SGU_GW = 128
CHUNK = 128
NSA_W = 512
HEAD_DIM = 64
N_HEADS = 8
N_KV = 2
GQA_R = 4
CMP_LEN = 32
CMP_STRIDE = 16
SEL_BLOCK = 64
N_SELECT = 16
WINDOW = 512
KV_W = 256
N_BUCKETS = 32
MAX_EXACT = 16
MAX_DIST = 128
MEM_LEN = 256
MEM_HEADS = 4
MEM_HD = 128
N_EXPERTS = 32
TOP_K = 4
D_FF = 1024
SWIGLU_LIMIT = 7.0
SWIGLU_ALPHA = 1.702
EPS = 1e-6
NEG = -1e30
FORCE = 1e9

LANES = 128
VMEM_LIMIT = 56 * 1024 * 1024

ROW_TILE = 512
QT = 256
KT = 256
CHAIN_HEADS = 4
VT_ROWS = LANES + 16
LOG2E = math.log2(math.e)

F32 = jnp.float32
BF16 = jnp.bfloat16


def _cparams(*sem):
    return pltpu.CompilerParams(dimension_semantics=sem, vmem_limit_bytes=VMEM_LIMIT)


def _const_spec(shape):
    nd = len(shape)
    return pl.BlockSpec(shape, lambda *_: (0,) * nd)


def _rms(x, g):
    return x * lax.rsqrt(jnp.mean(x * x, axis=-1, keepdims=True) + EPS) * g


def _rel_bucket_np(dist):
    n = np.maximum(dist, 0)
    nf = np.maximum(n, 1).astype(np.float32)
    large = MAX_EXACT + (np.log(nf / np.float32(MAX_EXACT)) / np.float32(math.log(MAX_DIST / MAX_EXACT))
                         * np.float32(N_BUCKETS - MAX_EXACT)).astype(np.int32)
    large = np.minimum(large, N_BUCKETS - 1)
    return np.where(n < MAX_EXACT, n, large).astype(np.int32)


C_U, C_V, C_Q, C_KC, C_GT, C_END = 0, 512, 1024, 2048, 2816, 2944


def _inproj_kernel(x_ref, g_ref, w_ref, lng_ref, lnb_ref, mix_ref, mixb_ref, *out_refs, mix_block, attn_extras):
    a_ref, v_ref, q_ref, gt_ref = out_refs[:4]
    x = x_ref[...]
    h = _rms(x, g_ref[...]).astype(BF16)

    def proj(lo, hi):
        return jnp.dot(h, w_ref[:, lo:hi], preferred_element_type=F32)

    q_ref[...] = proj(C_Q, C_KC).astype(BF16)
    gt_ref[...] = proj(C_GT, C_END)
    kv = [proj(C_KC + LANES * j, C_KC + LANES * (j + 1)) for j in range(6)]
    if attn_extras:
        kvt_refs = out_refs[4:7]
        kskb_ref, ksvt_ref, kwkb_ref, kwvt_ref = out_refs[7:11]
        kskb_ref[...] = kv[2].astype(BF16)
        kwkb_ref[...] = kv[4].astype(BF16)
        for j, (kvt_ref, vtile_ref) in enumerate(zip(kvt_refs, (None, ksvt_ref, kwvt_ref))):
            kvt_ref[0, :LANES, :] = kv[2 * j].T
            vt = kv[2 * j + 1].T
            kvt_ref[0, LANES:, :] = vt
            if vtile_ref is not None:
                vtb = jnp.concatenate([vt.astype(BF16), jnp.ones((VT_ROWS - LANES, vt.shape[1]), BF16)], axis=0)
                for t in range(vt.shape[1] // KT):
                    vtile_ref[t] = vtb[:, t * KT:(t + 1) * KT]
    else:
        for r, val in zip(out_refs[4:10], kv):
            r[...] = val

    u = jax.nn.gelu(proj(C_U, C_V))
    v = jax.nn.gelu(proj(C_V, C_Q))
    mu = jnp.mean(v, axis=-1, keepdims=True)
    var = jnp.mean(jnp.square(v - mu), axis=-1, keepdims=True)
    v = (v - mu) * lax.rsqrt(var + EPS) * lng_ref[...] + lnb_ref[...]
    v_ref[...] = v
    vb = v.astype(BF16)
    rows = x.shape[0]
    for blk in range(rows // mix_block):
        r0 = blk * mix_block
        for g in range(SGU_GROUPS):
            c0 = g * SGU_GW
            mixed = jnp.dot(mix_ref[g], vb[r0:r0 + mix_block, c0:c0 + SGU_GW],
                            preferred_element_type=F32) + mixb_ref[:, c0:c0 + SGU_GW]
            a_ref[r0:r0 + mix_block, c0:c0 + SGU_GW] = (
                u[r0:r0 + mix_block, c0:c0 + SGU_GW] * mixed).astype(BF16)


def _inproj(x2d, g, w_all, ln_g, ln_b, mix, mixb, t_len, attn_extras):
    n = x2d.shape[0]
    mb = mix.shape[1]
    row = lambda c: pl.BlockSpec((ROW_TILE, c), lambda i: (i, 0))
    outs = [(SGU_W, BF16), (SGU_W, F32), (N_HEADS * LANES, BF16), (LANES, F32)]
    if not attn_extras:
        outs += [(LANES, F32)] * 6
    out_specs = [row(c) for c, _ in outs]
    out_shape = [jax.ShapeDtypeStruct((n, c), dt) for c, dt in outs]
    if attn_extras:
        per_b = t_len // ROW_TILE
        kvt_spec = pl.BlockSpec((1, 2 * LANES, ROW_TILE), lambda i: (i // per_b, 0, i % per_b))
        kvt_shape = jax.ShapeDtypeStruct((n // t_len, 2 * LANES, t_len), F32)
        tiles = ROW_TILE // KT
        vt_spec = pl.BlockSpec((tiles, VT_ROWS, KT), lambda i: (i, 0, 0))
        vt_shape = jax.ShapeDtypeStruct((n // KT, VT_ROWS, KT), BF16)
        out_specs += [kvt_spec] * 3 + [row(LANES), vt_spec, row(LANES), vt_spec]
        out_shape += [kvt_shape] * 3 + [jax.ShapeDtypeStruct((n, LANES), BF16), vt_shape,
                                        jax.ShapeDtypeStruct((n, LANES), BF16), vt_shape]
    return pl.pallas_call(
        functools.partial(_inproj_kernel, mix_block=mb, attn_extras=attn_extras),
        grid=(n // ROW_TILE,),
        in_specs=[row(D_MODEL), _const_spec((1, D_MODEL)), _const_spec(w_all.shape),
                  _const_spec((1, SGU_W)), _const_spec((1, SGU_W)), _const_spec(mix.shape),
                  _const_spec(mixb.shape)],
        out_specs=out_specs,
        out_shape=out_shape,
        compiler_params=_cparams("parallel"),
        name="inproj_sgu",
    )(x2d, g, w_all, ln_g, ln_b, mix, mixb)


def _build_w_in(w_in):
    u = w_in[:, 0:512]
    v = w_in[:, 512:1024]
    q = w_in[:, 1024:1536].reshape(D_MODEL, N_KV, GQA_R, HEAD_DIM) * (HEAD_DIM ** -0.5 * LOG2E)
    qp = jnp.zeros((D_MODEL, N_KV, GQA_R, N_KV, HEAD_DIM), F32)
    for g in range(N_KV):
        qp = qp.at[:, g, :, g, :].set(q[:, g])
    qp = qp.reshape(D_MODEL, N_HEADS * LANES)
    kv = w_in[:, 1536:2304]
    gt = jnp.pad(w_in[:, 2304:2328], ((0, 0), (0, LANES - 3 * N_HEADS)))
    return jnp.concatenate([u, v, qp, kv, gt], axis=1).astype(BF16)


def _build_sgu_mix(sgu_w, sgu_b, chunk, mix_block):
    tri = jnp.tril(jnp.ones((chunk, chunk), bool))
    w = jnp.where(tri, sgu_w[:, :chunk, :chunk], 0.0)
    reps = mix_block // chunk
    eye = jnp.eye(reps, dtype=F32)
    mix = jnp.einsum('ab,gst->gasbt', eye, w).reshape(SGU_GROUPS, mix_block, mix_block)
    b = jnp.tile(sgu_b[:, :chunk], (1, reps))
    mixb = jnp.repeat(b.T, SGU_GW, axis=1)
    return mix.astype(BF16), mixb


def _build_cmp_w(cmp_w):
    w = cmp_w.reshape(2, N_KV, 2, CMP_STRIDE, HEAD_DIM, HEAD_DIM)
    out = jnp.zeros((2, CMP_STRIDE, N_KV, HEAD_DIM, 2, N_KV, HEAD_DIM), F32)
    for g in range(N_KV):
        out = out.at[:, :, g, :, :, g, :].set(jnp.transpose(w[:, g], (0, 2, 3, 1, 4)))
    return out.reshape(2, CMP_STRIDE, LANES, 2 * LANES).astype(BF16)


PAGE = 128
PER_PAGE = PAGE // CMP_STRIDE


def _regroup_perm():
    tok = np.arange(PAGE)
    return jnp.asarray(np.arange(PAGE)[:, None] == ((tok % CMP_STRIDE) * PER_PAGE + tok // CMP_STRIDE)[None, :], BF16)


def _regroup_page(page, perm_ref, tok_ref, block0):
    t = lax.dot_general(perm_ref[...], page.astype(BF16), (((1,), (1,)), ((), ())), preferred_element_type=F32)
    for r in range(CMP_STRIDE):
        for c in range(2):
            tok_ref[c, r // 2, pl.ds(block0, PER_PAGE), (r % 2) * LANES:(r % 2 + 1) * LANES] = (
                t[r * PER_PAGE:(r + 1) * PER_PAGE, c * LANES:(c + 1) * LANES])


def _compress_halves(tok_ref, w_ref, c):
    acc = jnp.dot(tok_ref[c, 0].astype(BF16), w_ref[c, 0], preferred_element_type=F32)
    for q in range(1, CMP_STRIDE // 2):
        acc = acc + jnp.dot(tok_ref[c, q].astype(BF16), w_ref[c, q], preferred_element_type=F32)
    return acc


def _compress_kernel(kct_ref, w_ref, b_ref, perm_ref, ck_ref, cvt_ref, tok_ref, *, n_half):
    for p in range(n_half // PER_PAGE):
        _regroup_page(kct_ref[0, :, p * PAGE:(p + 1) * PAGE], perm_ref, tok_ref, p * PER_PAGE)
    for c in range(2):
        acc = _compress_halves(tok_ref, w_ref, c)
        hi_next = pltpu.roll(acc[:, LANES:], n_half - 1, 0)
        comp = acc[:, :LANES] + hi_next + b_ref[c:c + 1, :]
        if c == 0:
            ck_ref[0] = comp.astype(BF16)
        else:
            cvt_ref[0] = comp.T.astype(BF16)


def _compress_prompt(kct, w_cmp, b_cmp, bsz, t_len):
    n_half = t_len // CMP_STRIDE
    perm = _regroup_perm()
    w2 = w_cmp.reshape(2, CMP_STRIDE // 2, 2 * LANES, 2 * LANES)
    return pl.pallas_call(
        functools.partial(_compress_kernel, n_half=n_half),
        grid=(bsz,),
        in_specs=[pl.BlockSpec((1, 2 * LANES, t_len), lambda b: (b, 0, 0)),
                  _const_spec(w2.shape), _const_spec(b_cmp.shape), _const_spec(perm.shape)],
        out_specs=[pl.BlockSpec((1, n_half, LANES), lambda b: (b, 0, 0)),
                   pl.BlockSpec((1, LANES, n_half), lambda b: (b, 0, 0))],
        out_shape=[jax.ShapeDtypeStruct((bsz, n_half, LANES), BF16),
                   jax.ShapeDtypeStruct((bsz, LANES, n_half), BF16)],
        scratch_shapes=[pltpu.VMEM((2, CMP_STRIDE // 2, n_half, 2 * LANES), F32)],
        compiler_params=_cparams("parallel"),
        name="compress_prompt",
    )(kct, w2, b_cmp, perm)


def _toeplitz(rel_bias, rows, cols, off, valid_fn):
    span = rows + cols - 1
    d = np.arange(span) + off - (rows - 1)
    vec = jnp.where(valid_fn(d)[None, :], rel_bias[_rel_bucket_np(d)].T, NEG)
    flat = jnp.tile(vec, (1, rows + 1))[:, :rows * (span + 1)]
    skew = flat.reshape(N_HEADS, rows, span + 1)
    return skew[:, ::-1, :cols]


def _head_table(rel_bias, dist, valid, head):
    bucket = _rel_bucket_np(dist)
    need = valid & (bucket != N_BUCKETS - 1)
    out = jnp.broadcast_to(rel_bias[N_BUCKETS - 1][head[:, 0]][:, None], dist.shape)
    cols = np.nonzero(need.any(axis=0))[0]
    if cols.size:
        c0, c1 = int(cols[0]), int(cols[-1]) + 1
        mid = jnp.where(need[:, c0:c1], rel_bias[bucket[:, c0:c1], head], out[:, c0:c1])
        out = jnp.concatenate([out[:, :c0], mid, out[:, c1:]], axis=1)
    return jnp.where(valid, out, NEG)


def _attn_tables(rel_bias, t_len):
    rel_bias = rel_bias * LOG2E
    n_half = t_len // CMP_STRIDE
    n_sel = t_len // SEL_BLOCK
    nq = t_len // QT
    c_far = rel_bias[N_BUCKETS - 1]
    to_lanes = lambda a: a.reshape(N_KV, GQA_R, a.shape[1], QT).transpose(0, 2, 1, 3).reshape(
        N_KV, a.shape[1], GQA_R * QT)
    near = jnp.stack([
        to_lanes(_toeplitz(rel_bias, KT, QT, 0, lambda d: d >= 0)),
        to_lanes(_toeplitz(rel_bias, KT, QT, KT, lambda d: d >= 0)),
        to_lanes(_toeplitz(rel_bias, KT, QT, 2 * KT, lambda d: d < WINDOW))])
    far = jnp.repeat(c_far.reshape(N_KV, 1, GQA_R), QT, axis=2).reshape(N_KV, 1, GQA_R * QT)
    per_q = QT // CMP_STRIDE
    band_rows = 2 * per_q
    lead = per_q // 2
    off = lead * CMP_STRIDE - (CMP_LEN - 1)
    band = _toeplitz(rel_bias, band_rows * CMP_STRIDE, QT, off, lambda d: d >= 0)[:, ::CMP_STRIDE]
    assert _rel_bucket_np(np.array([off + CMP_STRIDE])).item() == N_BUCKETS - 1
    tiles = []
    for i in range(nq):
        lo = i * per_q - lead
        parts = []
        if lo > 0:
            parts.append(jnp.broadcast_to(c_far[:, None, None], (N_HEADS, lo, QT)))
        b0, b1 = max(0, -lo), min(band_rows, n_half - lo)
        parts.append(band[:, b0:b1])
        rest = n_half - (lo + b1)
        if rest > 0:
            parts.append(jnp.full((N_HEADS, rest, QT), NEG, F32))
        tiles.append(to_lanes(jnp.concatenate(parts, axis=1)))
    biasc = jnp.stack(tiles)
    cs = np.arange(n_half - 1)[None, :] * CMP_STRIDE
    ss = np.arange(n_sel)[:, None] * SEL_BLOCK
    ov = np.zeros((n_sel, n_half), np.float32)
    ov[:, :n_half - 1] = (cs < ss + SEL_BLOCK) & (cs + CMP_LEN > ss)
    ovt4 = jnp.asarray(np.tile(ov, (1, GQA_R)), BF16)
    return near, far, biasc, ovt4


def _prompt_attn_kernel(q_ref, gt_ref, ck_ref, cvt_ref, ksk_ref, ksvt_ref, kwk_ref, kwvt_ref,
                        biasc_ref, near_ref, far_ref, ovt_ref, o_ref,
                        imp_ref, selneg_ref, m_ref, acc_ref, oc_ref, os_ref, outt_ref, *, n_sel):
    i = pl.program_id(1)
    nt = (((1,), (1,)), ((), ()))
    groups = range(N_KV)
    own = lambda g: slice(g * HEAD_DIM, (g + 1) * HEAD_DIM)
    jj = lax.broadcasted_iota(jnp.int32, (n_sel, QT), 0)
    tt = i * QT + lax.broadcasted_iota(jnp.int32, (n_sel, QT), 1)
    cur = lax.shift_right_logical(tt, 6)
    forced = (jj == 0) | (jj == cur) | (jj == cur - 1)
    future = jj > cur
    qs = [jnp.concatenate([q_ref[0, :, (g * GQA_R + r) * LANES:(g * GQA_R + r + 1) * LANES]
                           for r in range(GQA_R)], axis=0) for g in groups]

    near = lambda idx: (lambda g, cols: near_ref[idx, g, :, cols])
    far = lambda g, cols: far_ref[g, :, cols]

    def online_step(k_ref, vt_ref, kt, bias_of, mask_of, shift_of, first):
        kk = k_ref[pl.ds(pl.multiple_of(kt * KT, KT), KT), :]
        vt = vt_ref[kt]
        for g in groups:
            mask = None if mask_of is None else jnp.concatenate([mask_of(g)] * CHAIN_HEADS, axis=1)
            for c0 in range(0, GQA_R, CHAIN_HEADS):
                cols = slice(c0 * QT, (c0 + CHAIN_HEADS) * QT)
                s = lax.dot_general(kk, qs[g][cols], nt, preferred_element_type=F32)
                if bias_of is not None:
                    s = s + bias_of(g, cols)
                if mask is not None:
                    s = s + mask
                mt = jnp.max(s, axis=0, keepdims=True)
                shift = None if shift_of is None else shift_of(g, cols)
                if shift is not None:
                    mt = mt + shift
                if first:
                    m_new = mt
                else:
                    m_old = m_ref[g, :, cols]
                    m_new = jnp.maximum(m_old, mt)
                    alpha = jnp.exp2(m_old - m_new)
                p = jnp.exp2(s - (m_new if shift is None else m_new - shift))
                pv = jnp.dot(vt, p.astype(BF16), preferred_element_type=F32)
                if first:
                    acc_ref[g, :, cols] = pv
                else:
                    acc_ref[g, :, cols] = alpha * acc_ref[g, :, cols] + pv
                m_ref[g, :, cols] = m_new

    def normalized(g):
        return acc_ref[g, own(g), :] / acc_ref[g, LANES:LANES + 1, :]

    def sel_mask(g, kt):
        rows = [jnp.broadcast_to(selneg_ref[g, pl.ds(kt * (KT // SEL_BLOCK) + b, 1), :], (SEL_BLOCK, QT))
                for b in range(KT // SEL_BLOCK)]
        return jnp.concatenate(rows, axis=0)

    for g in groups:
        bc = biasc_ref[0, g]
        sc = lax.dot_general(ck_ref[0], qs[g], nt, preferred_element_type=F32) + bc
        e = jnp.exp2(sc - jnp.max(sc, axis=0, keepdims=True))
        e = jnp.where(bc > 0.5 * NEG, e, 0.0)
        lsum = jnp.sum(e, axis=0, keepdims=True)
        pb = (e / jnp.where(lsum > 0.0, lsum, 1.0)).astype(BF16)
        oc_ref[g] = jnp.dot(cvt_ref[0], pb, preferred_element_type=F32)[own(g)]
        p4 = jnp.concatenate([pb[:, r * QT:(r + 1) * QT] for r in range(GQA_R)], axis=0)
        imp = jnp.dot(ovt_ref[...], p4, preferred_element_type=F32)
        imp = jnp.where(forced, FORCE, imp)
        imp = jnp.where(future, NEG, imp)
        imp_ref[...] = imp

        def rank_body(ii, cnt, imp=imp):
            row = imp_ref[pl.ds(ii, 1), :]
            beats = (row > imp) | ((row == imp) & (jj > ii))
            return cnt + jnp.where(beats, 1.0, 0.0)

        cnt = lax.fori_loop(0, n_sel, rank_body, jnp.zeros((n_sel, QT), F32), unroll=8)
        selneg_ref[g] = jnp.where((cnt < N_SELECT) & jnp.logical_not(future), 0.0, NEG)

    online_step(ksk_ref, ksvt_ref, i, near(0), lambda g: sel_mask(g, i), None, True)

    @pl.when(i >= 1)
    def _():
        online_step(ksk_ref, ksvt_ref, i - 1, near(1), lambda g: sel_mask(g, i - 1), None, False)

    def far_body(kt, carry):
        online_step(ksk_ref, ksvt_ref, kt, None, lambda g: sel_mask(g, kt), far, False)
        return carry

    lax.fori_loop(0, jnp.maximum(i - 1, 0), far_body, 0)
    for g in groups:
        os_ref[g] = normalized(g)
    online_step(kwk_ref, kwvt_ref, i, near(0), None, None, True)

    @pl.when(i >= 1)
    def _():
        online_step(kwk_ref, kwvt_ref, i - 1, near(1), None, None, False)

    @pl.when(i >= 2)
    def _():
        online_step(kwk_ref, kwvt_ref, i - 2, near(2), None, None, False)

    gsig = jax.nn.sigmoid(gt_ref[0]).T
    for g in groups:
        o_c, o_s, o_w = oc_ref[g], os_ref[g], normalized(g)
        for r in range(GQA_R):
            h = g * GQA_R + r
            cols = slice(r * QT, (r + 1) * QT)
            outt_ref[h * HEAD_DIM:(h + 1) * HEAD_DIM, :] = (
                gsig[h:h + 1, :] * o_c[:, cols]
                + gsig[N_HEADS + h:N_HEADS + h + 1, :] * o_s[:, cols]
                + gsig[2 * N_HEADS + h:2 * N_HEADS + h + 1, :] * o_w[:, cols])
    o_ref[0] = outt_ref[...].T.astype(BF16)


def _prompt_attn(q, gt, ck, cvt, ksk, ksvt, kwk, kwvt, tables, bsz, t_len):
    near, far, biasc, ovt4 = tables
    n_half = t_len // CMP_STRIDE
    n_sel = t_len // SEL_BLOCK
    nkt = t_len // KT
    per_b2 = lambda c: pl.BlockSpec((t_len, c), lambda b, i: (b, 0))
    vt_spec = pl.BlockSpec((nkt, VT_ROWS, KT), lambda b, i: (b, 0, 0))
    return pl.pallas_call(
        functools.partial(_prompt_attn_kernel, n_sel=n_sel),
        grid=(bsz, t_len // QT),
        in_specs=[pl.BlockSpec((1, QT, N_HEADS * LANES), lambda b, i: (b, i, 0)),
                  pl.BlockSpec((1, QT, LANES), lambda b, i: (b, i, 0)),
                  pl.BlockSpec((1, n_half, LANES), lambda b, i: (b, 0, 0)),
                  pl.BlockSpec((1, LANES, n_half), lambda b, i: (b, 0, 0)),
                  per_b2(LANES), vt_spec, per_b2(LANES), vt_spec,
                  pl.BlockSpec((1, N_KV, n_half, GQA_R * QT), lambda b, i: (i, 0, 0, 0)),
                  _const_spec(near.shape), _const_spec(far.shape), _const_spec(ovt4.shape)],
        out_specs=pl.BlockSpec((1, QT, NSA_W), lambda b, i: (b, i, 0)),
        out_shape=jax.ShapeDtypeStruct((bsz, t_len, NSA_W), BF16),
        scratch_shapes=[pltpu.VMEM((n_sel, QT), F32), pltpu.VMEM((N_KV, n_sel, QT), F32),
                        pltpu.VMEM((N_KV, 1, GQA_R * QT), F32), pltpu.VMEM((N_KV, VT_ROWS, GQA_R * QT), F32),
                        pltpu.VMEM((N_KV, HEAD_DIM, GQA_R * QT), F32), pltpu.VMEM((N_KV, HEAD_DIM, GQA_R * QT), F32),
                        pltpu.VMEM((NSA_W, QT), F32)],
        compiler_params=_cparams("parallel", "arbitrary"),
        name="prompt_attn",
    )(q, gt, ck, cvt, ksk, ksvt, kwk, kwvt, biasc, near, far, ovt4)


def _oproj_kernel(x_ref, a_ref, b_ref, woa_ref, wob_ref, g_ref, wq_ref, x1_ref, q_ref):
    x1 = (x_ref[...] + jnp.dot(a_ref[...], woa_ref[...], preferred_element_type=F32)
          + jnp.dot(b_ref[...], wob_ref[...], preferred_element_type=F32))
    x1_ref[...] = x1
    h = _rms(x1, g_ref[...]).astype(BF16)
    q_ref[...] = jnp.dot(h, wq_ref[...], preferred_element_type=F32).astype(BF16)


def _oproj(x2d, a, b, w_oa, w_ob, g, w_xq):
    n = x2d.shape[0]
    row = lambda c: pl.BlockSpec((ROW_TILE, c), lambda i: (i, 0))
    hq = MEM_HEADS * MEM_HD
    return pl.pallas_call(
        _oproj_kernel,
        grid=(n // ROW_TILE,),
        in_specs=[row(D_MODEL), row(SGU_W), row(NSA_W), _const_spec(w_oa.shape), _const_spec(w_ob.shape),
                  _const_spec((1, D_MODEL)), _const_spec(w_xq.shape)],
        out_specs=[row(D_MODEL), row(hq)],
        out_shape=[jax.ShapeDtypeStruct((n, D_MODEL), F32), jax.ShapeDtypeStruct((n, hq), BF16)],
        compiler_params=_cparams("parallel"),
        name="oproj_xq",
    )(x2d, a, b, w_oa, w_ob, g, w_xq)


def _memkv_kernel(x_ref, g_ref, w_ref, o_ref, ob_ref):
    h = _rms(x_ref[...], g_ref[...]).astype(BF16)
    o = jnp.dot(h, w_ref[...], preferred_element_type=F32)
    o_ref[...] = o
    ob_ref[...] = o.astype(BF16)


def _memkv(mem2d, g, w):
    n = mem2d.shape[0]
    c = w.shape[1]
    row = lambda cc: pl.BlockSpec((ROW_TILE, cc), lambda i: (i, 0))
    return pl.pallas_call(
        _memkv_kernel,
        grid=(n // ROW_TILE,),
        in_specs=[row(D_MODEL), _const_spec((1, D_MODEL)), _const_spec(w.shape)],
        out_specs=[row(c), row(c)],
        out_shape=[jax.ShapeDtypeStruct((n, c), F32), jax.ShapeDtypeStruct((n, c), BF16)],
        compiler_params=_cparams("parallel"),
        name="memkv_proj",
    )(mem2d, g, w)


def _softmax_rows(s):
    e = jnp.exp(s - jnp.max(s, axis=-1, keepdims=True))
    return e / jnp.sum(e, axis=-1, keepdims=True)


def _xattn_tail(x1, o, wxo_ref, g_ref, wr_ref, br_ref, x2_ref, h_ref, te_ref, tg_ref):
    x2 = x1 + jnp.dot(o.astype(BF16), wxo_ref[...], preferred_element_type=F32)
    x2_ref[...] = x2
    hb = _rms(x2, g_ref[...]).astype(BF16)
    h_ref[...] = hb
    lt = lax.dot_general(wr_ref[...], hb, (((1,), (1,)), ((), ())), preferred_element_type=F32) + br_ref[...]
    eidx = lax.broadcasted_iota(jnp.int32, lt.shape, 0)
    tops, idxs = [], []
    for _ in range(TOP_K):
        m = jnp.max(lt, axis=0, keepdims=True)
        idx = jnp.min(jnp.where(lt == m, eidx, N_EXPERTS), axis=0, keepdims=True)
        tops.append(m)
        idxs.append(idx)
        lt = jnp.where(eidx == idx, -jnp.inf, lt)
    es = [jnp.exp(t - tops[0]) for t in tops]
    den = es[0] + es[1] + es[2] + es[3]
    te_ref[...] = jnp.concatenate(idxs, axis=0)
    tg_ref[...] = jnp.concatenate([e / den for e in es], axis=0)


def _xattn_prompt_kernel(x1_ref, q_ref, kv_ref, wxo_ref, g_ref, wr_ref, br_ref, x2_ref, h_ref, te_ref, tg_ref):
    outs = []
    for hh in range(MEM_HEADS):
        qh = q_ref[:, hh * MEM_HD:(hh + 1) * MEM_HD]
        kh = kv_ref[:, hh * MEM_HD:(hh + 1) * MEM_HD]
        vh = kv_ref[:, (MEM_HEADS + hh) * MEM_HD:(MEM_HEADS + hh + 1) * MEM_HD]
        s = lax.dot_general(qh, kh, (((1,), (1,)), ((), ())), preferred_element_type=F32) * (MEM_HD ** -0.5)
        outs.append(jnp.dot(_softmax_rows(s).astype(BF16), vh, preferred_element_type=F32))
    o = jnp.concatenate(outs, axis=1)
    _xattn_tail(x1_ref[...], o, wxo_ref, g_ref, wr_ref, br_ref, x2_ref, h_ref, te_ref, tg_ref)


def _xattn_sample_kernel(x1_ref, q_ref, kv_ref, wxo_ref, g_ref, wr_ref, br_ref, x2_ref, h_ref, te_ref, tg_ref,
                         o_scr, *, t_len):
    nb = q_ref.shape[0] // t_len
    qf = q_ref[...].astype(F32)
    for bb in range(nb):
        for hh in range(MEM_HEADS):
            qh = qf[bb * t_len:(bb + 1) * t_len, hh * MEM_HD:(hh + 1) * MEM_HD].astype(BF16)
            kh = kv_ref[bb, pl.ds(hh, MEM_LEN, stride=2 * MEM_HEADS), :].astype(BF16)
            vh = kv_ref[bb, pl.ds(MEM_HEADS + hh, MEM_LEN, stride=2 * MEM_HEADS), :].astype(BF16)
            s = lax.dot_general(qh, kh, (((1,), (1,)), ((), ())), preferred_element_type=F32) * (MEM_HD ** -0.5)
            o_scr[bb * t_len:(bb + 1) * t_len, hh * MEM_HD:(hh + 1) * MEM_HD] = jnp.dot(
                _softmax_rows(s).astype(BF16), vh, preferred_element_type=F32)
    _xattn_tail(x1_ref[...], o_scr[...], wxo_ref, g_ref, wr_ref, br_ref, x2_ref, h_ref, te_ref, tg_ref)


XS_BATCH = 16


def _xattn(x1, q, kv, w_xo, g, w_rt, b_r, t_len, prompt):
    n = x1.shape[0]
    hq = MEM_HEADS * MEM_HD
    if prompt:
        rows = ROW_TILE
        per_b = t_len // rows
        kv_spec = pl.BlockSpec((MEM_LEN, 2 * hq), lambda i: (i // per_b, 0))
        kern = _xattn_prompt_kernel
        scratch = []
    else:
        rows = XS_BATCH * t_len
        kv_spec = pl.BlockSpec((XS_BATCH, MEM_LEN * 2 * MEM_HEADS, MEM_HD), lambda i: (i, 0, 0))
        kern = functools.partial(_xattn_sample_kernel, t_len=t_len)
        scratch = [pltpu.VMEM((rows, hq), F32)]
    row = lambda c: pl.BlockSpec((rows, c), lambda i: (i, 0))
    col = pl.BlockSpec((TOP_K, rows), lambda i: (0, i))
    return pl.pallas_call(
        kern,
        grid=(n // rows,),
        in_specs=[row(D_MODEL), row(hq), kv_spec, _const_spec(w_xo.shape), _const_spec((1, D_MODEL)),
                  _const_spec(w_rt.shape), _const_spec(b_r.shape)],
        out_specs=[row(D_MODEL), row(D_MODEL), col, col],
        out_shape=[jax.ShapeDtypeStruct((n, D_MODEL), F32), jax.ShapeDtypeStruct((n, D_MODEL), BF16),
                   jax.ShapeDtypeStruct((TOP_K, n), jnp.int32), jax.ShapeDtypeStruct((TOP_K, n), F32)],
        scratch_shapes=scratch,
        compiler_params=_cparams("parallel"),
        name="xattn_router_prompt" if prompt else "xattn_router_sample",
    )(x1, q, kv, w_xo, g, w_rt, b_r)


MOE_BLOCK = 512


def _expert_kernel(blk_ref, exp_ref, lo_ref, hi_ref, x_ref, wgu_ref, bgu_ref, wd_ref, bd_ref, y_ref,
                   wgu_b, wd_b):
    w = pl.program_id(0)
    lo, hi, row0 = lo_ref[w], hi_ref[w], blk_ref[w] * MOE_BLOCK

    @pl.when((w == 0) | (exp_ref[w] != exp_ref[jnp.maximum(w - 1, 0)]))
    def _():
        wgu_b[...] = wgu_ref[0].astype(BF16)
        wd_b[...] = wd_ref[0].astype(BF16)

    @pl.when(hi > lo)
    def _():
        gu = jnp.dot(x_ref[...], wgu_b[...], preferred_element_type=F32) + bgu_ref[0]
        glu = jnp.minimum(gu[:, :D_FF], SWIGLU_LIMIT)
        lin = jnp.clip(gu[:, D_FF:], -SWIGLU_LIMIT, SWIGLU_LIMIT)
        hdn = glu * jax.nn.sigmoid(SWIGLU_ALPHA * glu) * (lin + 1.0)
        y = jnp.dot(hdn.astype(BF16), wd_b[...], preferred_element_type=F32) + bd_ref[0]
        row = row0 + lax.broadcasted_iota(jnp.int32, (MOE_BLOCK, 1), 0)
        mine = (row >= lo) & (row < hi)

        @pl.when(lo == row0)
        def _():
            y_ref[...] = jnp.where(mine, y, 0.0).astype(BF16)

        @pl.when(lo != row0)
        def _():
            y_ref[...] = jnp.where(mine, y.astype(BF16), y_ref[...])


def _experts(items, xs, w_gu, b_gu, w_down, b_down):
    blk, exp, lo, hi = items
    n_rows = xs.shape[0]
    im = lambda f: (lambda w, blk, exp, lo, hi: f(blk[w], exp[w]))
    grid_spec = pltpu.PrefetchScalarGridSpec(
        num_scalar_prefetch=4,
        grid=(blk.shape[0],),
        in_specs=[pl.BlockSpec((MOE_BLOCK, D_MODEL), im(lambda b, e: (b, 0))),
                  pl.BlockSpec((1, D_MODEL, 2 * D_FF), im(lambda b, e: (e, 0, 0))),
                  pl.BlockSpec((1, 1, 2 * D_FF), im(lambda b, e: (e, 0, 0))),
                  pl.BlockSpec((1, D_FF, D_MODEL), im(lambda b, e: (e, 0, 0))),
                  pl.BlockSpec((1, 1, D_MODEL), im(lambda b, e: (e, 0, 0)))],
        out_specs=pl.BlockSpec((MOE_BLOCK, D_MODEL), im(lambda b, e: (b, 0))),
        scratch_shapes=[pltpu.VMEM((D_MODEL, 2 * D_FF), BF16), pltpu.VMEM((D_FF, D_MODEL), BF16)],
    )
    return pl.pallas_call(
        _expert_kernel,
        grid_spec=grid_spec,
        out_shape=jax.ShapeDtypeStruct((n_rows, D_MODEL), BF16),
        compiler_params=_cparams("arbitrary"),
        name="moe_experts",
    )(blk, exp, lo, hi, xs, w_gu, b_gu, w_down, b_down)


def _moe_dispatch(te):
    n = te.shape[1]
    nk = TOP_K * n
    assert nk % MOE_BLOCK == 0
    n_blk = nk // MOE_BLOCK
    iota = jnp.arange(nk, dtype=jnp.int32)
    flat_e = te.reshape(nk)
    _, order = lax.sort_key_val(flat_e, iota)
    pos = lax.sort_key_val(order, iota)[1].reshape(TOP_K, n)
    counts = jnp.sum(flat_e[:, None] == jnp.arange(N_EXPERTS)[None, :], axis=0, dtype=jnp.int32)
    start = jnp.cumsum(counts) - counts
    lo = jnp.sort(jnp.concatenate([jnp.arange(n_blk, dtype=jnp.int32) * MOE_BLOCK, start[1:]]))
    hi = jnp.concatenate([lo[1:], jnp.full((1,), nk, jnp.int32)])
    blk = jnp.minimum(lo // MOE_BLOCK, n_blk - 1)
    exp = jnp.sum(start[None, 1:] <= lo[:, None], axis=1, dtype=jnp.int32)
    return order % n, pos, (blk, exp, lo, hi)


def _final_kernel(x_ref, y_ref, g_ref, fn_ref, o_ref):
    x = x_ref[...]
    for k in range(TOP_K):
        x = x + y_ref[k].astype(F32) * g_ref[:, k:k + 1]
    o_ref[...] = _rms(x, fn_ref[...])


def _final(x2, yk, gates, fnorm):
    n = x2.shape[0]
    rows = 256
    return pl.pallas_call(
        _final_kernel,
        grid=(n // rows,),
        in_specs=[pl.BlockSpec((rows, D_MODEL), lambda i: (i, 0)),
                  pl.BlockSpec((TOP_K, rows, D_MODEL), lambda i: (0, i, 0)),
                  pl.BlockSpec((rows, TOP_K), lambda i: (i, 0)), _const_spec((1, D_MODEL))],
        out_specs=pl.BlockSpec((rows, D_MODEL), lambda i: (i, 0)),
        out_shape=jax.ShapeDtypeStruct((n, D_MODEL), F32),
        compiler_params=_cparams("parallel"),
        name="moe_combine_final_norm",
    )(x2, yk, gates, fnorm)


def _page_copy(pool_ref, pt_ref, buf_ref, sem_ref, b, p, slot, n_pages, rows):
    return pltpu.make_async_copy(pool_ref.at[pt_ref[b * n_pages + p]],
                                 buf_ref.at[slot, pl.ds(p * rows, rows)], sem_ref.at[slot])


def _fetch_pages(pool_ref, pt_ref, buf_ref, sem_ref, b, slot, n_pages, rows):
    def body(p, c):
        _page_copy(pool_ref, pt_ref, buf_ref, sem_ref, b, p, slot, n_pages, rows).start()
        return c
    lax.fori_loop(0, n_pages, body, 0)


def _wait_pages(pool_ref, pt_ref, buf_ref, sem_ref, b, slot, n_pages, rows):
    def body(p, c):
        _page_copy(pool_ref, pt_ref, buf_ref, sem_ref, b, p, slot, n_pages, rows).wait()
        return c
    lax.fori_loop(0, n_pages, body, 0)


def _compress_sample_kernel(pt_ref, pool_ref, hi_ref, w_ref, b_ref, perm_ref, ck_ref, cv_ref, buf_ref, sem_ref, tok_ref,
                            *, n_pages, page_rows):
    b = pl.program_id(0)
    nb = pl.num_programs(0)
    slot = lax.rem(b, 2)
    rows = 2 * LANES
    n_half = n_pages * page_rows // CMP_STRIDE

    @pl.when(b == 0)
    def _():
        _fetch_pages(pool_ref, pt_ref, buf_ref, sem_ref, b, slot, n_pages, rows)

    @pl.when(b + 1 < nb)
    def _():
        _fetch_pages(pool_ref, pt_ref, buf_ref, sem_ref, b + 1, 1 - slot, n_pages, rows)

    _wait_pages(pool_ref, pt_ref, buf_ref, sem_ref, b, slot, n_pages, rows)

    def regroup(p, carry):
        page = buf_ref[slot, pl.ds(pl.multiple_of(p * rows, rows), rows), :]
        _regroup_page(page, perm_ref, tok_ref, pl.multiple_of(p * PER_PAGE, PER_PAGE))
        return carry

    lax.fori_loop(0, n_pages, regroup, 0, unroll=8)
    last = lax.broadcasted_iota(jnp.int32, (n_half, LANES), 0) == n_half - 1
    for c, out in enumerate((ck_ref, cv_ref)):
        acc = _compress_halves(tok_ref, w_ref, c)
        hi_next = pltpu.roll(acc[:, LANES:], n_half - 1, 0)
        hi_next = jnp.where(last, hi_ref[0, c:c + 1, :], hi_next)
        out[0] = (acc[:, :LANES] + hi_next + b_ref[c:c + 1, :]).astype(BF16)


def _hi_new_kernel(ak_ref, av_ref, w_ref, o_ref):
    for c, a_ref in enumerate((ak_ref, av_ref)):
        o_ref[:, c, :] = jnp.dot(a_ref[...].astype(BF16), w_ref[c], preferred_element_type=F32)


def _hi_new(kck_new, kcv_new, w_cmp, n_batch, t_len):
    w_hi = w_cmp[:, :t_len, :, LANES:].reshape(2, t_len * LANES, LANES)
    ak = kck_new.reshape(n_batch, t_len * LANES)
    av = kcv_new.reshape(n_batch, t_len * LANES)
    return pl.pallas_call(
        _hi_new_kernel,
        out_shape=jax.ShapeDtypeStruct((n_batch, 2, LANES), F32),
        name="compress_new_tokens",
    )(ak, av, w_hi)


def _compress_sample(page_table_flat, pool, hi_new, w_cmp, b_cmp, n_batch, n_pages, page_rows):
    assert page_rows == PAGE
    perm = _regroup_perm()
    w_cmp = w_cmp.reshape(2, CMP_STRIDE // 2, 2 * LANES, 2 * LANES)
    n_half = n_pages * page_rows // CMP_STRIDE
    grid_spec = pltpu.PrefetchScalarGridSpec(
        num_scalar_prefetch=1,
        grid=(n_batch,),
        in_specs=[pl.BlockSpec(memory_space=pl.ANY),
                  pl.BlockSpec((1, 2, LANES), lambda b, pt: (b, 0, 0)),
                  pl.BlockSpec(w_cmp.shape, lambda b, pt: (0, 0, 0, 0)),
                  pl.BlockSpec(b_cmp.shape, lambda b, pt: (0, 0)),
                  pl.BlockSpec(perm.shape, lambda b, pt: (0, 0))],
        out_specs=[pl.BlockSpec((1, n_half, LANES), lambda b, pt: (b, 0, 0)),
                   pl.BlockSpec((1, n_half, LANES), lambda b, pt: (b, 0, 0))],
        scratch_shapes=[pltpu.VMEM((2, n_pages * 2 * LANES, page_rows), F32), pltpu.SemaphoreType.DMA((2,)),
                        pltpu.VMEM((2, CMP_STRIDE // 2, n_half, 2 * LANES), F32)],
    )
    return pl.pallas_call(
        functools.partial(_compress_sample_kernel, n_pages=n_pages, page_rows=page_rows),
        grid_spec=grid_spec,
        out_shape=[jax.ShapeDtypeStruct((n_batch, n_half, LANES), BF16)] * 2,
        compiler_params=_cparams("arbitrary"),
        name="compress_sample",
    )(page_table_flat, pool, hi_new, w_cmp, b_cmp, perm)


N_SEL_PAD = 256


def _sample_tables(rel_bias, past, t_len):
    rel_bias = rel_bias * LOG2E
    n_cmp = past // CMP_STRIDE
    n_sel = past // SEL_BLOCK + 1
    y = np.tile(np.arange(t_len), N_HEADS)[:, None]
    hh = np.repeat(np.arange(N_HEADS), t_len)[:, None]

    def tab(dist, valid):
        return _head_table(rel_bias, dist, valid, hh)

    dc = past + y - (np.arange(n_cmp)[None, :] * CMP_STRIDE + CMP_LEN - 1)
    biasc = tab(dc, dc >= 0)
    dp = past + y - np.arange(past)[None, :]
    biasp = tab(dp, dp >= 0)
    dn = y - np.arange(t_len)[None, :]
    biasn = tab(dn, dn >= 0)
    dw = WINDOW + y - np.arange(WINDOW)[None, :]
    biasw = tab(dw, dw < WINDOW)
    cs = np.arange(n_cmp)[:, None] * CMP_STRIDE
    ss = np.arange(N_SEL_PAD)[None, :] * SEL_BLOCK
    ov = ((cs < ss + SEL_BLOCK) & (cs + CMP_LEN > ss) & (np.arange(N_SEL_PAD)[None, :] < n_sel))
    ov4 = jnp.asarray(np.tile(ov.astype(np.float32), (GQA_R, 1)), BF16)
    expand = (np.arange(past)[None, :] // SEL_BLOCK == np.arange(past // SEL_BLOCK)[:, None])
    return biasc, biasp, biasn, biasw, ov4, jnp.asarray(expand.astype(np.float32), BF16)


def _sample_attn_kernel(pt_ref, q_ref, gt_ref, ck_ref, cv_ref, pool_ref, win_ref, ksk_ref, ksv_ref, kwk_ref,
                        kwv_ref, biasc_ref, biasp_ref, biasn_ref, biasw_ref, ov_ref, exp_ref, o_ref,
                        buf_ref, sem_ref, s_ref, *, n_pages, page_tokens, t_len, cur_block):
    b = pl.program_id(0)
    nb = pl.num_programs(0)
    slot = lax.rem(b, 2)
    rows = 2 * LANES
    nt = (((1,), (1,)), ((), ()))
    n_rows = N_HEADS * t_len

    @pl.when(b == 0)
    def _():
        _fetch_pages(pool_ref, pt_ref, buf_ref, sem_ref, b, slot, n_pages, rows)

    @pl.when(b + 1 < nb)
    def _():
        _fetch_pages(pool_ref, pt_ref, buf_ref, sem_ref, b + 1, 1 - slot, n_pages, rows)

    qb = q_ref[0]
    bc = biasc_ref[...]
    sc = lax.dot_general(qb, ck_ref[0], nt, preferred_element_type=F32) + bc
    e = jnp.exp2(sc - jnp.max(sc, axis=1, keepdims=True))
    e = jnp.where(bc > 0.5 * NEG, e, 0.0)
    lsum = jnp.sum(e, axis=1, keepdims=True)
    pb = (e / jnp.where(lsum > 0.0, lsum, 1.0)).astype(BF16)
    o_c = jnp.dot(pb, cv_ref[0], preferred_element_type=F32)
    p4 = jnp.concatenate(
        [jnp.concatenate([pb[(g * GQA_R + r) * t_len:(g * GQA_R + r + 1) * t_len, :] for r in range(GQA_R)], axis=1)
         for g in range(N_KV)], axis=0)
    imp = jnp.dot(p4, ov_ref[...], preferred_element_type=F32)
    lane = lax.broadcasted_iota(jnp.int32, imp.shape, 1)
    imp = jnp.where((lane == 0) | (lane == cur_block) | (lane == cur_block - 1), FORCE, imp)
    imp = jnp.where(lane > cur_block, NEG, imp)

    def rank_body(k, cnt):
        other = pltpu.roll(imp, k, 1)
        beats = (other > imp) | ((other == imp) & (lane >= k))
        return cnt + jnp.where(beats, 1.0, 0.0)

    cnt = lax.fori_loop(1, N_SEL_PAD, rank_body, jnp.zeros(imp.shape, F32), unroll=15)
    sel = jnp.where((cnt < N_SELECT) & (lane <= cur_block), 1.0, 0.0)
    sel_rows = jnp.concatenate([sel[g * t_len:(g + 1) * t_len, :] for g in range(N_KV) for _ in range(GQA_R)], axis=0)
    n_past_blocks = n_pages * page_tokens // SEL_BLOCK
    keep = jnp.dot(sel_rows[:, :n_past_blocks].astype(BF16), exp_ref[...], preferred_element_type=F32)

    def softmax_pair(s_old, s_new):
        m = jnp.maximum(jnp.max(s_old, axis=1, keepdims=True), jnp.max(s_new, axis=1, keepdims=True))
        p_old = jnp.exp2(s_old - m)
        p_new = jnp.exp2(s_new - m)
        inv = 1.0 / (jnp.sum(p_old, axis=1, keepdims=True) + jnp.sum(p_new, axis=1, keepdims=True))
        return p_old, p_new, inv

    _wait_pages(pool_ref, pt_ref, buf_ref, sem_ref, b, slot, n_pages, rows)
    span = 2 if n_pages % 2 == 0 else 1

    def pages(p, row0):
        return jnp.concatenate([buf_ref[slot, (p + j) * rows + row0:(p + j) * rows + row0 + LANES, :]
                                for j in range(span)], axis=1).astype(BF16)

    for p in range(0, n_pages, span):
        s_ref[:, p * page_tokens:(p + span) * page_tokens] = jnp.dot(qb, pages(p, 0), preferred_element_type=F32)
    s_past = jnp.where(keep > 0.5, s_ref[...] + biasp_ref[...], NEG)
    bn = biasn_ref[...]
    s_new = lax.dot_general(qb, ksk_ref[...].astype(BF16), nt, preferred_element_type=F32) + bn
    p_past, p_new, inv = softmax_pair(s_past, s_new)
    p_past = p_past.astype(BF16)
    o_s = jnp.dot(p_new.astype(BF16), ksv_ref[...].astype(BF16), preferred_element_type=F32)
    for p in range(0, n_pages, span):
        o_s = o_s + lax.dot_general(p_past[:, p * page_tokens:(p + span) * page_tokens], pages(p, LANES), nt,
                                    preferred_element_type=F32)
    o_s = o_s * inv
    s_win = jnp.dot(qb, win_ref[0, :LANES, :].astype(BF16), preferred_element_type=F32) + biasw_ref[...]
    s_wnew = lax.dot_general(qb, kwk_ref[...].astype(BF16), nt, preferred_element_type=F32) + bn
    p_win, p_wnew, inv_w = softmax_pair(s_win, s_wnew)
    o_w = (lax.dot_general(p_win.astype(BF16), win_ref[0, LANES:, :].astype(BF16), nt, preferred_element_type=F32)
           + jnp.dot(p_wnew.astype(BF16), kwv_ref[...].astype(BF16), preferred_element_type=F32)) * inv_w
    gs = jax.nn.sigmoid(gt_ref[0])
    o = gs[:, 0:1] * o_c + gs[:, 1:2] * o_s + gs[:, 2:3] * o_w
    row = lax.broadcasted_iota(jnp.int32, (n_rows, HEAD_DIM), 0)
    o_ref[0] = jnp.where(row < GQA_R * t_len, o[:, :HEAD_DIM], o[:, HEAD_DIM:])


def _sample_attn(page_table_flat, q, gt, ck, cv, pool, win, ksk, ksv, kwk, kwv, tables, n_batch, n_pages, page_tokens,
                 t_len):
    biasc, biasp, biasn, biasw, ov4, expand = tables
    past = n_pages * page_tokens
    n_rows = N_HEADS * t_len
    cur_block = past // SEL_BLOCK
    assert past % SEL_BLOCK == 0 and t_len <= SEL_BLOCK and cur_block < N_SEL_PAD
    new_spec = pl.BlockSpec((t_len, LANES), lambda b, pt: (b, 0))
    cst = lambda a: pl.BlockSpec(a.shape, lambda b, pt: (0,) * a.ndim)
    grid_spec = pltpu.PrefetchScalarGridSpec(
        num_scalar_prefetch=1,
        grid=(n_batch,),
        in_specs=[pl.BlockSpec((1, n_rows, LANES), lambda b, pt: (b, 0, 0)),
                  pl.BlockSpec((1, n_rows, 3), lambda b, pt: (b, 0, 0)),
                  pl.BlockSpec((1, past // CMP_STRIDE, LANES), lambda b, pt: (b, 0, 0)),
                  pl.BlockSpec((1, past // CMP_STRIDE, LANES), lambda b, pt: (b, 0, 0)),
                  pl.BlockSpec(memory_space=pl.ANY),
                  pl.BlockSpec((1, 2 * LANES, WINDOW), lambda b, pt: (b, 0, 0)),
                  new_spec, new_spec, new_spec, new_spec,
                  cst(biasc), cst(biasp), cst(biasn), cst(biasw), cst(ov4), cst(expand)],
        out_specs=pl.BlockSpec((1, n_rows, HEAD_DIM), lambda b, pt: (b, 0, 0)),
        scratch_shapes=[pltpu.VMEM((2, n_pages * 2 * LANES, page_tokens), F32), pltpu.SemaphoreType.DMA((2,)),
                        pltpu.VMEM((n_rows, past), F32)],
    )
    return pl.pallas_call(
        functools.partial(_sample_attn_kernel, n_pages=n_pages, page_tokens=page_tokens, t_len=t_len,
                          cur_block=cur_block),
        grid_spec=grid_spec,
        out_shape=jax.ShapeDtypeStruct((n_batch, n_rows, HEAD_DIM), F32),
        compiler_params=_cparams("arbitrary"),
        name="sample_attn",
    )(page_table_flat, q, gt, ck, cv, pool, win, ksk, ksv, kwk, kwv, biasc, biasp, biasn, biasw, ov4, expand)


def kernel(x_prompt, x_sample, mem_prompt, cache_cmp_kv, cache_slc_kv, cache_win_kv, cache_mem_kv, page_table, norm_mix, w_in, sgu_norm_g, sgu_norm_b, sgu_w, sgu_b, cmp_w, cmp_b, w_o, rel_bias, norm_x, norm_mem, w_xq, w_mem_kv, w_xo, norm_ffn, w_router, b_router, w_gu, b_gu, w_down, b_down, final_norm):
    assert norm_mix.shape[0] == 1, "single-layer trunk"
    bsz, t_p, _ = x_prompt.shape
    dbs, t_s, _ = x_sample.shape
    n_phys, page_tokens = cache_cmp_kv.shape[1], cache_cmp_kv.shape[2]
    n_pages = page_table.shape[1]
    past = n_pages * page_tokens
    wb = cache_win_kv.shape[2]
    assert wb == WINDOW and t_p >= WINDOW and t_p >= CHUNK and t_s <= CHUNK
    n_p, n_s = bsz * t_p, dbs * t_s
    row1 = lambda v: v.reshape(1, -1)

    w_all = _build_w_in(w_in[0])
    mix_p, mixb_p = _build_sgu_mix(sgu_w[0], sgu_b[0], CHUNK, CHUNK)
    mix_s, mixb_s = _build_sgu_mix(sgu_w[0], sgu_b[0], t_s, ROW_TILE)
    w_cmp = _build_cmp_w(cmp_w[0])
    b_cmp = cmp_b[0].reshape(2, LANES)
    w_oa, w_ob = w_o[0, :SGU_W].astype(BF16), w_o[0, SGU_W:].astype(BF16)
    w_xq_b, w_xo_b = w_xq[0].astype(BF16), w_xo[0].astype(BF16)
    w_rt = w_router[0].T.astype(BF16)
    b_rt = b_router[0].reshape(N_EXPERTS, 1)
    in_args = (row1(norm_mix[0]), w_all, row1(sgu_norm_g[0]), row1(sgu_norm_b[0]))
    tail_args = (w_xo_b, row1(norm_ffn[0]), w_rt, b_rt)

    xp = x_prompt.reshape(n_p, D_MODEL)
    (a_p, v_p, q_p, gt_p, kct, kst, kwt, kskb, ksvt, kwkb, kwvt) = _inproj(
        xp, *in_args, mix_p, mixb_p, t_p, True)
    ck_p, cvt_p = _compress_prompt(kct, w_cmp, b_cmp, bsz, t_p)
    b_p = _prompt_attn(q_p.reshape(bsz, t_p, N_HEADS * LANES), gt_p.reshape(bsz, t_p, LANES), ck_p, cvt_p,
                       kskb, ksvt, kwkb, kwvt, _attn_tables(rel_bias, t_p), bsz, t_p)
    x1_p, qm_p = _oproj(xp, a_p, b_p.reshape(n_p, NSA_W), w_oa, w_ob, row1(norm_x[0]), w_xq_b)
    memkv_f, memkv_b = _memkv(mem_prompt.reshape(bsz * MEM_LEN, D_MODEL), row1(norm_mem[0]), w_mem_kv[0].astype(BF16))
    x2_p, h_p, te_p, tg_p = _xattn(x1_p, qm_p, memkv_b, *tail_args, t_p, True)

    xs = x_sample.reshape(n_s, D_MODEL)
    (a_s, v_s, q_s, gt_s, kck_s, kcv_s, ksk_s, ksv_s, kwk_s, kwv_s) = _inproj(
        xs, *in_args, mix_s, mixb_s, t_s, False)
    pt_flat = page_table.reshape(-1)
    stored = lambda c: c[0].transpose(0, 2, 3, 4, 1).reshape(c.shape[1], 2 * LANES, c.shape[2])
    pool_c, pool_s = stored(cache_cmp_kv), stored(cache_slc_kv)
    win_t = cache_win_kv[0].transpose(0, 2, 3, 4, 1).reshape(dbs, 2 * LANES, wb)
    hi_new = _hi_new(kck_s, kcv_s, w_cmp, dbs, t_s)
    ck_s, cv_s = _compress_sample(pt_flat, pool_c, hi_new, w_cmp, b_cmp, dbs, n_pages, page_tokens)
    q_sb = q_s.reshape(dbs, t_s, N_HEADS, LANES).transpose(0, 2, 1, 3).reshape(dbs, N_HEADS * t_s, LANES)
    gt_sb = gt_s[:, :3 * N_HEADS].reshape(dbs, t_s, 3, N_HEADS).transpose(0, 3, 1, 2).reshape(dbs, N_HEADS * t_s, 3)
    bo_s = _sample_attn(pt_flat, q_sb, gt_sb, ck_s, cv_s, pool_s, win_t, ksk_s, ksv_s, kwk_s, kwv_s,
                        _sample_tables(rel_bias, past, t_s), dbs, n_pages, page_tokens, t_s)
    b_s = bo_s.reshape(dbs, N_HEADS, t_s, HEAD_DIM).transpose(0, 2, 1, 3).reshape(n_s, NSA_W).astype(BF16)
    x1_s, qm_s = _oproj(xs, a_s, b_s, w_oa, w_ob, row1(norm_x[0]), w_xq_b)
    memkv_s = cache_mem_kv[0].reshape(dbs, MEM_LEN * 2 * MEM_HEADS, MEM_HD)
    x2_s, h_s, te_s, tg_s = _xattn(x1_s, qm_s, memkv_s, *tail_args, t_s, False)

    row_tok, pos, items = _moe_dispatch(jnp.concatenate([te_p, te_s], axis=1))
    xs_sorted = jnp.concatenate([h_p, h_s], axis=0)[row_tok]
    y = _experts(items, xs_sorted, w_gu[0], b_gu[0][:, None, :], w_down[0], b_down[0][:, None, :])
    fn = row1(final_norm)
    y_prompt = _final(x2_p, y[pos[:, :n_p]], tg_p.T, fn).reshape(bsz, t_p, D_MODEL)
    y_sample = _final(x2_s, y[pos[:, n_p:]], tg_s.T, fn).reshape(dbs, t_s, D_MODEL)

    kv6 = lambda k, v, b, t: jnp.concatenate([k, v], axis=1).reshape(1, b, t, 2, N_KV, HEAD_DIM)
    unstored = lambda a: a.reshape(1, a.shape[0], 2, N_KV, HEAD_DIM, a.shape[2]).transpose(0, 1, 5, 2, 3, 4)
    kw_s = kv6(kwk_s, kwv_s, dbs, t_s)
    win_s = jnp.concatenate([cache_win_kv, kw_s], axis=2)[:, :, -wb:]
    return (y_prompt, y_sample, unstored(kct), unstored(kst), unstored(kwt[:, :, t_p - WINDOW:]),
            memkv_f.reshape(1, bsz, MEM_LEN, 2, MEM_HEADS, MEM_HD), v_p.reshape(1, bsz, t_p, SGU_W)[:, :, -CHUNK:],
            kv6(kck_s, kcv_s, dbs, t_s), kv6(ksk_s, ksv_s, dbs, t_s), win_s, v_s.reshape(1, dbs, t_s, SGU_W))
```

```python
import functools
import math

import numpy as np
import jax
import jax.numpy as jnp
from jax import lax
from jax.experimental import pallas as pl
from jax.experimental.pallas import tpu as pltpu

D_MODEL = 1024
SGU_W = 512
SGU_GROUPS = 4
SGU_GW = 128
CHUNK = 128
NSA_W = 512
HEAD_DIM = 64
N_HEADS = 8
N_KV = 2
GQA_R = 4
CMP_LEN = 32
CMP_STRIDE = 16
SEL_BLOCK = 64
N_SELECT = 16
WINDOW = 512
KV_W = 256
N_BUCKETS = 32
MAX_EXACT = 16
MAX_DIST = 128
MEM_LEN = 256
MEM_HEADS = 4
MEM_HD = 128
N_EXPERTS = 32
TOP_K = 4
D_FF = 1024
SWIGLU_LIMIT = 7.0
SWIGLU_ALPHA = 1.702
EPS = 1e-6
NEG = -1e30
FORCE = 1e9

LANES = 128
VMEM_LIMIT = 56 * 1024 * 1024

ROW_TILE = 512
QT = 256
KT = 256
CHAIN_HEADS = 4
RANK_UNROLL = 8
VT_ROWS = LANES + 16
LOG2E = math.log2(math.e)

F32 = jnp.float32
BF16 = jnp.bfloat16


def _cparams(*sem):
    return pltpu.CompilerParams(dimension_semantics=sem, vmem_limit_bytes=VMEM_LIMIT)


def _const_spec(shape):
    nd = len(shape)
    return pl.BlockSpec(shape, lambda *_: (0,) * nd)


def _rms(x, g):
    return x * lax.rsqrt(jnp.mean(x * x, axis=-1, keepdims=True) + EPS) * g


def _rel_bucket_np(dist):
    n = np.maximum(dist, 0)
    nf = np.maximum(n, 1).astype(np.float32)
    large = MAX_EXACT + (np.log(nf / np.float32(MAX_EXACT)) / np.float32(math.log(MAX_DIST / MAX_EXACT))
                         * np.float32(N_BUCKETS - MAX_EXACT)).astype(np.int32)
    large = np.minimum(large, N_BUCKETS - 1)
    return np.where(n < MAX_EXACT, n, large).astype(np.int32)


C_U, C_V, C_Q, C_KC, C_GT, C_END = 0, 512, 1024, 2048, 2816, 2944


def _inproj_kernel(x_ref, g_ref, w_ref, lng_ref, lnb_ref, mix_ref, mixb_ref, *out_refs, mix_block, attn_extras):
    a_ref, v_ref, q_ref, gt_ref = out_refs[:4]
    x = x_ref[...]
    h = _rms(x, g_ref[...]).astype(BF16)

    def proj(lo, hi):
        return jnp.dot(h, w_ref[:, lo:hi], preferred_element_type=F32)

    q_ref[...] = proj(C_Q, C_KC).astype(BF16)
    gt_ref[...] = proj(C_GT, C_END)
    kv = [proj(C_KC + LANES * j, C_KC + LANES * (j + 1)) for j in range(6)]
    if attn_extras:
        kvt_refs = out_refs[4:7]
        kskb_ref, ksvt_ref, kwkb_ref, kwvt_ref = out_refs[7:11]
        kskb_ref[...] = kv[2].astype(BF16)
        kwkb_ref[...] = kv[4].astype(BF16)
        for j, (kvt_ref, vtile_ref) in enumerate(zip(kvt_refs, (None, ksvt_ref, kwvt_ref))):
            kvt_ref[0, :LANES, :] = kv[2 * j].T
            vt = kv[2 * j + 1].T
            kvt_ref[0, LANES:, :] = vt
            if vtile_ref is not None:
                vtb = jnp.concatenate([vt.astype(BF16), jnp.ones((VT_ROWS - LANES, vt.shape[1]), BF16)], axis=0)
                for t in range(vt.shape[1] // KT):
                    vtile_ref[t] = vtb[:, t * KT:(t + 1) * KT]
    else:
        for r, val in zip(out_refs[4:10], kv):
            r[...] = val

    u = jax.nn.gelu(proj(C_U, C_V))
    v = jax.nn.gelu(proj(C_V, C_Q))
    mu = jnp.mean(v, axis=-1, keepdims=True)
    var = jnp.mean(jnp.square(v - mu), axis=-1, keepdims=True)
    v = (v - mu) * lax.rsqrt(var + EPS) * lng_ref[...] + lnb_ref[...]
    v_ref[...] = v
    vb = v.astype(BF16)
    rows = x.shape[0]
    for blk in range(rows // mix_block):
        r0 = blk * mix_block
        for g in range(SGU_GROUPS):
            c0 = g * SGU_GW
            mixed = jnp.dot(mix_ref[g], vb[r0:r0 + mix_block, c0:c0 + SGU_GW],
                            preferred_element_type=F32) + mixb_ref[:, c0:c0 + SGU_GW]
            a_ref[r0:r0 + mix_block, c0:c0 + SGU_GW] = (
                u[r0:r0 + mix_block, c0:c0 + SGU_GW] * mixed).astype(BF16)


def _inproj(x2d, g, w_all, ln_g, ln_b, mix, mixb, t_len, attn_extras):
    n = x2d.shape[0]
    mb = mix.shape[1]
    row = lambda c: pl.BlockSpec((ROW_TILE, c), lambda i: (i, 0))
    outs = [(SGU_W, BF16), (SGU_W, F32), (N_HEADS * LANES, BF16), (LANES, F32)]
    if not attn_extras:
        outs += [(LANES, F32)] * 6
    out_specs = [row(c) for c, _ in outs]
    out_shape = [jax.ShapeDtypeStruct((n, c), dt) for c, dt in outs]
    if attn_extras:
        per_b = t_len // ROW_TILE
        kvt_spec = pl.BlockSpec((1, 2 * LANES, ROW_TILE), lambda i: (i // per_b, 0, i % per_b))
        kvt_shape = jax.ShapeDtypeStruct((n // t_len, 2 * LANES, t_len), F32)
        tiles = ROW_TILE // KT
        vt_spec = pl.BlockSpec((tiles, VT_ROWS, KT), lambda i: (i, 0, 0))
        vt_shape = jax.ShapeDtypeStruct((n // KT, VT_ROWS, KT), BF16)
        out_specs += [kvt_spec] * 3 + [row(LANES), vt_spec, row(LANES), vt_spec]
        out_shape += [kvt_shape] * 3 + [jax.ShapeDtypeStruct((n, LANES), BF16), vt_shape,
                                        jax.ShapeDtypeStruct((n, LANES), BF16), vt_shape]
    return pl.pallas_call(
        functools.partial(_inproj_kernel, mix_block=mb, attn_extras=attn_extras),
        grid=(n // ROW_TILE,),
        in_specs=[row(D_MODEL), _const_spec((1, D_MODEL)), _const_spec(w_all.shape),
                  _const_spec((1, SGU_W)), _const_spec((1, SGU_W)), _const_spec(mix.shape),
                  _const_spec(mixb.shape)],
        out_specs=out_specs,
        out_shape=out_shape,
        compiler_params=_cparams("parallel"),
        name="inproj_sgu",
    )(x2d, g, w_all, ln_g, ln_b, mix, mixb)


def _build_w_in(w_in):
    u = w_in[:, 0:512]
    v = w_in[:, 512:1024]
    q = w_in[:, 1024:1536].reshape(D_MODEL, N_KV, GQA_R, HEAD_DIM) * (HEAD_DIM ** -0.5 * LOG2E)
    qp = jnp.zeros((D_MODEL, N_KV, GQA_R, N_KV, HEAD_DIM), F32)
    for g in range(N_KV):
        qp = qp.at[:, g, :, g, :].set(q[:, g])
    qp = qp.reshape(D_MODEL, N_HEADS * LANES)
    kv = w_in[:, 1536:2304]
    gt = jnp.pad(w_in[:, 2304:2328], ((0, 0), (0, LANES - 3 * N_HEADS)))
    return jnp.concatenate([u, v, qp, kv, gt], axis=1).astype(BF16)


def _build_sgu_mix(sgu_w, sgu_b, chunk, mix_block):
    tri = jnp.tril(jnp.ones((chunk, chunk), bool))
    w = jnp.where(tri, sgu_w[:, :chunk, :chunk], 0.0)
    reps = mix_block // chunk
    eye = jnp.eye(reps, dtype=F32)
    mix = jnp.einsum('ab,gst->gasbt', eye, w).reshape(SGU_GROUPS, mix_block, mix_block)
    b = jnp.tile(sgu_b[:, :chunk], (1, reps))
    mixb = jnp.repeat(b.T, SGU_GW, axis=1)
    return mix.astype(BF16), mixb


def _build_cmp_w(cmp_w):
    w = cmp_w.reshape(2, N_KV, 2, CMP_STRIDE, HEAD_DIM, HEAD_DIM)
    out = jnp.zeros((2, CMP_STRIDE, N_KV, HEAD_DIM, 2, N_KV, HEAD_DIM), F32)
    for g in range(N_KV):
        out = out.at[:, :, g, :, :, g, :].set(jnp.transpose(w[:, g], (0, 2, 3, 1, 4)))
    return out.reshape(2, CMP_STRIDE, LANES, 2 * LANES).astype(BF16)


PAGE = 128
PER_PAGE = PAGE // CMP_STRIDE


def _regroup_perm():
    tok = np.arange(PAGE)
    return jnp.asarray(np.arange(PAGE)[:, None] == ((tok % CMP_STRIDE) * PER_PAGE + tok // CMP_STRIDE)[None, :], BF16)


def _regroup_page(page, perm_ref, tok_ref, block0):
    t = lax.dot_general(perm_ref[...], page.astype(BF16), (((1,), (1,)), ((), ())), preferred_element_type=F32)
    for r in range(CMP_STRIDE):
        for c in range(2):
            tok_ref[c, r // 2, pl.ds(block0, PER_PAGE), (r % 2) * LANES:(r % 2 + 1) * LANES] = (
                t[r * PER_PAGE:(r + 1) * PER_PAGE, c * LANES:(c + 1) * LANES])


def _compress_halves(tok_ref, w_ref, c):
    acc = jnp.dot(tok_ref[c, 0].astype(BF16), w_ref[c, 0], preferred_element_type=F32)
    for q in range(1, CMP_STRIDE // 2):
        acc = acc + jnp.dot(tok_ref[c, q].astype(BF16), w_ref[c, q], preferred_element_type=F32)
    return acc


def _compress_kernel(kct_ref, w_ref, b_ref, perm_ref, ck_ref, cvt_ref, tok_ref, *, n_half):
    for p in range(n_half // PER_PAGE):
        _regroup_page(kct_ref[0, :, p * PAGE:(p + 1) * PAGE], perm_ref, tok_ref, p * PER_PAGE)
    for c in range(2):
        acc = _compress_halves(tok_ref, w_ref, c)
        hi_next = pltpu.roll(acc[:, LANES:], n_half - 1, 0)
        comp = acc[:, :LANES] + hi_next + b_ref[c:c + 1, :]
        if c == 0:
            ck_ref[0] = comp.astype(BF16)
        else:
            cvt_ref[0] = comp.T.astype(BF16)


def _compress_prompt(kct, w_cmp, b_cmp, bsz, t_len):
    n_half = t_len // CMP_STRIDE
    perm = _regroup_perm()
    w2 = w_cmp.reshape(2, CMP_STRIDE // 2, 2 * LANES, 2 * LANES)
    return pl.pallas_call(
        functools.partial(_compress_kernel, n_half=n_half),
        grid=(bsz,),
        in_specs=[pl.BlockSpec((1, 2 * LANES, t_len), lambda b: (b, 0, 0)),
                  _const_spec(w2.shape), _const_spec(b_cmp.shape), _const_spec(perm.shape)],
        out_specs=[pl.BlockSpec((1, n_half, LANES), lambda b: (b, 0, 0)),
                   pl.BlockSpec((1, LANES, n_half), lambda b: (b, 0, 0))],
        out_shape=[jax.ShapeDtypeStruct((bsz, n_half, LANES), BF16),
                   jax.ShapeDtypeStruct((bsz, LANES, n_half), BF16)],
        scratch_shapes=[pltpu.VMEM((2, CMP_STRIDE // 2, n_half, 2 * LANES), F32)],
        compiler_params=_cparams("parallel"),
        name="compress_prompt",
    )(kct, w2, b_cmp, perm)


def _toeplitz(rel_bias, rows, cols, off, valid_fn):
    span = rows + cols - 1
    d = np.arange(span) + off - (rows - 1)
    vec = jnp.where(valid_fn(d)[None, :], rel_bias[_rel_bucket_np(d)].T, NEG)
    flat = jnp.tile(vec, (1, rows + 1))[:, :rows * (span + 1)]
    skew = flat.reshape(N_HEADS, rows, span + 1)
    return skew[:, ::-1, :cols]


def _head_table(rel_bias, dist, valid, head):
    bucket = _rel_bucket_np(dist)
    need = valid & (bucket != N_BUCKETS - 1)
    out = jnp.broadcast_to(rel_bias[N_BUCKETS - 1][head[:, 0]][:, None], dist.shape)
    cols = np.nonzero(need.any(axis=0))[0]
    if cols.size:
        c0, c1 = int(cols[0]), int(cols[-1]) + 1
        mid = jnp.where(need[:, c0:c1], rel_bias[bucket[:, c0:c1], head], out[:, c0:c1])
        out = jnp.concatenate([out[:, :c0], mid, out[:, c1:]], axis=1)
    return jnp.where(valid, out, NEG)


def _attn_tables(rel_bias, t_len):
    rel_bias = rel_bias * LOG2E
    n_half = t_len // CMP_STRIDE
    n_sel = t_len // SEL_BLOCK
    nq = t_len // QT
    c_far = rel_bias[N_BUCKETS - 1]
    to_lanes = lambda a: a.reshape(N_KV, GQA_R, a.shape[1], QT).transpose(0, 2, 1, 3).reshape(
        N_KV, a.shape[1], GQA_R * QT)
    near = jnp.stack([
        to_lanes(_toeplitz(rel_bias, KT, QT, 0, lambda d: d >= 0)),
        to_lanes(_toeplitz(rel_bias, KT, QT, KT, lambda d: d >= 0)),
        to_lanes(_toeplitz(rel_bias, KT, QT, 2 * KT, lambda d: d < WINDOW))])
    far = jnp.repeat(c_far.reshape(N_KV, 1, GQA_R), QT, axis=2).reshape(N_KV, 1, GQA_R * QT)
    per_q = QT // CMP_STRIDE
    band_rows = 2 * per_q
    lead = per_q // 2
    off = lead * CMP_STRIDE - (CMP_LEN - 1)
    band = _toeplitz(rel_bias, band_rows * CMP_STRIDE, QT, off, lambda d: d >= 0)[:, ::CMP_STRIDE]
    assert _rel_bucket_np(np.array([off + CMP_STRIDE])).item() == N_BUCKETS - 1
    tiles = []
    for i in range(nq):
        lo = i * per_q - lead
        parts = []
        if lo > 0:
            parts.append(jnp.broadcast_to(c_far[:, None, None], (N_HEADS, lo, QT)))
        b0, b1 = max(0, -lo), min(band_rows, n_half - lo)
        parts.append(band[:, b0:b1])
        rest = n_half - (lo + b1)
        if rest > 0:
            parts.append(jnp.full((N_HEADS, rest, QT), NEG, F32))
        tiles.append(to_lanes(jnp.concatenate(parts, axis=1)))
    biasc = jnp.stack(tiles)
    cs = np.arange(n_half - 1)[None, :] * CMP_STRIDE
    ss = np.arange(n_sel)[:, None] * SEL_BLOCK
    ov = np.zeros((n_sel, n_half), np.float32)
    ov[:, :n_half - 1] = (cs < ss + SEL_BLOCK) & (cs + CMP_LEN > ss)
    ovt4 = jnp.asarray(np.tile(ov, (1, GQA_R)), BF16)
    return near, far, biasc, ovt4


def _prompt_attn_kernel(q_ref, gt_ref, ck_ref, cvt_ref, ksk_ref, ksvt_ref, kwk_ref, kwvt_ref,
                        biasc_ref, near_ref, far_ref, ovt_ref, o_ref,
                        imp_ref, selneg_ref, m_ref, acc_ref, oc_ref, os_ref, outt_ref, *, n_sel):
    i = pl.program_id(1)
    nt = (((1,), (1,)), ((), ()))
    groups = range(N_KV)
    own = lambda g: slice(g * HEAD_DIM, (g + 1) * HEAD_DIM)
    jj = lax.broadcasted_iota(jnp.int32, (n_sel, QT), 0)
    tt = i * QT + lax.broadcasted_iota(jnp.int32, (n_sel, QT), 1)
    cur = lax.shift_right_logical(tt, 6)
    forced = (jj == 0) | (jj == cur) | (jj == cur - 1)
    future = jj > cur
    qs = [jnp.concatenate([q_ref[0, :, (g * GQA_R + r) * LANES:(g * GQA_R + r + 1) * LANES]
                           for r in range(GQA_R)], axis=0) for g in groups]

    near = lambda idx: (lambda g, cols: near_ref[idx, g, :, cols])
    far = lambda g, cols: far_ref[g, :, cols]

    def online_step(k_ref, vt_ref, kt, bias_of, mask_of, shift_of, first):
        kk = k_ref[pl.ds(pl.multiple_of(kt * KT, KT), KT), :]
        vt = vt_ref[kt]
        for g in groups:
            mask = None if mask_of is None else jnp.concatenate([mask_of(g)] * CHAIN_HEADS, axis=1)
            for c0 in range(0, GQA_R, CHAIN_HEADS):
                cols = slice(c0 * QT, (c0 + CHAIN_HEADS) * QT)
                s = lax.dot_general(kk, qs[g][cols], nt, preferred_element_type=F32)
                if bias_of is not None:
                    s = s + bias_of(g, cols)
                if mask is not None:
                    s = s + mask
                mt = jnp.max(s, axis=0, keepdims=True)
                shift = None if shift_of is None else shift_of(g, cols)
                if shift is not None:
                    mt = mt + shift
                if first:
                    m_new = mt
                else:
                    m_old = m_ref[g, :, cols]
                    m_new = jnp.maximum(m_old, mt)
                    alpha = jnp.exp2(m_old - m_new)
                p = jnp.exp2(s - (m_new if shift is None else m_new - shift))
                pv = jnp.dot(vt, p.astype(BF16), preferred_element_type=F32)
                if first:
                    acc_ref[g, :, cols] = pv
                else:
                    acc_ref[g, :, cols] = alpha * acc_ref[g, :, cols] + pv
                m_ref[g, :, cols] = m_new

    def normalized(g):
        return acc_ref[g, own(g), :] / acc_ref[g, LANES:LANES + 1, :]

    def sel_mask(g, kt):
        rows = [jnp.broadcast_to(selneg_ref[g, pl.ds(kt * (KT // SEL_BLOCK) + b, 1), :], (SEL_BLOCK, QT))
                for b in range(KT // SEL_BLOCK)]
        return jnp.concatenate(rows, axis=0)

    for g in groups:
        bc = biasc_ref[0, g]
        sc = lax.dot_general(ck_ref[0], qs[g], nt, preferred_element_type=F32) + bc
        e = jnp.exp2(sc - jnp.max(sc, axis=0, keepdims=True))
        e = jnp.where(bc > 0.5 * NEG, e, 0.0)
        lsum = jnp.sum(e, axis=0, keepdims=True)
        pb = (e / jnp.where(lsum > 0.0, lsum, 1.0)).astype(BF16)
        oc_ref[g] = jnp.dot(cvt_ref[0], pb, preferred_element_type=F32)[own(g)]
        p4 = jnp.concatenate([pb[:, r * QT:(r + 1) * QT] for r in range(GQA_R)], axis=0)
        imp = jnp.dot(ovt_ref[...], p4, preferred_element_type=F32)
        imp = jnp.where(forced, FORCE, imp)
        imp = jnp.where(future, NEG, imp)
        imp_ref[...] = imp

        def rank_body(i8, cnt, imp=imp):
            for u in range(RANK_UNROLL):
                ii = i8 * RANK_UNROLL + u
                row = imp_ref[pl.ds(ii, 1), :]
                beats = (row > imp) | ((row == imp) & (jj > ii))
                cnt = cnt + jnp.where(beats, 1.0, 0.0)
            return cnt

        live_blocks = (i + 1) * (QT // SEL_BLOCK)
        cnt = lax.fori_loop(0, (live_blocks + RANK_UNROLL - 1) // RANK_UNROLL, rank_body,
                            jnp.zeros((n_sel, QT), F32))
        selneg_ref[g] = jnp.where((cnt < N_SELECT) & jnp.logical_not(future), 0.0, NEG)

    online_step(ksk_ref, ksvt_ref, i, near(0), lambda g: sel_mask(g, i), None, True)

    @pl.when(i >= 1)
    def _():
        online_step(ksk_ref, ksvt_ref, i - 1, near(1), lambda g: sel_mask(g, i - 1), None, False)

    def far_body(kt, carry):
        online_step(ksk_ref, ksvt_ref, kt, None, lambda g: sel_mask(g, kt), far, False)
        return carry

    lax.fori_loop(0, jnp.maximum(i - 1, 0), far_body, 0)
    for g in groups:
        os_ref[g] = normalized(g)
    online_step(kwk_ref, kwvt_ref, i, near(0), None, None, True)

    @pl.when(i >= 1)
    def _():
        online_step(kwk_ref, kwvt_ref, i - 1, near(1), None, None, False)

    @pl.when(i >= 2)
    def _():
        online_step(kwk_ref, kwvt_ref, i - 2, near(2), None, None, False)

    gsig = jax.nn.sigmoid(gt_ref[0]).T
    for g in groups:
        o_c, o_s, o_w = oc_ref[g], os_ref[g], normalized(g)
        for r in range(GQA_R):
            h = g * GQA_R + r
            cols = slice(r * QT, (r + 1) * QT)
            outt_ref[h * HEAD_DIM:(h + 1) * HEAD_DIM, :] = (
                gsig[h:h + 1, :] * o_c[:, cols]
                + gsig[N_HEADS + h:N_HEADS + h + 1, :] * o_s[:, cols]
                + gsig[2 * N_HEADS + h:2 * N_HEADS + h + 1, :] * o_w[:, cols])
    o_ref[0] = outt_ref[...].T.astype(BF16)


def _prompt_attn(q, gt, ck, cvt, ksk, ksvt, kwk, kwvt, tables, bsz, t_len):
    near, far, biasc, ovt4 = tables
    n_half = t_len // CMP_STRIDE
    n_sel = t_len // SEL_BLOCK
    nkt = t_len // KT
    per_b2 = lambda c: pl.BlockSpec((t_len, c), lambda b, i: (b, 0))
    vt_spec = pl.BlockSpec((nkt, VT_ROWS, KT), lambda b, i: (b, 0, 0))
    return pl.pallas_call(
        functools.partial(_prompt_attn_kernel, n_sel=n_sel),
        grid=(bsz, t_len // QT),
        in_specs=[pl.BlockSpec((1, QT, N_HEADS * LANES), lambda b, i: (b, i, 0)),
                  pl.BlockSpec((1, QT, LANES), lambda b, i: (b, i, 0)),
                  pl.BlockSpec((1, n_half, LANES), lambda b, i: (b, 0, 0)),
                  pl.BlockSpec((1, LANES, n_half), lambda b, i: (b, 0, 0)),
                  per_b2(LANES), vt_spec, per_b2(LANES), vt_spec,
                  pl.BlockSpec((1, N_KV, n_half, GQA_R * QT), lambda b, i: (i, 0, 0, 0)),
                  _const_spec(near.shape), _const_spec(far.shape), _const_spec(ovt4.shape)],
        out_specs=pl.BlockSpec((1, QT, NSA_W), lambda b, i: (b, i, 0)),
        out_shape=jax.ShapeDtypeStruct((bsz, t_len, NSA_W), BF16),
        scratch_shapes=[pltpu.VMEM((n_sel, QT), F32), pltpu.VMEM((N_KV, n_sel, QT), F32),
                        pltpu.VMEM((N_KV, 1, GQA_R * QT), F32), pltpu.VMEM((N_KV, VT_ROWS, GQA_R * QT), F32),
                        pltpu.VMEM((N_KV, HEAD_DIM, GQA_R * QT), F32), pltpu.VMEM((N_KV, HEAD_DIM, GQA_R * QT), F32),
                        pltpu.VMEM((NSA_W, QT), F32)],
        compiler_params=_cparams("parallel", "arbitrary"),
        name="prompt_attn",
    )(q, gt, ck, cvt, ksk, ksvt, kwk, kwvt, biasc, near, far, ovt4)


def _oproj_kernel(x_ref, a_ref, b_ref, woa_ref, wob_ref, g_ref, wq_ref, x1_ref, q_ref):
    x1 = (x_ref[...] + jnp.dot(a_ref[...], woa_ref[...], preferred_element_type=F32)
          + jnp.dot(b_ref[...], wob_ref[...], preferred_element_type=F32))
    x1_ref[...] = x1
    h = _rms(x1, g_ref[...]).astype(BF16)
    q_ref[...] = jnp.dot(h, wq_ref[...], preferred_element_type=F32).astype(BF16)


def _oproj(x2d, a, b, w_oa, w_ob, g, w_xq):
    n = x2d.shape[0]
    row = lambda c: pl.BlockSpec((ROW_TILE, c), lambda i: (i, 0))
    hq = MEM_HEADS * MEM_HD
    return pl.pallas_call(
        _oproj_kernel,
        grid=(n // ROW_TILE,),
        in_specs=[row(D_MODEL), row(SGU_W), row(NSA_W), _const_spec(w_oa.shape), _const_spec(w_ob.shape),
                  _const_spec((1, D_MODEL)), _const_spec(w_xq.shape)],
        out_specs=[row(D_MODEL), row(hq)],
        out_shape=[jax.ShapeDtypeStruct((n, D_MODEL), F32), jax.ShapeDtypeStruct((n, hq), BF16)],
        compiler_params=_cparams("parallel"),
        name="oproj_xq",
    )(x2d, a, b, w_oa, w_ob, g, w_xq)


def _memkv_kernel(x_ref, g_ref, w_ref, o_ref, ob_ref):
    h = _rms(x_ref[...], g_ref[...]).astype(BF16)
    o = jnp.dot(h, w_ref[...], preferred_element_type=F32)
    o_ref[...] = o
    ob_ref[...] = o.astype(BF16)


def _memkv(mem2d, g, w):
    n = mem2d.shape[0]
    c = w.shape[1]
    row = lambda cc: pl.BlockSpec((ROW_TILE, cc), lambda i: (i, 0))
    return pl.pallas_call(
        _memkv_kernel,
        grid=(n // ROW_TILE,),
        in_specs=[row(D_MODEL), _const_spec((1, D_MODEL)), _const_spec(w.shape)],
        out_specs=[row(c), row(c)],
        out_shape=[jax.ShapeDtypeStruct((n, c), F32), jax.ShapeDtypeStruct((n, c), BF16)],
        compiler_params=_cparams("parallel"),
        name="memkv_proj",
    )(mem2d, g, w)


def _softmax_rows(s):
    e = jnp.exp(s - jnp.max(s, axis=-1, keepdims=True))
    return e / jnp.sum(e, axis=-1, keepdims=True)


def _xattn_tail(x1, o, wxo_ref, g_ref, wr_ref, br_ref, x2_ref, h_ref, te_ref, tg_ref):
    x2 = x1 + jnp.dot(o.astype(BF16), wxo_ref[...], preferred_element_type=F32)
    x2_ref[...] = x2
    hb = _rms(x2, g_ref[...]).astype(BF16)
    h_ref[...] = hb
    lt = lax.dot_general(wr_ref[...], hb, (((1,), (1,)), ((), ())), preferred_element_type=F32) + br_ref[...]
    eidx = lax.broadcasted_iota(jnp.int32, lt.shape, 0)
    tops, idxs = [], []
    for _ in range(TOP_K):
        m = jnp.max(lt, axis=0, keepdims=True)
        idx = jnp.min(jnp.where(lt == m, eidx, N_EXPERTS), axis=0, keepdims=True)
        tops.append(m)
        idxs.append(idx)
        lt = jnp.where(eidx == idx, -jnp.inf, lt)
    es = [jnp.exp(t - tops[0]) for t in tops]
    den = es[0] + es[1] + es[2] + es[3]
    te_ref[...] = jnp.concatenate(idxs, axis=0)
    tg_ref[...] = jnp.concatenate([e / den for e in es], axis=0)


def _xattn_prompt_kernel(x1_ref, q_ref, kv_ref, wxo_ref, g_ref, wr_ref, br_ref, x2_ref, h_ref, te_ref, tg_ref):
    outs = []
    for hh in range(MEM_HEADS):
        qh = q_ref[:, hh * MEM_HD:(hh + 1) * MEM_HD]
        kh = kv_ref[:, hh * MEM_HD:(hh + 1) * MEM_HD]
        vh = kv_ref[:, (MEM_HEADS + hh) * MEM_HD:(MEM_HEADS + hh + 1) * MEM_HD]
        s = lax.dot_general(qh, kh, (((1,), (1,)), ((), ())), preferred_element_type=F32) * (MEM_HD ** -0.5)
        outs.append(jnp.dot(_softmax_rows(s).astype(BF16), vh, preferred_element_type=F32))
    o = jnp.concatenate(outs, axis=1)
    _xattn_tail(x1_ref[...], o, wxo_ref, g_ref, wr_ref, br_ref, x2_ref, h_ref, te_ref, tg_ref)


def _xattn_sample_kernel(x1_ref, q_ref, kv_ref, wxo_ref, g_ref, wr_ref, br_ref, x2_ref, h_ref, te_ref, tg_ref,
                         o_scr, *, t_len):
    nb = q_ref.shape[0] // t_len
    qf = q_ref[...].astype(F32)
    for bb in range(nb):
        for hh in range(MEM_HEADS):
            qh = qf[bb * t_len:(bb + 1) * t_len, hh * MEM_HD:(hh + 1) * MEM_HD].astype(BF16)
            kh = kv_ref[bb, pl.ds(hh, MEM_LEN, stride=2 * MEM_HEADS), :].astype(BF16)
            vh = kv_ref[bb, pl.ds(MEM_HEADS + hh, MEM_LEN, stride=2 * MEM_HEADS), :].astype(BF16)
            s = lax.dot_general(qh, kh, (((1,), (1,)), ((), ())), preferred_element_type=F32) * (MEM_HD ** -0.5)
            o_scr[bb * t_len:(bb + 1) * t_len, hh * MEM_HD:(hh + 1) * MEM_HD] = jnp.dot(
                _softmax_rows(s).astype(BF16), vh, preferred_element_type=F32)
    _xattn_tail(x1_ref[...], o_scr[...], wxo_ref, g_ref, wr_ref, br_ref, x2_ref, h_ref, te_ref, tg_ref)


XS_BATCH = 16


def _xattn(x1, q, kv, w_xo, g, w_rt, b_r, t_len, prompt):
    n = x1.shape[0]
    hq = MEM_HEADS * MEM_HD
    if prompt:
        rows = ROW_TILE
        per_b = t_len // rows
        kv_spec = pl.BlockSpec((MEM_LEN, 2 * hq), lambda i: (i // per_b, 0))
        kern = _xattn_prompt_kernel
        scratch = []
    else:
        rows = XS_BATCH * t_len
        kv_spec = pl.BlockSpec((XS_BATCH, MEM_LEN * 2 * MEM_HEADS, MEM_HD), lambda i: (i, 0, 0))
        kern = functools.partial(_xattn_sample_kernel, t_len=t_len)
        scratch = [pltpu.VMEM((rows, hq), F32)]
    row = lambda c: pl.BlockSpec((rows, c), lambda i: (i, 0))
    col = pl.BlockSpec((TOP_K, rows), lambda i: (0, i))
    return pl.pallas_call(
        kern,
        grid=(n // rows,),
        in_specs=[row(D_MODEL), row(hq), kv_spec, _const_spec(w_xo.shape), _const_spec((1, D_MODEL)),
                  _const_spec(w_rt.shape), _const_spec(b_r.shape)],
        out_specs=[row(D_MODEL), row(D_MODEL), col, col],
        out_shape=[jax.ShapeDtypeStruct((n, D_MODEL), F32), jax.ShapeDtypeStruct((n, D_MODEL), BF16),
                   jax.ShapeDtypeStruct((TOP_K, n), jnp.int32), jax.ShapeDtypeStruct((TOP_K, n), F32)],
        scratch_shapes=scratch,
        compiler_params=_cparams("parallel"),
        name="xattn_router_prompt" if prompt else "xattn_router_sample",
    )(x1, q, kv, w_xo, g, w_rt, b_r)


MOE_BLOCK = 512


def _expert_kernel(blk_ref, exp_ref, lo_ref, hi_ref, x_ref, wgu_ref, bgu_ref, wd_ref, bd_ref, y_ref,
                   wgu_b, wd_b):
    w = pl.program_id(0)
    lo, hi, row0 = lo_ref[w], hi_ref[w], blk_ref[w] * MOE_BLOCK

    @pl.when((w == 0) | (exp_ref[w] != exp_ref[jnp.maximum(w - 1, 0)]))
    def _():
        wgu_b[...] = wgu_ref[0].astype(BF16)
        wd_b[...] = wd_ref[0].astype(BF16)

    @pl.when(hi > lo)
    def _():
        gu = jnp.dot(x_ref[...], wgu_b[...], preferred_element_type=F32) + bgu_ref[0]
        glu = jnp.minimum(gu[:, :D_FF], SWIGLU_LIMIT)
        lin = jnp.clip(gu[:, D_FF:], -SWIGLU_LIMIT, SWIGLU_LIMIT)
        hdn = glu * jax.nn.sigmoid(SWIGLU_ALPHA * glu) * (lin + 1.0)
        y = jnp.dot(hdn.astype(BF16), wd_b[...], preferred_element_type=F32) + bd_ref[0]
        row = row0 + lax.broadcasted_iota(jnp.int32, (MOE_BLOCK, 1), 0)
        mine = (row >= lo) & (row < hi)

        @pl.when(lo == row0)
        def _():
            y_ref[...] = jnp.where(mine, y, 0.0).astype(BF16)

        @pl.when(lo != row0)
        def _():
            y_ref[...] = jnp.where(mine, y.astype(BF16), y_ref[...])


def _experts(items, xs, w_gu, b_gu, w_down, b_down):
    blk, exp, lo, hi = items
    n_rows = xs.shape[0]
    im = lambda f: (lambda w, blk, exp, lo, hi: f(blk[w], exp[w]))
    grid_spec = pltpu.PrefetchScalarGridSpec(
        num_scalar_prefetch=4,
        grid=(blk.shape[0],),
        in_specs=[pl.BlockSpec((MOE_BLOCK, D_MODEL), im(lambda b, e: (b, 0))),
                  pl.BlockSpec((1, D_MODEL, 2 * D_FF), im(lambda b, e: (e, 0, 0))),
                  pl.BlockSpec((1, 1, 2 * D_FF), im(lambda b, e: (e, 0, 0))),
                  pl.BlockSpec((1, D_FF, D_MODEL), im(lambda b, e: (e, 0, 0))),
                  pl.BlockSpec((1, 1, D_MODEL), im(lambda b, e: (e, 0, 0)))],
        out_specs=pl.BlockSpec((MOE_BLOCK, D_MODEL), im(lambda b, e: (b, 0))),
        scratch_shapes=[pltpu.VMEM((D_MODEL, 2 * D_FF), BF16), pltpu.VMEM((D_FF, D_MODEL), BF16)],
    )
    return pl.pallas_call(
        _expert_kernel,
        grid_spec=grid_spec,
        out_shape=jax.ShapeDtypeStruct((n_rows, D_MODEL), BF16),
        compiler_params=_cparams("arbitrary"),
        name="moe_experts",
    )(blk, exp, lo, hi, xs, w_gu, b_gu, w_down, b_down)


def _moe_dispatch(te):
    n = te.shape[1]
    nk = TOP_K * n
    assert nk % MOE_BLOCK == 0
    n_blk = nk // MOE_BLOCK
    iota = jnp.arange(nk, dtype=jnp.int32)
    flat_e = te.reshape(nk)
    _, order = lax.sort_key_val(flat_e, iota)
    pos = lax.sort_key_val(order, iota)[1].reshape(TOP_K, n)
    counts = jnp.sum(flat_e[:, None] == jnp.arange(N_EXPERTS)[None, :], axis=0, dtype=jnp.int32)
    start = jnp.cumsum(counts) - counts
    lo = jnp.sort(jnp.concatenate([jnp.arange(n_blk, dtype=jnp.int32) * MOE_BLOCK, start[1:]]))
    hi = jnp.concatenate([lo[1:], jnp.full((1,), nk, jnp.int32)])
    blk = jnp.minimum(lo // MOE_BLOCK, n_blk - 1)
    exp = jnp.sum(start[None, 1:] <= lo[:, None], axis=1, dtype=jnp.int32)
    return order % n, pos, (blk, exp, lo, hi)


def _final_kernel(x_ref, y_ref, g_ref, fn_ref, o_ref):
    x = x_ref[...]
    for k in range(TOP_K):
        x = x + y_ref[k].astype(F32) * g_ref[:, k:k + 1]
    o_ref[...] = _rms(x, fn_ref[...])


def _final(x2, yk, gates, fnorm):
    n = x2.shape[0]
    rows = 256
    return pl.pallas_call(
        _final_kernel,
        grid=(n // rows,),
        in_specs=[pl.BlockSpec((rows, D_MODEL), lambda i: (i, 0)),
                  pl.BlockSpec((TOP_K, rows, D_MODEL), lambda i: (0, i, 0)),
                  pl.BlockSpec((rows, TOP_K), lambda i: (i, 0)), _const_spec((1, D_MODEL))],
        out_specs=pl.BlockSpec((rows, D_MODEL), lambda i: (i, 0)),
        out_shape=jax.ShapeDtypeStruct((n, D_MODEL), F32),
        compiler_params=_cparams("parallel"),
        name="moe_combine_final_norm",
    )(x2, yk, gates, fnorm)


def _page_copy(pool_ref, pt_ref, buf_ref, sem_ref, b, p, slot, n_pages, rows):
    return pltpu.make_async_copy(pool_ref.at[pt_ref[b * n_pages + p]],
                                 buf_ref.at[slot, pl.ds(p * rows, rows)], sem_ref.at[slot])


def _fetch_pages(pool_ref, pt_ref, buf_ref, sem_ref, b, slot, n_pages, rows):
    def body(p, c):
        _page_copy(pool_ref, pt_ref, buf_ref, sem_ref, b, p, slot, n_pages, rows).start()
        return c
    lax.fori_loop(0, n_pages, body, 0)


def _wait_pages(pool_ref, pt_ref, buf_ref, sem_ref, b, slot, n_pages, rows):
    def body(p, c):
        _page_copy(pool_ref, pt_ref, buf_ref, sem_ref, b, p, slot, n_pages, rows).wait()
        return c
    lax.fori_loop(0, n_pages, body, 0)


def _compress_sample_kernel(pt_ref, pool_ref, hi_ref, w_ref, b_ref, perm_ref, ck_ref, cv_ref, buf_ref, sem_ref, tok_ref,
                            *, n_pages, page_rows):
    b = pl.program_id(0)
    nb = pl.num_programs(0)
    slot = lax.rem(b, 2)
    rows = 2 * LANES
    n_half = n_pages * page_rows // CMP_STRIDE

    @pl.when(b == 0)
    def _():
        _fetch_pages(pool_ref, pt_ref, buf_ref, sem_ref, b, slot, n_pages, rows)

    @pl.when(b + 1 < nb)
    def _():
        _fetch_pages(pool_ref, pt_ref, buf_ref, sem_ref, b + 1, 1 - slot, n_pages, rows)

    _wait_pages(pool_ref, pt_ref, buf_ref, sem_ref, b, slot, n_pages, rows)

    def regroup(p, carry):
        page = buf_ref[slot, pl.ds(pl.multiple_of(p * rows, rows), rows), :]
        _regroup_page(page, perm_ref, tok_ref, pl.multiple_of(p * PER_PAGE, PER_PAGE))
        return carry

    lax.fori_loop(0, n_pages, regroup, 0, unroll=8)
    last = lax.broadcasted_iota(jnp.int32, (n_half, LANES), 0) == n_half - 1
    for c, out in enumerate((ck_ref, cv_ref)):
        acc = _compress_halves(tok_ref, w_ref, c)
        hi_next = pltpu.roll(acc[:, LANES:], n_half - 1, 0)
        hi_next = jnp.where(last, hi_ref[0, c:c + 1, :], hi_next)
        out[0] = (acc[:, :LANES] + hi_next + b_ref[c:c + 1, :]).astype(BF16)


def _hi_new_kernel(ak_ref, av_ref, w_ref, o_ref):
    for c, a_ref in enumerate((ak_ref, av_ref)):
        o_ref[:, c, :] = jnp.dot(a_ref[...].astype(BF16), w_ref[c], preferred_element_type=F32)


def _hi_new(kck_new, kcv_new, w_cmp, n_batch, t_len):
    w_hi = w_cmp[:, :t_len, :, LANES:].reshape(2, t_len * LANES, LANES)
    ak = kck_new.reshape(n_batch, t_len * LANES)
    av = kcv_new.reshape(n_batch, t_len * LANES)
    return pl.pallas_call(
        _hi_new_kernel,
        out_shape=jax.ShapeDtypeStruct((n_batch, 2, LANES), F32),
        name="compress_new_tokens",
    )(ak, av, w_hi)


def _compress_sample(page_table_flat, pool, hi_new, w_cmp, b_cmp, n_batch, n_pages, page_rows):
    assert page_rows == PAGE
    perm = _regroup_perm()
    w_cmp = w_cmp.reshape(2, CMP_STRIDE // 2, 2 * LANES, 2 * LANES)
    n_half = n_pages * page_rows // CMP_STRIDE
    grid_spec = pltpu.PrefetchScalarGridSpec(
        num_scalar_prefetch=1,
        grid=(n_batch,),
        in_specs=[pl.BlockSpec(memory_space=pl.ANY),
                  pl.BlockSpec((1, 2, LANES), lambda b, pt: (b, 0, 0)),
                  pl.BlockSpec(w_cmp.shape, lambda b, pt: (0, 0, 0, 0)),
                  pl.BlockSpec(b_cmp.shape, lambda b, pt: (0, 0)),
                  pl.BlockSpec(perm.shape, lambda b, pt: (0, 0))],
        out_specs=[pl.BlockSpec((1, n_half, LANES), lambda b, pt: (b, 0, 0)),
                   pl.BlockSpec((1, n_half, LANES), lambda b, pt: (b, 0, 0))],
        scratch_shapes=[pltpu.VMEM((2, n_pages * 2 * LANES, page_rows), F32), pltpu.SemaphoreType.DMA((2,)),
                        pltpu.VMEM((2, CMP_STRIDE // 2, n_half, 2 * LANES), F32)],
    )
    return pl.pallas_call(
        functools.partial(_compress_sample_kernel, n_pages=n_pages, page_rows=page_rows),
        grid_spec=grid_spec,
        out_shape=[jax.ShapeDtypeStruct((n_batch, n_half, LANES), BF16)] * 2,
        compiler_params=_cparams("arbitrary"),
        name="compress_sample",
    )(page_table_flat, pool, hi_new, w_cmp, b_cmp, perm)


N_SEL_PAD = 256


def _sample_tables(rel_bias, past, t_len):
    rel_bias = rel_bias * LOG2E
    n_cmp = past // CMP_STRIDE
    n_sel = past // SEL_BLOCK + 1
    y = np.tile(np.arange(t_len), N_HEADS)[:, None]
    hh = np.repeat(np.arange(N_HEADS), t_len)[:, None]

    def tab(dist, valid):
        return _head_table(rel_bias, dist, valid, hh)

    dc = past + y - (np.arange(n_cmp)[None, :] * CMP_STRIDE + CMP_LEN - 1)
    biasc = tab(dc, dc >= 0)
    dp = past + y - np.arange(past)[None, :]
    biasp = tab(dp, dp >= 0)
    dn = y - np.arange(t_len)[None, :]
    biasn = tab(dn, dn >= 0)
    dw = WINDOW + y - np.arange(WINDOW)[None, :]
    biasw = tab(dw, dw < WINDOW)
    cs = np.arange(n_cmp)[:, None] * CMP_STRIDE
    ss = np.arange(N_SEL_PAD)[None, :] * SEL_BLOCK
    ov = ((cs < ss + SEL_BLOCK) & (cs + CMP_LEN > ss) & (np.arange(N_SEL_PAD)[None, :] < n_sel))
    ov4 = jnp.asarray(np.tile(ov.astype(np.float32), (GQA_R, 1)), BF16)
    expand = (np.arange(past)[None, :] // SEL_BLOCK == np.arange(past // SEL_BLOCK)[:, None])
    return biasc, biasp, biasn, biasw, ov4, jnp.asarray(expand.astype(np.float32), BF16)


def _sample_attn_kernel(pt_ref, q_ref, gt_ref, ck_ref, cv_ref, pool_ref, win_ref, ksk_ref, ksv_ref, kwk_ref,
                        kwv_ref, biasc_ref, biasp_ref, biasn_ref, biasw_ref, ov_ref, exp_ref, o_ref,
                        buf_ref, sem_ref, s_ref, *, n_pages, page_tokens, t_len, cur_block):
    b = pl.program_id(0)
    nb = pl.num_programs(0)
    slot = lax.rem(b, 2)
    rows = 2 * LANES
    nt = (((1,), (1,)), ((), ()))
    n_rows = N_HEADS * t_len

    @pl.when(b == 0)
    def _():
        _fetch_pages(pool_ref, pt_ref, buf_ref, sem_ref, b, slot, n_pages, rows)

    @pl.when(b + 1 < nb)
    def _():
        _fetch_pages(pool_ref, pt_ref, buf_ref, sem_ref, b + 1, 1 - slot, n_pages, rows)

    qb = q_ref[0]
    bc = biasc_ref[...]
    sc = lax.dot_general(qb, ck_ref[0], nt, preferred_element_type=F32) + bc
    e = jnp.exp2(sc - jnp.max(sc, axis=1, keepdims=True))
    e = jnp.where(bc > 0.5 * NEG, e, 0.0)
    lsum = jnp.sum(e, axis=1, keepdims=True)
    pb = (e / jnp.where(lsum > 0.0, lsum, 1.0)).astype(BF16)
    o_c = jnp.dot(pb, cv_ref[0], preferred_element_type=F32)
    p4 = jnp.concatenate(
        [jnp.concatenate([pb[(g * GQA_R + r) * t_len:(g * GQA_R + r + 1) * t_len, :] for r in range(GQA_R)], axis=1)
         for g in range(N_KV)], axis=0)
    imp = jnp.dot(p4, ov_ref[...], preferred_element_type=F32)
    lane = lax.broadcasted_iota(jnp.int32, imp.shape, 1)
    imp = jnp.where((lane == 0) | (lane == cur_block) | (lane == cur_block - 1), FORCE, imp)
    imp = jnp.where(lane > cur_block, NEG, imp)

    def rank_body(k, cnt):
        other = pltpu.roll(imp, k, 1)
        beats = (other > imp) | ((other == imp) & (lane >= k))
        return cnt + jnp.where(beats, 1.0, 0.0)

    cnt = lax.fori_loop(1, N_SEL_PAD, rank_body, jnp.zeros(imp.shape, F32), unroll=15)
    sel = jnp.where((cnt < N_SELECT) & (lane <= cur_block), 1.0, 0.0)
    sel_rows = jnp.concatenate([sel[g * t_len:(g + 1) * t_len, :] for g in range(N_KV) for _ in range(GQA_R)], axis=0)
    n_past_blocks = n_pages * page_tokens // SEL_BLOCK
    keep = jnp.dot(sel_rows[:, :n_past_blocks].astype(BF16), exp_ref[...], preferred_element_type=F32)

    def softmax_pair(s_old, s_new):
        m = jnp.maximum(jnp.max(s_old, axis=1, keepdims=True), jnp.max(s_new, axis=1, keepdims=True))
        p_old = jnp.exp2(s_old - m)
        p_new = jnp.exp2(s_new - m)
        inv = 1.0 / (jnp.sum(p_old, axis=1, keepdims=True) + jnp.sum(p_new, axis=1, keepdims=True))
        return p_old, p_new, inv

    _wait_pages(pool_ref, pt_ref, buf_ref, sem_ref, b, slot, n_pages, rows)
    span = 2 if n_pages % 2 == 0 else 1

    def pages(p, row0):
        return jnp.concatenate([buf_ref[slot, (p + j) * rows + row0:(p + j) * rows + row0 + LANES, :]
                                for j in range(span)], axis=1).astype(BF16)

    for p in range(0, n_pages, span):
        s_ref[:, p * page_tokens:(p + span) * page_tokens] = jnp.dot(qb, pages(p, 0), preferred_element_type=F32)
    s_past = jnp.where(keep > 0.5, s_ref[...] + biasp_ref[...], NEG)
    bn = biasn_ref[...]
    s_new = lax.dot_general(qb, ksk_ref[...].astype(BF16), nt, preferred_element_type=F32) + bn
    p_past, p_new, inv = softmax_pair(s_past, s_new)
    p_past = p_past.astype(BF16)
    o_s = jnp.dot(p_new.astype(BF16), ksv_ref[...].astype(BF16), preferred_element_type=F32)
    for p in range(0, n_pages, span):
        o_s = o_s + lax.dot_general(p_past[:, p * page_tokens:(p + span) * page_tokens], pages(p, LANES), nt,
                                    preferred_element_type=F32)
    o_s = o_s * inv
    s_win = jnp.dot(qb, win_ref[0, :LANES, :].astype(BF16), preferred_element_type=F32) + biasw_ref[...]
    s_wnew = lax.dot_general(qb, kwk_ref[...].astype(BF16), nt, preferred_element_type=F32) + bn
    p_win, p_wnew, inv_w = softmax_pair(s_win, s_wnew)
    o_w = (lax.dot_general(p_win.astype(BF16), win_ref[0, LANES:, :].astype(BF16), nt, preferred_element_type=F32)
           + jnp.dot(p_wnew.astype(BF16), kwv_ref[...].astype(BF16), preferred_element_type=F32)) * inv_w
    gs = jax.nn.sigmoid(gt_ref[0])
    o = gs[:, 0:1] * o_c + gs[:, 1:2] * o_s + gs[:, 2:3] * o_w
    row = lax.broadcasted_iota(jnp.int32, (n_rows, HEAD_DIM), 0)
    o_ref[0] = jnp.where(row < GQA_R * t_len, o[:, :HEAD_DIM], o[:, HEAD_DIM:])


def _sample_attn(page_table_flat, q, gt, ck, cv, pool, win, ksk, ksv, kwk, kwv, tables, n_batch, n_pages, page_tokens,
                 t_len):
    biasc, biasp, biasn, biasw, ov4, expand = tables
    past = n_pages * page_tokens
    n_rows = N_HEADS * t_len
    cur_block = past // SEL_BLOCK
    assert past % SEL_BLOCK == 0 and t_len <= SEL_BLOCK and cur_block < N_SEL_PAD
    new_spec = pl.BlockSpec((t_len, LANES), lambda b, pt: (b, 0))
    cst = lambda a: pl.BlockSpec(a.shape, lambda b, pt: (0,) * a.ndim)
    grid_spec = pltpu.PrefetchScalarGridSpec(
        num_scalar_prefetch=1,
        grid=(n_batch,),
        in_specs=[pl.BlockSpec((1, n_rows, LANES), lambda b, pt: (b, 0, 0)),
                  pl.BlockSpec((1, n_rows, 3), lambda b, pt: (b, 0, 0)),
                  pl.BlockSpec((1, past // CMP_STRIDE, LANES), lambda b, pt: (b, 0, 0)),
                  pl.BlockSpec((1, past // CMP_STRIDE, LANES), lambda b, pt: (b, 0, 0)),
                  pl.BlockSpec(memory_space=pl.ANY),
                  pl.BlockSpec((1, 2 * LANES, WINDOW), lambda b, pt: (b, 0, 0)),
                  new_spec, new_spec, new_spec, new_spec,
                  cst(biasc), cst(biasp), cst(biasn), cst(biasw), cst(ov4), cst(expand)],
        out_specs=pl.BlockSpec((1, n_rows, HEAD_DIM), lambda b, pt: (b, 0, 0)),
        scratch_shapes=[pltpu.VMEM((2, n_pages * 2 * LANES, page_tokens), F32), pltpu.SemaphoreType.DMA((2,)),
                        pltpu.VMEM((n_rows, past), F32)],
    )
    return pl.pallas_call(
        functools.partial(_sample_attn_kernel, n_pages=n_pages, page_tokens=page_tokens, t_len=t_len,
                          cur_block=cur_block),
        grid_spec=grid_spec,
        out_shape=jax.ShapeDtypeStruct((n_batch, n_rows, HEAD_DIM), F32),
        compiler_params=_cparams("arbitrary"),
        name="sample_attn",
    )(page_table_flat, q, gt, ck, cv, pool, win, ksk, ksv, kwk, kwv, biasc, biasp, biasn, biasw, ov4, expand)


def kernel(x_prompt, x_sample, mem_prompt, cache_cmp_kv, cache_slc_kv, cache_win_kv, cache_mem_kv, page_table, norm_mix, w_in, sgu_norm_g, sgu_norm_b, sgu_w, sgu_b, cmp_w, cmp_b, w_o, rel_bias, norm_x, norm_mem, w_xq, w_mem_kv, w_xo, norm_ffn, w_router, b_router, w_gu, b_gu, w_down, b_down, final_norm):
    assert norm_mix.shape[0] == 1, "single-layer trunk"
    bsz, t_p, _ = x_prompt.shape
    dbs, t_s, _ = x_sample.shape
    n_phys, page_tokens = cache_cmp_kv.shape[1], cache_cmp_kv.shape[2]
    n_pages = page_table.shape[1]
    past = n_pages * page_tokens
    wb = cache_win_kv.shape[2]
    assert wb == WINDOW and t_p >= WINDOW and t_p >= CHUNK and t_s <= CHUNK
    n_p, n_s = bsz * t_p, dbs * t_s
    row1 = lambda v: v.reshape(1, -1)

    w_all = _build_w_in(w_in[0])
    mix_p, mixb_p = _build_sgu_mix(sgu_w[0], sgu_b[0], CHUNK, CHUNK)
    mix_s, mixb_s = _build_sgu_mix(sgu_w[0], sgu_b[0], t_s, ROW_TILE)
    w_cmp = _build_cmp_w(cmp_w[0])
    b_cmp = cmp_b[0].reshape(2, LANES)
    w_oa, w_ob = w_o[0, :SGU_W].astype(BF16), w_o[0, SGU_W:].astype(BF16)
    w_xq_b, w_xo_b = w_xq[0].astype(BF16), w_xo[0].astype(BF16)
    w_rt = w_router[0].T.astype(BF16)
    b_rt = b_router[0].reshape(N_EXPERTS, 1)
    in_args = (row1(norm_mix[0]), w_all, row1(sgu_norm_g[0]), row1(sgu_norm_b[0]))
    tail_args = (w_xo_b, row1(norm_ffn[0]), w_rt, b_rt)

    xp = x_prompt.reshape(n_p, D_MODEL)
    (a_p, v_p, q_p, gt_p, kct, kst, kwt, kskb, ksvt, kwkb, kwvt) = _inproj(
        xp, *in_args, mix_p, mixb_p, t_p, True)
    ck_p, cvt_p = _compress_prompt(kct, w_cmp, b_cmp, bsz, t_p)
    b_p = _prompt_attn(q_p.reshape(bsz, t_p, N_HEADS * LANES), gt_p.reshape(bsz, t_p, LANES), ck_p, cvt_p,
                       kskb, ksvt, kwkb, kwvt, _attn_tables(rel_bias, t_p), bsz, t_p)
    x1_p, qm_p = _oproj(xp, a_p, b_p.reshape(n_p, NSA_W), w_oa, w_ob, row1(norm_x[0]), w_xq_b)
    memkv_f, memkv_b = _memkv(mem_prompt.reshape(bsz * MEM_LEN, D_MODEL), row1(norm_mem[0]), w_mem_kv[0].astype(BF16))
    x2_p, h_p, te_p, tg_p = _xattn(x1_p, qm_p, memkv_b, *tail_args, t_p, True)

    xs = x_sample.reshape(n_s, D_MODEL)
    (a_s, v_s, q_s, gt_s, kck_s, kcv_s, ksk_s, ksv_s, kwk_s, kwv_s) = _inproj(
        xs, *in_args, mix_s, mixb_s, t_s, False)
    pt_flat = page_table.reshape(-1)
    stored = lambda c: c[0].transpose(0, 2, 3, 4, 1).reshape(c.shape[1], 2 * LANES, c.shape[2])
    pool_c, pool_s = stored(cache_cmp_kv), stored(cache_slc_kv)
    win_t = cache_win_kv[0].transpose(0, 2, 3, 4, 1).reshape(dbs, 2 * LANES, wb)
    hi_new = _hi_new(kck_s, kcv_s, w_cmp, dbs, t_s)
    ck_s, cv_s = _compress_sample(pt_flat, pool_c, hi_new, w_cmp, b_cmp, dbs, n_pages, page_tokens)
    q_sb = q_s.reshape(dbs, t_s, N_HEADS, LANES).transpose(0, 2, 1, 3).reshape(dbs, N_HEADS * t_s, LANES)
    gt_sb = gt_s[:, :3 * N_HEADS].reshape(dbs, t_s, 3, N_HEADS).transpose(0, 3, 1, 2).reshape(dbs, N_HEADS * t_s, 3)
    bo_s = _sample_attn(pt_flat, q_sb, gt_sb, ck_s, cv_s, pool_s, win_t, ksk_s, ksv_s, kwk_s, kwv_s,
                        _sample_tables(rel_bias, past, t_s), dbs, n_pages, page_tokens, t_s)
    b_s = bo_s.reshape(dbs, N_HEADS, t_s, HEAD_DIM).transpose(0, 2, 1, 3).reshape(n_s, NSA_W).astype(BF16)
    x1_s, qm_s = _oproj(xs, a_s, b_s, w_oa, w_ob, row1(norm_x[0]), w_xq_b)
    memkv_s = cache_mem_kv[0].reshape(dbs, MEM_LEN * 2 * MEM_HEADS, MEM_HD)
    x2_s, h_s, te_s, tg_s = _xattn(x1_s, qm_s, memkv_s, *tail_args, t_s, False)

    row_tok, pos, items = _moe_dispatch(jnp.concatenate([te_p, te_s], axis=1))
    xs_sorted = jnp.concatenate([h_p, h_s], axis=0)[row_tok]
    y = _experts(items, xs_sorted, w_gu[0], b_gu[0][:, None, :], w_down[0], b_down[0][:, None, :])
    fn = row1(final_norm)
    y_prompt = _final(x2_p, y[pos[:, :n_p]], tg_p.T, fn).reshape(bsz, t_p, D_MODEL)
    y_sample = _final(x2_s, y[pos[:, n_p:]], tg_s.T, fn).reshape(dbs, t_s, D_MODEL)

    kv6 = lambda k, v, b, t: jnp.concatenate([k, v], axis=1).reshape(1, b, t, 2, N_KV, HEAD_DIM)
    unstored = lambda a: a.reshape(1, a.shape[0], 2, N_KV, HEAD_DIM, a.shape[2]).transpose(0, 1, 5, 2, 3, 4)
    kw_s = kv6(kwk_s, kwv_s, dbs, t_s)
    win_s = jnp.concatenate([cache_win_kv, kw_s], axis=2)[:, :, -wb:]
    return (y_prompt, y_sample, unstored(kct), unstored(kst), unstored(kwt[:, :, t_p - WINDOW:]),
            memkv_f.reshape(1, bsz, MEM_LEN, 2, MEM_HEADS, MEM_HD), v_p.reshape(1, bsz, t_p, SGU_W)[:, :, -CHUNK:],
            kv6(kck_s, kcv_s, dbs, t_s), kv6(ksk_s, ksv_s, dbs, t_s), win_s, v_s.reshape(1, dbs, t_s, SGU_W))
```

```python
import functools
import math

import numpy as np
import jax
import jax.numpy as jnp
from jax import lax
from jax.experimental import pallas as pl
from jax.experimental.pallas import tpu as pltpu

D_MODEL = 1024
SGU_W = 512
SGU_GROUPS = 4
SGU_GW = 128
CHUNK = 128
NSA_W = 512
HEAD_DIM = 64
N_HEADS = 8
N_KV = 2
GQA_R = 4
CMP_LEN = 32
CMP_STRIDE = 16
SEL_BLOCK = 64
N_SELECT = 16
WINDOW = 512
KV_W = 256
N_BUCKETS = 32
MAX_EXACT = 16
MAX_DIST = 128
MEM_LEN = 256
MEM_HEADS = 4
MEM_HD = 128
N_EXPERTS = 32
TOP_K = 4
D_FF = 1024
SWIGLU_LIMIT = 7.0
SWIGLU_ALPHA = 1.702
EPS = 1e-6
NEG = -1e30
FORCE = 1e9

LANES = 128
VMEM_LIMIT = 56 * 1024 * 1024

ROW_TILE = 512
QT = 256
KT = 256
CHAIN_HEADS = 4
RANK_UNROLL = 8
VT_ROWS = LANES + 16
LOG2E = math.log2(math.e)

F32 = jnp.float32
BF16 = jnp.bfloat16


def _cparams(*sem):
    return pltpu.CompilerParams(dimension_semantics=sem, vmem_limit_bytes=VMEM_LIMIT)


def _const_spec(shape):
    nd = len(shape)
    return pl.BlockSpec(shape, lambda *_: (0,) * nd)


def _rms(x, g):
    return x * lax.rsqrt(jnp.mean(x * x, axis=-1, keepdims=True) + EPS) * g


def _rel_bucket_np(dist):
    n = np.maximum(dist, 0)
    nf = np.maximum(n, 1).astype(np.float32)
    large = MAX_EXACT + (np.log(nf / np.float32(MAX_EXACT)) / np.float32(math.log(MAX_DIST / MAX_EXACT))
                         * np.float32(N_BUCKETS - MAX_EXACT)).astype(np.int32)
    large = np.minimum(large, N_BUCKETS - 1)
    return np.where(n < MAX_EXACT, n, large).astype(np.int32)


C_U, C_V, C_Q, C_KC, C_GT, C_END = 0, 512, 1024, 2048, 2816, 2944


def _inproj_kernel(x_ref, g_ref, w_ref, lng_ref, lnb_ref, mix_ref, mixb_ref, *out_refs, mix_block, attn_extras):
    a_ref, v_ref, q_ref, gt_ref = out_refs[:4]
    x = x_ref[...]
    h = _rms(x, g_ref[...]).astype(BF16)

    def proj(lo, hi):
        return jnp.dot(h, w_ref[:, lo:hi], preferred_element_type=F32)

    q_ref[...] = proj(C_Q, C_KC).astype(BF16)
    gt_ref[...] = proj(C_GT, C_END)
    kv = [proj(C_KC + LANES * j, C_KC + LANES * (j + 1)) for j in range(6)]
    if attn_extras:
        kvt_refs = out_refs[4:7]
        kskb_ref, ksvt_ref, kwkb_ref, kwvt_ref = out_refs[7:11]
        kskb_ref[...] = kv[2].astype(BF16)
        kwkb_ref[...] = kv[4].astype(BF16)
        for j, (kvt_ref, vtile_ref) in enumerate(zip(kvt_refs, (None, ksvt_ref, kwvt_ref))):
            kvt_ref[0, :LANES, :] = kv[2 * j].T
            vt = kv[2 * j + 1].T
            kvt_ref[0, LANES:, :] = vt
            if vtile_ref is not None:
                vtb = jnp.concatenate([vt.astype(BF16), jnp.ones((VT_ROWS - LANES, vt.shape[1]), BF16)], axis=0)
                for t in range(vt.shape[1] // KT):
                    vtile_ref[t] = vtb[:, t * KT:(t + 1) * KT]
    else:
        for r, val in zip(out_refs[4:10], kv):
            r[...] = val

    u = jax.nn.gelu(proj(C_U, C_V))
    v = jax.nn.gelu(proj(C_V, C_Q))
    mu = jnp.mean(v, axis=-1, keepdims=True)
    var = jnp.mean(jnp.square(v - mu), axis=-1, keepdims=True)
    v = (v - mu) * lax.rsqrt(var + EPS) * lng_ref[...] + lnb_ref[...]
    v_ref[...] = v
    vb = v.astype(BF16)
    rows = x.shape[0]
    for blk in range(rows // mix_block):
        r0 = blk * mix_block
        for g in range(SGU_GROUPS):
            c0 = g * SGU_GW
            mixed = jnp.dot(mix_ref[g], vb[r0:r0 + mix_block, c0:c0 + SGU_GW],
                            preferred_element_type=F32) + mixb_ref[:, c0:c0 + SGU_GW]
            a_ref[r0:r0 + mix_block, c0:c0 + SGU_GW] = (
                u[r0:r0 + mix_block, c0:c0 + SGU_GW] * mixed).astype(BF16)


def _inproj(x2d, g, w_all, ln_g, ln_b, mix, mixb, t_len, attn_extras):
    n = x2d.shape[0]
    mb = mix.shape[1]
    row = lambda c: pl.BlockSpec((ROW_TILE, c), lambda i: (i, 0))
    outs = [(SGU_W, BF16), (SGU_W, F32), (N_HEADS * LANES, BF16), (LANES, F32)]
    if not attn_extras:
        outs += [(LANES, F32)] * 6
    out_specs = [row(c) for c, _ in outs]
    out_shape = [jax.ShapeDtypeStruct((n, c), dt) for c, dt in outs]
    if attn_extras:
        per_b = t_len // ROW_TILE
        kvt_spec = pl.BlockSpec((1, 2 * LANES, ROW_TILE), lambda i: (i // per_b, 0, i % per_b))
        kvt_shape = jax.ShapeDtypeStruct((n // t_len, 2 * LANES, t_len), F32)
        tiles = ROW_TILE // KT
        vt_spec = pl.BlockSpec((tiles, VT_ROWS, KT), lambda i: (i, 0, 0))
        vt_shape = jax.ShapeDtypeStruct((n // KT, VT_ROWS, KT), BF16)
        out_specs += [kvt_spec] * 3 + [row(LANES), vt_spec, row(LANES), vt_spec]
        out_shape += [kvt_shape] * 3 + [jax.ShapeDtypeStruct((n, LANES), BF16), vt_shape,
                                        jax.ShapeDtypeStruct((n, LANES), BF16), vt_shape]
    return pl.pallas_call(
        functools.partial(_inproj_kernel, mix_block=mb, attn_extras=attn_extras),
        grid=(n // ROW_TILE,),
        in_specs=[row(D_MODEL), _const_spec((1, D_MODEL)), _const_spec(w_all.shape),
                  _const_spec((1, SGU_W)), _const_spec((1, SGU_W)), _const_spec(mix.shape),
                  _const_spec(mixb.shape)],
        out_specs=out_specs,
        out_shape=out_shape,
        compiler_params=_cparams("parallel"),
        name="inproj_sgu",
    )(x2d, g, w_all, ln_g, ln_b, mix, mixb)


def _build_w_in(w_in):
    u = w_in[:, 0:512]
    v = w_in[:, 512:1024]
    q = w_in[:, 1024:1536].reshape(D_MODEL, N_KV, GQA_R, HEAD_DIM) * (HEAD_DIM ** -0.5 * LOG2E)
    qp = jnp.zeros((D_MODEL, N_KV, GQA_R, N_KV, HEAD_DIM), F32)
    for g in range(N_KV):
        qp = qp.at[:, g, :, g, :].set(q[:, g])
    qp = qp.reshape(D_MODEL, N_HEADS * LANES)
    kv = w_in[:, 1536:2304]
    gt = jnp.pad(w_in[:, 2304:2328], ((0, 0), (0, LANES - 3 * N_HEADS)))
    return jnp.concatenate([u, v, qp, kv, gt], axis=1).astype(BF16)


def _build_sgu_mix(sgu_w, sgu_b, chunk, mix_block):
    tri = jnp.tril(jnp.ones((chunk, chunk), bool))
    w = jnp.where(tri, sgu_w[:, :chunk, :chunk], 0.0)
    reps = mix_block // chunk
    eye = jnp.eye(reps, dtype=F32)
    mix = jnp.einsum('ab,gst->gasbt', eye, w).reshape(SGU_GROUPS, mix_block, mix_block)
    b = jnp.tile(sgu_b[:, :chunk], (1, reps))
    mixb = jnp.repeat(b.T, SGU_GW, axis=1)
    return mix.astype(BF16), mixb


def _build_cmp_w(cmp_w):
    w = cmp_w.reshape(2, N_KV, 2, CMP_STRIDE, HEAD_DIM, HEAD_DIM)
    out = jnp.zeros((2, CMP_STRIDE, N_KV, HEAD_DIM, 2, N_KV, HEAD_DIM), F32)
    for g in range(N_KV):
        out = out.at[:, :, g, :, :, g, :].set(jnp.transpose(w[:, g], (0, 2, 3, 1, 4)))
    return out.reshape(2, CMP_STRIDE, LANES, 2 * LANES).astype(BF16)


PAGE = 128
PER_PAGE = PAGE // CMP_STRIDE


def _regroup_perm():
    tok = np.arange(PAGE)
    return jnp.asarray(np.arange(PAGE)[:, None] == ((tok % CMP_STRIDE) * PER_PAGE + tok // CMP_STRIDE)[None, :], BF16)


def _regroup_page(page, perm_ref, tok_ref, block0):
    t = lax.dot_general(perm_ref[...], page.astype(BF16), (((1,), (1,)), ((), ())), preferred_element_type=F32)
    for r in range(CMP_STRIDE):
        for c in range(2):
            tok_ref[c, r // 2, pl.ds(block0, PER_PAGE), (r % 2) * LANES:(r % 2 + 1) * LANES] = (
                t[r * PER_PAGE:(r + 1) * PER_PAGE, c * LANES:(c + 1) * LANES])


def _compress_halves(tok_ref, w_ref, c):
    acc = jnp.dot(tok_ref[c, 0].astype(BF16), w_ref[c, 0], preferred_element_type=F32)
    for q in range(1, CMP_STRIDE // 2):
        acc = acc + jnp.dot(tok_ref[c, q].astype(BF16), w_ref[c, q], preferred_element_type=F32)
    return acc


def _compress_kernel(kct_ref, w_ref, b_ref, perm_ref, ck_ref, cvt_ref, tok_ref, *, n_half):
    for p in range(n_half // PER_PAGE):
        _regroup_page(kct_ref[0, :, p * PAGE:(p + 1) * PAGE], perm_ref, tok_ref, p * PER_PAGE)
    for c in range(2):
        acc = _compress_halves(tok_ref, w_ref, c)
        hi_next = pltpu.roll(acc[:, LANES:], n_half - 1, 0)
        comp = acc[:, :LANES] + hi_next + b_ref[c:c + 1, :]
        if c == 0:
            ck_ref[0] = comp.astype(BF16)
        else:
            cvt_ref[0] = comp.T.astype(BF16)


def _compress_prompt(kct, w_cmp, b_cmp, bsz, t_len):
    n_half = t_len // CMP_STRIDE
    perm = _regroup_perm()
    w2 = w_cmp.reshape(2, CMP_STRIDE // 2, 2 * LANES, 2 * LANES)
    return pl.pallas_call(
        functools.partial(_compress_kernel, n_half=n_half),
        grid=(bsz,),
        in_specs=[pl.BlockSpec((1, 2 * LANES, t_len), lambda b: (b, 0, 0)),
                  _const_spec(w2.shape), _const_spec(b_cmp.shape), _const_spec(perm.shape)],
        out_specs=[pl.BlockSpec((1, n_half, LANES), lambda b: (b, 0, 0)),
                   pl.BlockSpec((1, LANES, n_half), lambda b: (b, 0, 0))],
        out_shape=[jax.ShapeDtypeStruct((bsz, n_half, LANES), BF16),
                   jax.ShapeDtypeStruct((bsz, LANES, n_half), BF16)],
        scratch_shapes=[pltpu.VMEM((2, CMP_STRIDE // 2, n_half, 2 * LANES), F32)],
        compiler_params=_cparams("parallel"),
        name="compress_prompt",
    )(kct, w2, b_cmp, perm)


def _toeplitz(rel_bias, rows, cols, off, valid_fn):
    span = rows + cols - 1
    d = np.arange(span) + off - (rows - 1)
    vec = jnp.where(valid_fn(d)[None, :], rel_bias[_rel_bucket_np(d)].T, NEG)
    flat = jnp.tile(vec, (1, rows + 1))[:, :rows * (span + 1)]
    skew = flat.reshape(N_HEADS, rows, span + 1)
    return skew[:, ::-1, :cols]


def _head_table(rel_bias, dist, valid, head):
    bucket = _rel_bucket_np(dist)
    need = valid & (bucket != N_BUCKETS - 1)
    out = jnp.broadcast_to(rel_bias[N_BUCKETS - 1][head[:, 0]][:, None], dist.shape)
    cols = np.nonzero(need.any(axis=0))[0]
    if cols.size:
        c0, c1 = int(cols[0]), int(cols[-1]) + 1
        mid = jnp.where(need[:, c0:c1], rel_bias[bucket[:, c0:c1], head], out[:, c0:c1])
        out = jnp.concatenate([out[:, :c0], mid, out[:, c1:]], axis=1)
    return jnp.where(valid, out, NEG)


def _attn_tables(rel_bias, t_len):
    rel_bias = rel_bias * LOG2E
    n_half = t_len // CMP_STRIDE
    n_sel = t_len // SEL_BLOCK
    nq = t_len // QT
    c_far = rel_bias[N_BUCKETS - 1]
    to_lanes = lambda a: a.reshape(N_KV, GQA_R, a.shape[1], QT).transpose(0, 2, 1, 3).reshape(
        N_KV, a.shape[1], GQA_R * QT)
    near = jnp.stack([
        to_lanes(_toeplitz(rel_bias, KT, QT, 0, lambda d: d >= 0)),
        to_lanes(_toeplitz(rel_bias, KT, QT, KT, lambda d: d >= 0)),
        to_lanes(_toeplitz(rel_bias, KT, QT, 2 * KT, lambda d: d < WINDOW))])
    far = jnp.repeat(c_far.reshape(N_KV, 1, GQA_R), QT, axis=2).reshape(N_KV, 1, GQA_R * QT)
    per_q = QT // CMP_STRIDE
    band_rows = 2 * per_q
    lead = per_q // 2
    off = lead * CMP_STRIDE - (CMP_LEN - 1)
    band = _toeplitz(rel_bias, band_rows * CMP_STRIDE, QT, off, lambda d: d >= 0)[:, ::CMP_STRIDE]
    assert _rel_bucket_np(np.array([off + CMP_STRIDE])).item() == N_BUCKETS - 1
    tiles = []
    for i in range(nq):
        lo = i * per_q - lead
        parts = []
        if lo > 0:
            parts.append(jnp.broadcast_to(c_far[:, None, None], (N_HEADS, lo, QT)))
        b0, b1 = max(0, -lo), min(band_rows, n_half - lo)
        parts.append(band[:, b0:b1])
        rest = n_half - (lo + b1)
        if rest > 0:
            parts.append(jnp.full((N_HEADS, rest, QT), NEG, F32))
        tiles.append(to_lanes(jnp.concatenate(parts, axis=1)))
    biasc = jnp.stack(tiles)
    cs = np.arange(n_half - 1)[None, :] * CMP_STRIDE
    ss = np.arange(n_sel)[:, None] * SEL_BLOCK
    ov = np.zeros((n_sel, n_half), np.float32)
    ov[:, :n_half - 1] = (cs < ss + SEL_BLOCK) & (cs + CMP_LEN > ss)
    ovt4 = jnp.asarray(np.tile(ov, (1, GQA_R)), BF16)
    return near, far, biasc, ovt4


def _prompt_attn_kernel(q_ref, gt_ref, ck_ref, cvt_ref, ksk_ref, ksvt_ref, kwk_ref, kwvt_ref,
                        biasc_ref, near_ref, far_ref, ovt_ref, o_ref,
                        imp_ref, selneg_ref, m_ref, acc_ref, oc_ref, os_ref, outt_ref, *, n_sel):
    i = pl.program_id(1)
    nt = (((1,), (1,)), ((), ()))
    groups = range(N_KV)
    own = lambda g: slice(g * HEAD_DIM, (g + 1) * HEAD_DIM)
    jj = lax.broadcasted_iota(jnp.int32, (n_sel, QT), 0)
    tt = i * QT + lax.broadcasted_iota(jnp.int32, (n_sel, QT), 1)
    cur = lax.shift_right_logical(tt, 6)
    forced = (jj == 0) | (jj == cur) | (jj == cur - 1)
    future = jj > cur
    qs = [jnp.concatenate([q_ref[0, :, (g * GQA_R + r) * LANES:(g * GQA_R + r + 1) * LANES]
                           for r in range(GQA_R)], axis=0) for g in groups]

    near = lambda idx: (lambda g, cols: near_ref[idx, g, :, cols])
    far = lambda g, cols: far_ref[g, :, cols]

    def online_step(k_ref, vt_ref, kt, bias_of, mask_of, shift_of, first):
        kk = k_ref[pl.ds(pl.multiple_of(kt * KT, KT), KT), :]
        vt = vt_ref[kt]
        for g in groups:
            mask = None if mask_of is None else jnp.concatenate([mask_of(g)] * CHAIN_HEADS, axis=1)
            for c0 in range(0, GQA_R, CHAIN_HEADS):
                cols = slice(c0 * QT, (c0 + CHAIN_HEADS) * QT)
                s = lax.dot_general(kk, qs[g][cols], nt, preferred_element_type=F32)
                if bias_of is not None:
                    s = s + bias_of(g, cols)
                if mask is not None:
                    s = s + mask
                mt = jnp.max(s, axis=0, keepdims=True)
                shift = None if shift_of is None else shift_of(g, cols)
                if shift is not None:
                    mt = mt + shift
                if first:
                    m_new = mt
                else:
                    m_old = m_ref[g, :, cols]
                    m_new = jnp.maximum(m_old, mt)
                    alpha = jnp.exp2(m_old - m_new)
                p = jnp.exp2(s - (m_new if shift is None else m_new - shift))
                pv = jnp.dot(vt, p.astype(BF16), preferred_element_type=F32)
                if first:
                    acc_ref[g, :, cols] = pv
                else:
                    acc_ref[g, :, cols] = alpha * acc_ref[g, :, cols] + pv
                m_ref[g, :, cols] = m_new

    def normalized(g):
        return acc_ref[g, own(g), :] / acc_ref[g, LANES:LANES + 1, :]

    def sel_mask(g, kt):
        rows = [jnp.broadcast_to(selneg_ref[g, pl.ds(kt * (KT // SEL_BLOCK) + b, 1), :], (SEL_BLOCK, QT))
                for b in range(KT // SEL_BLOCK)]
        return jnp.concatenate(rows, axis=0)

    for g in groups:
        bc = biasc_ref[0, g]
        sc = lax.dot_general(ck_ref[0], qs[g], nt, preferred_element_type=F32) + bc
        e = jnp.exp2(sc - jnp.max(sc, axis=0, keepdims=True))
        e = jnp.where(bc > 0.5 * NEG, e, 0.0)
        lsum = jnp.sum(e, axis=0, keepdims=True)
        pb = (e / jnp.where(lsum > 0.0, lsum, 1.0)).astype(BF16)
        oc_ref[g] = jnp.dot(cvt_ref[0], pb, preferred_element_type=F32)[own(g)]
        p4 = jnp.concatenate([pb[:, r * QT:(r + 1) * QT] for r in range(GQA_R)], axis=0)
        imp = jnp.dot(ovt_ref[...], p4, preferred_element_type=F32)
        imp = jnp.where(forced, FORCE, imp)
        imp = jnp.where(future, NEG, imp)
        imp_ref[...] = imp

        def rank_body(i8, cnt, imp=imp):
            for u in range(RANK_UNROLL):
                ii = i8 * RANK_UNROLL + u
                row = imp_ref[pl.ds(ii, 1), :]
                beats = (row > imp) | ((row == imp) & (jj > ii))
                cnt = cnt + jnp.where(beats, 1.0, 0.0)
            return cnt

        live_blocks = (i + 1) * (QT // SEL_BLOCK)
        cnt = lax.fori_loop(0, (live_blocks + RANK_UNROLL - 1) // RANK_UNROLL, rank_body,
                            jnp.zeros((n_sel, QT), F32))
        selneg_ref[g] = jnp.where((cnt < N_SELECT) & jnp.logical_not(future), 0.0, NEG)

    online_step(ksk_ref, ksvt_ref, i, near(0), lambda g: sel_mask(g, i), None, True)

    @pl.when(i >= 1)
    def _():
        online_step(ksk_ref, ksvt_ref, i - 1, near(1), lambda g: sel_mask(g, i - 1), None, False)

    def far_body(kt, carry):
        online_step(ksk_ref, ksvt_ref, kt, None, lambda g: sel_mask(g, kt), far, False)
        return carry

    lax.fori_loop(0, jnp.maximum(i - 1, 0), far_body, 0)
    for g in groups:
        os_ref[g] = normalized(g)
    online_step(kwk_ref, kwvt_ref, i, near(0), None, None, True)

    @pl.when(i >= 1)
    def _():
        online_step(kwk_ref, kwvt_ref, i - 1, near(1), None, None, False)

    @pl.when(i >= 2)
    def _():
        online_step(kwk_ref, kwvt_ref, i - 2, near(2), None, None, False)

    gsig = jax.nn.sigmoid(gt_ref[0]).T
    for g in groups:
        o_c, o_s, o_w = oc_ref[g], os_ref[g], normalized(g)
        for r in range(GQA_R):
            h = g * GQA_R + r
            cols = slice(r * QT, (r + 1) * QT)
            outt_ref[h * HEAD_DIM:(h + 1) * HEAD_DIM, :] = (
                gsig[h:h + 1, :] * o_c[:, cols]
                + gsig[N_HEADS + h:N_HEADS + h + 1, :] * o_s[:, cols]
                + gsig[2 * N_HEADS + h:2 * N_HEADS + h + 1, :] * o_w[:, cols])
    o_ref[0] = outt_ref[...].T.astype(BF16)


def _prompt_attn(q, gt, ck, cvt, ksk, ksvt, kwk, kwvt, tables, bsz, t_len):
    near, far, biasc, ovt4 = tables
    n_half = t_len // CMP_STRIDE
    n_sel = t_len // SEL_BLOCK
    nkt = t_len // KT
    per_b2 = lambda c: pl.BlockSpec((t_len, c), lambda b, i: (b, 0))
    vt_spec = pl.BlockSpec((nkt, VT_ROWS, KT), lambda b, i: (b, 0, 0))
    return pl.pallas_call(
        functools.partial(_prompt_attn_kernel, n_sel=n_sel),
        grid=(bsz, t_len // QT),
        in_specs=[pl.BlockSpec((1, QT, N_HEADS * LANES), lambda b, i: (b, i, 0)),
                  pl.BlockSpec((1, QT, LANES), lambda b, i: (b, i, 0)),
                  pl.BlockSpec((1, n_half, LANES), lambda b, i: (b, 0, 0)),
                  pl.BlockSpec((1, LANES, n_half), lambda b, i: (b, 0, 0)),
                  per_b2(LANES), vt_spec, per_b2(LANES), vt_spec,
                  pl.BlockSpec((1, N_KV, n_half, GQA_R * QT), lambda b, i: (i, 0, 0, 0)),
                  _const_spec(near.shape), _const_spec(far.shape), _const_spec(ovt4.shape)],
        out_specs=pl.BlockSpec((1, QT, NSA_W), lambda b, i: (b, i, 0)),
        out_shape=jax.ShapeDtypeStruct((bsz, t_len, NSA_W), BF16),
        scratch_shapes=[pltpu.VMEM((n_sel, QT), F32), pltpu.VMEM((N_KV, n_sel, QT), F32),
                        pltpu.VMEM((N_KV, 1, GQA_R * QT), F32), pltpu.VMEM((N_KV, VT_ROWS, GQA_R * QT), F32),
                        pltpu.VMEM((N_KV, HEAD_DIM, GQA_R * QT), F32), pltpu.VMEM((N_KV, HEAD_DIM, GQA_R * QT), F32),
                        pltpu.VMEM((NSA_W, QT), F32)],
        compiler_params=_cparams("parallel", "arbitrary"),
        name="prompt_attn",
    )(q, gt, ck, cvt, ksk, ksvt, kwk, kwvt, biasc, near, far, ovt4)


def _oproj_kernel(x_ref, a_ref, b_ref, woa_ref, wob_ref, g_ref, wq_ref, x1_ref, q_ref):
    x1 = (x_ref[...] + jnp.dot(a_ref[...], woa_ref[...], preferred_element_type=F32)
          + jnp.dot(b_ref[...], wob_ref[...], preferred_element_type=F32))
    x1_ref[...] = x1
    h = _rms(x1, g_ref[...]).astype(BF16)
    q_ref[...] = jnp.dot(h, wq_ref[...], preferred_element_type=F32).astype(BF16)


def _oproj(x2d, a, b, w_oa, w_ob, g, w_xq):
    n = x2d.shape[0]
    row = lambda c: pl.BlockSpec((ROW_TILE, c), lambda i: (i, 0))
    hq = MEM_HEADS * MEM_HD
    return pl.pallas_call(
        _oproj_kernel,
        grid=(n // ROW_TILE,),
        in_specs=[row(D_MODEL), row(SGU_W), row(NSA_W), _const_spec(w_oa.shape), _const_spec(w_ob.shape),
                  _const_spec((1, D_MODEL)), _const_spec(w_xq.shape)],
        out_specs=[row(D_MODEL), row(hq)],
        out_shape=[jax.ShapeDtypeStruct((n, D_MODEL), F32), jax.ShapeDtypeStruct((n, hq), BF16)],
        compiler_params=_cparams("parallel"),
        name="oproj_xq",
    )(x2d, a, b, w_oa, w_ob, g, w_xq)


def _memkv_kernel(x_ref, g_ref, w_ref, o_ref, ob_ref):
    h = _rms(x_ref[...], g_ref[...]).astype(BF16)
    o = jnp.dot(h, w_ref[...], preferred_element_type=F32)
    o_ref[...] = o
    ob_ref[...] = o.astype(BF16)


def _memkv(mem2d, g, w):
    n = mem2d.shape[0]
    c = w.shape[1]
    row = lambda cc: pl.BlockSpec((ROW_TILE, cc), lambda i: (i, 0))
    return pl.pallas_call(
        _memkv_kernel,
        grid=(n // ROW_TILE,),
        in_specs=[row(D_MODEL), _const_spec((1, D_MODEL)), _const_spec(w.shape)],
        out_specs=[row(c), row(c)],
        out_shape=[jax.ShapeDtypeStruct((n, c), F32), jax.ShapeDtypeStruct((n, c), BF16)],
        compiler_params=_cparams("parallel"),
        name="memkv_proj",
    )(mem2d, g, w)


def _softmax_rows(s):
    e = jnp.exp(s - jnp.max(s, axis=-1, keepdims=True))
    return e / jnp.sum(e, axis=-1, keepdims=True)


def _xattn_tail(x1, o, wxo_ref, g_ref, wr_ref, br_ref, x2_ref, h_ref, te_ref, tg_ref):
    x2 = x1 + jnp.dot(o.astype(BF16), wxo_ref[...], preferred_element_type=F32)
    x2_ref[...] = x2
    hb = _rms(x2, g_ref[...]).astype(BF16)
    h_ref[...] = hb
    lt = lax.dot_general(wr_ref[...], hb, (((1,), (1,)), ((), ())), preferred_element_type=F32) + br_ref[...]
    eidx = lax.broadcasted_iota(jnp.int32, lt.shape, 0)
    tops, idxs = [], []
    for _ in range(TOP_K):
        m = jnp.max(lt, axis=0, keepdims=True)
        idx = jnp.min(jnp.where(lt == m, eidx, N_EXPERTS), axis=0, keepdims=True)
        tops.append(m)
        idxs.append(idx)
        lt = jnp.where(eidx == idx, -jnp.inf, lt)
    es = [jnp.exp(t - tops[0]) for t in tops]
    den = es[0] + es[1] + es[2] + es[3]
    te_ref[...] = jnp.concatenate(idxs, axis=0)
    tg_ref[...] = jnp.concatenate([e / den for e in es], axis=0)


def _xattn_prompt_kernel(x1_ref, q_ref, kv_ref, wxo_ref, g_ref, wr_ref, br_ref, x2_ref, h_ref, te_ref, tg_ref):
    outs = []
    for hh in range(MEM_HEADS):
        qh = q_ref[:, hh * MEM_HD:(hh + 1) * MEM_HD]
        kh = kv_ref[:, hh * MEM_HD:(hh + 1) * MEM_HD]
        vh = kv_ref[:, (MEM_HEADS + hh) * MEM_HD:(MEM_HEADS + hh + 1) * MEM_HD]
        s = lax.dot_general(qh, kh, (((1,), (1,)), ((), ())), preferred_element_type=F32) * (MEM_HD ** -0.5)
        outs.append(jnp.dot(_softmax_rows(s).astype(BF16), vh, preferred_element_type=F32))
    o = jnp.concatenate(outs, axis=1)
    _xattn_tail(x1_ref[...], o, wxo_ref, g_ref, wr_ref, br_ref, x2_ref, h_ref, te_ref, tg_ref)


def _xattn_sample_kernel(x1_ref, q_ref, kv_ref, wxo_ref, g_ref, wr_ref, br_ref, x2_ref, h_ref, te_ref, tg_ref,
                         o_scr, *, t_len):
    nb = q_ref.shape[0] // t_len
    qf = q_ref[...].astype(F32)
    for bb in range(nb):
        for hh in range(MEM_HEADS):
            qh = qf[bb * t_len:(bb + 1) * t_len, hh * MEM_HD:(hh + 1) * MEM_HD].astype(BF16)
            kh = kv_ref[bb, pl.ds(hh, MEM_LEN, stride=2 * MEM_HEADS), :].astype(BF16)
            vh = kv_ref[bb, pl.ds(MEM_HEADS + hh, MEM_LEN, stride=2 * MEM_HEADS), :].astype(BF16)
            s = lax.dot_general(qh, kh, (((1,), (1,)), ((), ())), preferred_element_type=F32) * (MEM_HD ** -0.5)
            o_scr[bb * t_len:(bb + 1) * t_len, hh * MEM_HD:(hh + 1) * MEM_HD] = jnp.dot(
                _softmax_rows(s).astype(BF16), vh, preferred_element_type=F32)
    _xattn_tail(x1_ref[...], o_scr[...], wxo_ref, g_ref, wr_ref, br_ref, x2_ref, h_ref, te_ref, tg_ref)


XS_BATCH = 16


def _xattn(x1, q, kv, w_xo, g, w_rt, b_r, t_len, prompt):
    n = x1.shape[0]
    hq = MEM_HEADS * MEM_HD
    if prompt:
        rows = ROW_TILE
        per_b = t_len // rows
        kv_spec = pl.BlockSpec((MEM_LEN, 2 * hq), lambda i: (i // per_b, 0))
        kern = _xattn_prompt_kernel
        scratch = []
    else:
        rows = XS_BATCH * t_len
        kv_spec = pl.BlockSpec((XS_BATCH, MEM_LEN * 2 * MEM_HEADS, MEM_HD), lambda i: (i, 0, 0))
        kern = functools.partial(_xattn_sample_kernel, t_len=t_len)
        scratch = [pltpu.VMEM((rows, hq), F32)]
    row = lambda c: pl.BlockSpec((rows, c), lambda i: (i, 0))
    col = pl.BlockSpec((TOP_K, rows), lambda i: (0, i))
    return pl.pallas_call(
        kern,
        grid=(n // rows,),
        in_specs=[row(D_MODEL), row(hq), kv_spec, _const_spec(w_xo.shape), _const_spec((1, D_MODEL)),
                  _const_spec(w_rt.shape), _const_spec(b_r.shape)],
        out_specs=[row(D_MODEL), row(D_MODEL), col, col],
        out_shape=[jax.ShapeDtypeStruct((n, D_MODEL), F32), jax.ShapeDtypeStruct((n, D_MODEL), BF16),
                   jax.ShapeDtypeStruct((TOP_K, n), jnp.int32), jax.ShapeDtypeStruct((TOP_K, n), F32)],
        scratch_shapes=scratch,
        compiler_params=_cparams("parallel"),
        name="xattn_router_prompt" if prompt else "xattn_router_sample",
    )(x1, q, kv, w_xo, g, w_rt, b_r)


MOE_BLOCK = 512


def _expert_kernel(blk_ref, exp_ref, lo_ref, hi_ref, x_ref, wgu_ref, bgu_ref, wd_ref, bd_ref, y_ref,
                   wgu_b, wd_b):
    w = pl.program_id(0)
    lo, hi, row0 = lo_ref[w], hi_ref[w], blk_ref[w] * MOE_BLOCK

    @pl.when((w == 0) | (exp_ref[w] != exp_ref[jnp.maximum(w - 1, 0)]))
    def _():
        wgu_b[...] = wgu_ref[0].astype(BF16)
        wd_b[...] = wd_ref[0].astype(BF16)

    @pl.when(hi > lo)
    def _():
        gu = jnp.dot(x_ref[...], wgu_b[...], preferred_element_type=F32) + bgu_ref[0]
        glu = jnp.minimum(gu[:, :D_FF], SWIGLU_LIMIT)
        lin = jnp.clip(gu[:, D_FF:], -SWIGLU_LIMIT, SWIGLU_LIMIT)
        hdn = glu * jax.nn.sigmoid(SWIGLU_ALPHA * glu) * (lin + 1.0)
        y = jnp.dot(hdn.astype(BF16), wd_b[...], preferred_element_type=F32) + bd_ref[0]
        row = row0 + lax.broadcasted_iota(jnp.int32, (MOE_BLOCK, 1), 0)
        mine = (row >= lo) & (row < hi)

        @pl.when(lo == row0)
        def _():
            y_ref[...] = jnp.where(mine, y, 0.0).astype(BF16)

        @pl.when(lo != row0)
        def _():
            y_ref[...] = jnp.where(mine, y.astype(BF16), y_ref[...])


def _experts(items, xs, w_gu, b_gu, w_down, b_down):
    blk, exp, lo, hi = items
    n_rows = xs.shape[0]
    im = lambda f: (lambda w, blk, exp, lo, hi: f(blk[w], exp[w]))
    grid_spec = pltpu.PrefetchScalarGridSpec(
        num_scalar_prefetch=4,
        grid=(blk.shape[0],),
        in_specs=[pl.BlockSpec((MOE_BLOCK, D_MODEL), im(lambda b, e: (b, 0))),
                  pl.BlockSpec((1, D_MODEL, 2 * D_FF), im(lambda b, e: (e, 0, 0))),
                  pl.BlockSpec((1, 1, 2 * D_FF), im(lambda b, e: (e, 0, 0))),
                  pl.BlockSpec((1, D_FF, D_MODEL), im(lambda b, e: (e, 0, 0))),
                  pl.BlockSpec((1, 1, D_MODEL), im(lambda b, e: (e, 0, 0)))],
        out_specs=pl.BlockSpec((MOE_BLOCK, D_MODEL), im(lambda b, e: (b, 0))),
        scratch_shapes=[pltpu.VMEM((D_MODEL, 2 * D_FF), BF16), pltpu.VMEM((D_FF, D_MODEL), BF16)],
    )
    return pl.pallas_call(
        _expert_kernel,
        grid_spec=grid_spec,
        out_shape=jax.ShapeDtypeStruct((n_rows, D_MODEL), BF16),
        compiler_params=_cparams("arbitrary"),
        name="moe_experts",
    )(blk, exp, lo, hi, xs, w_gu, b_gu, w_down, b_down)


def _moe_dispatch(te):
    n = te.shape[1]
    nk = TOP_K * n
    assert nk % MOE_BLOCK == 0
    n_blk = nk // MOE_BLOCK
    iota = jnp.arange(nk, dtype=jnp.int32)
    flat_e = te.reshape(nk)
    _, order = lax.sort_key_val(flat_e, iota)
    pos = lax.sort_key_val(order, iota)[1].reshape(TOP_K, n)
    counts = jnp.sum(flat_e[:, None] == jnp.arange(N_EXPERTS)[None, :], axis=0, dtype=jnp.int32)
    start = jnp.cumsum(counts) - counts
    lo = jnp.sort(jnp.concatenate([jnp.arange(n_blk, dtype=jnp.int32) * MOE_BLOCK, start[1:]]))
    hi = jnp.concatenate([lo[1:], jnp.full((1,), nk, jnp.int32)])
    blk = jnp.minimum(lo // MOE_BLOCK, n_blk - 1)
    exp = jnp.sum(start[None, 1:] <= lo[:, None], axis=1, dtype=jnp.int32)
    return order % n, pos, (blk, exp, lo, hi)


def _final_kernel(x_ref, y_ref, g_ref, fn_ref, o_ref):
    x = x_ref[...]
    for k in range(TOP_K):
        x = x + y_ref[k].astype(F32) * g_ref[:, k:k + 1]
    o_ref[...] = _rms(x, fn_ref[...])


def _final(x2, yk, gates, fnorm):
    n = x2.shape[0]
    rows = 256
    return pl.pallas_call(
        _final_kernel,
        grid=(n // rows,),
        in_specs=[pl.BlockSpec((rows, D_MODEL), lambda i: (i, 0)),
                  pl.BlockSpec((TOP_K, rows, D_MODEL), lambda i: (0, i, 0)),
                  pl.BlockSpec((rows, TOP_K), lambda i: (i, 0)), _const_spec((1, D_MODEL))],
        out_specs=pl.BlockSpec((rows, D_MODEL), lambda i: (i, 0)),
        out_shape=jax.ShapeDtypeStruct((n, D_MODEL), F32),
        compiler_params=_cparams("parallel"),
        name="moe_combine_final_norm",
    )(x2, yk, gates, fnorm)


def _page_copy(pool_ref, pt_ref, buf_ref, sem_ref, b, p, slot, n_pages, rows):
    return pltpu.make_async_copy(pool_ref.at[pt_ref[b * n_pages + p]],
                                 buf_ref.at[slot, pl.ds(p * rows, rows)], sem_ref.at[slot])


def _fetch_pages(pool_ref, pt_ref, buf_ref, sem_ref, b, slot, n_pages, rows):
    def body(p, c):
        _page_copy(pool_ref, pt_ref, buf_ref, sem_ref, b, p, slot, n_pages, rows).start()
        return c
    lax.fori_loop(0, n_pages, body, 0, unroll=8)


def _wait_pages(pool_ref, pt_ref, buf_ref, sem_ref, b, slot, n_pages, rows):
    def body(p, c):
        _page_copy(pool_ref, pt_ref, buf_ref, sem_ref, b, p, slot, n_pages, rows).wait()
        return c
    lax.fori_loop(0, n_pages, body, 0, unroll=8)


def _compress_sample_kernel(pt_ref, pool_ref, hi_ref, w_ref, b_ref, perm_ref, ck_ref, cv_ref, buf_ref, sem_ref, tok_ref,
                            *, n_pages, page_rows):
    b = pl.program_id(0)
    nb = pl.num_programs(0)
    slot = lax.rem(b, 2)
    rows = 2 * LANES
    n_half = n_pages * page_rows // CMP_STRIDE

    @pl.when(b == 0)
    def _():
        _fetch_pages(pool_ref, pt_ref, buf_ref, sem_ref, b, slot, n_pages, rows)

    @pl.when(b + 1 < nb)
    def _():
        _fetch_pages(pool_ref, pt_ref, buf_ref, sem_ref, b + 1, 1 - slot, n_pages, rows)

    _wait_pages(pool_ref, pt_ref, buf_ref, sem_ref, b, slot, n_pages, rows)

    def regroup(p, carry):
        page = buf_ref[slot, pl.ds(pl.multiple_of(p * rows, rows), rows), :]
        _regroup_page(page, perm_ref, tok_ref, pl.multiple_of(p * PER_PAGE, PER_PAGE))
        return carry

    lax.fori_loop(0, n_pages, regroup, 0, unroll=8)
    last = lax.broadcasted_iota(jnp.int32, (n_half, LANES), 0) == n_half - 1
    for c, out in enumerate((ck_ref, cv_ref)):
        acc = _compress_halves(tok_ref, w_ref, c)
        hi_next = pltpu.roll(acc[:, LANES:], n_half - 1, 0)
        hi_next = jnp.where(last, hi_ref[0, c:c + 1, :], hi_next)
        out[0] = (acc[:, :LANES] + hi_next + b_ref[c:c + 1, :]).astype(BF16)


def _hi_new_kernel(ak_ref, av_ref, w_ref, o_ref):
    for c, a_ref in enumerate((ak_ref, av_ref)):
        o_ref[:, c, :] = jnp.dot(a_ref[...].astype(BF16), w_ref[c], preferred_element_type=F32)


def _hi_new(kck_new, kcv_new, w_cmp, n_batch, t_len):
    w_hi = w_cmp[:, :t_len, :, LANES:].reshape(2, t_len * LANES, LANES)
    ak = kck_new.reshape(n_batch, t_len * LANES)
    av = kcv_new.reshape(n_batch, t_len * LANES)
    return pl.pallas_call(
        _hi_new_kernel,
        out_shape=jax.ShapeDtypeStruct((n_batch, 2, LANES), F32),
        name="compress_new_tokens",
    )(ak, av, w_hi)


def _compress_sample(page_table_flat, pool, hi_new, w_cmp, b_cmp, n_batch, n_pages, page_rows):
    assert page_rows == PAGE
    perm = _regroup_perm()
    w_cmp = w_cmp.reshape(2, CMP_STRIDE // 2, 2 * LANES, 2 * LANES)
    n_half = n_pages * page_rows // CMP_STRIDE
    grid_spec = pltpu.PrefetchScalarGridSpec(
        num_scalar_prefetch=1,
        grid=(n_batch,),
        in_specs=[pl.BlockSpec(memory_space=pl.ANY),
                  pl.BlockSpec((1, 2, LANES), lambda b, pt: (b, 0, 0)),
                  pl.BlockSpec(w_cmp.shape, lambda b, pt: (0, 0, 0, 0)),
                  pl.BlockSpec(b_cmp.shape, lambda b, pt: (0, 0)),
                  pl.BlockSpec(perm.shape, lambda b, pt: (0, 0))],
        out_specs=[pl.BlockSpec((1, n_half, LANES), lambda b, pt: (b, 0, 0)),
                   pl.BlockSpec((1, n_half, LANES), lambda b, pt: (b, 0, 0))],
        scratch_shapes=[pltpu.VMEM((2, n_pages * 2 * LANES, page_rows), F32), pltpu.SemaphoreType.DMA((2,)),
                        pltpu.VMEM((2, CMP_STRIDE // 2, n_half, 2 * LANES), F32)],
    )
    return pl.pallas_call(
        functools.partial(_compress_sample_kernel, n_pages=n_pages, page_rows=page_rows),
        grid_spec=grid_spec,
        out_shape=[jax.ShapeDtypeStruct((n_batch, n_half, LANES), BF16)] * 2,
        compiler_params=_cparams("arbitrary"),
        name="compress_sample",
    )(page_table_flat, pool, hi_new, w_cmp, b_cmp, perm)


N_SEL_PAD = 256


def _sample_tables(rel_bias, past, t_len):
    rel_bias = rel_bias * LOG2E
    n_cmp = past // CMP_STRIDE
    n_sel = past // SEL_BLOCK + 1
    y = np.tile(np.arange(t_len), N_HEADS)[:, None]
    hh = np.repeat(np.arange(N_HEADS), t_len)[:, None]

    def tab(dist, valid):
        return _head_table(rel_bias, dist, valid, hh)

    dc = past + y - (np.arange(n_cmp)[None, :] * CMP_STRIDE + CMP_LEN - 1)
    biasc = tab(dc, dc >= 0)
    dp = past + y - np.arange(past)[None, :]
    biasp = tab(dp, dp >= 0)
    dn = y - np.arange(t_len)[None, :]
    biasn = tab(dn, dn >= 0)
    dw = WINDOW + y - np.arange(WINDOW)[None, :]
    biasw = tab(dw, dw < WINDOW)
    cs = np.arange(n_cmp)[:, None] * CMP_STRIDE
    ss = np.arange(N_SEL_PAD)[None, :] * SEL_BLOCK
    ov = ((cs < ss + SEL_BLOCK) & (cs + CMP_LEN > ss) & (np.arange(N_SEL_PAD)[None, :] < n_sel))
    ov4 = jnp.asarray(np.tile(ov.astype(np.float32), (GQA_R, 1)), BF16)
    expand = (np.arange(past)[None, :] // SEL_BLOCK == np.arange(past // SEL_BLOCK)[:, None])
    return biasc, biasp, biasn, biasw, ov4, jnp.asarray(expand.astype(np.float32), BF16)


def _sample_attn_kernel(pt_ref, q_ref, gt_ref, ck_ref, cv_ref, pool_ref, win_ref, ksk_ref, ksv_ref, kwk_ref,
                        kwv_ref, biasc_ref, biasp_ref, biasn_ref, biasw_ref, ov_ref, exp_ref, o_ref,
                        buf_ref, sem_ref, s_ref, *, n_pages, page_tokens, t_len, cur_block):
    b = pl.program_id(0)
    nb = pl.num_programs(0)
    slot = lax.rem(b, 2)
    rows = 2 * LANES
    nt = (((1,), (1,)), ((), ()))
    n_rows = N_HEADS * t_len

    @pl.when(b == 0)
    def _():
        _fetch_pages(pool_ref, pt_ref, buf_ref, sem_ref, b, slot, n_pages, rows)

    @pl.when(b + 1 < nb)
    def _():
        _fetch_pages(pool_ref, pt_ref, buf_ref, sem_ref, b + 1, 1 - slot, n_pages, rows)

    qb = q_ref[0]
    bc = biasc_ref[...]
    sc = lax.dot_general(qb, ck_ref[0], nt, preferred_element_type=F32) + bc
    e = jnp.exp2(sc - jnp.max(sc, axis=1, keepdims=True))
    e = jnp.where(bc > 0.5 * NEG, e, 0.0)
    lsum = jnp.sum(e, axis=1, keepdims=True)
    pb = (e / jnp.where(lsum > 0.0, lsum, 1.0)).astype(BF16)
    o_c = jnp.dot(pb, cv_ref[0], preferred_element_type=F32)
    p4 = jnp.concatenate(
        [jnp.concatenate([pb[(g * GQA_R + r) * t_len:(g * GQA_R + r + 1) * t_len, :] for r in range(GQA_R)], axis=1)
         for g in range(N_KV)], axis=0)
    imp = jnp.dot(p4, ov_ref[...], preferred_element_type=F32)
    lane = lax.broadcasted_iota(jnp.int32, imp.shape, 1)
    imp = jnp.where((lane == 0) | (lane == cur_block) | (lane == cur_block - 1), FORCE, imp)
    imp = jnp.where(lane > cur_block, NEG, imp)

    def rank_body(k, cnt):
        other = pltpu.roll(imp, k, 1)
        beats = (other > imp) | ((other == imp) & (lane >= k))
        return cnt + jnp.where(beats, 1.0, 0.0)

    cnt = lax.fori_loop(1, N_SEL_PAD, rank_body, jnp.zeros(imp.shape, F32), unroll=15)
    sel = jnp.where((cnt < N_SELECT) & (lane <= cur_block), 1.0, 0.0)
    sel_rows = jnp.concatenate([sel[g * t_len:(g + 1) * t_len, :] for g in range(N_KV) for _ in range(GQA_R)], axis=0)
    n_past_blocks = n_pages * page_tokens // SEL_BLOCK
    keep = jnp.dot(sel_rows[:, :n_past_blocks].astype(BF16), exp_ref[...], preferred_element_type=F32)

    def softmax_pair(s_old, s_new):
        m = jnp.maximum(jnp.max(s_old, axis=1, keepdims=True), jnp.max(s_new, axis=1, keepdims=True))
        p_old = jnp.exp2(s_old - m)
        p_new = jnp.exp2(s_new - m)
        inv = 1.0 / (jnp.sum(p_old, axis=1, keepdims=True) + jnp.sum(p_new, axis=1, keepdims=True))
        return p_old, p_new, inv

    _wait_pages(pool_ref, pt_ref, buf_ref, sem_ref, b, slot, n_pages, rows)
    span = 2 if n_pages % 2 == 0 else 1

    def pages(p, row0):
        return jnp.concatenate([buf_ref[slot, (p + j) * rows + row0:(p + j) * rows + row0 + LANES, :]
                                for j in range(span)], axis=1).astype(BF16)

    for p in range(0, n_pages, span):
        s_ref[:, p * page_tokens:(p + span) * page_tokens] = jnp.dot(qb, pages(p, 0), preferred_element_type=F32)
    s_past = jnp.where(keep > 0.5, s_ref[...] + biasp_ref[...], NEG)
    bn = biasn_ref[...]
    s_new = lax.dot_general(qb, ksk_ref[...].astype(BF16), nt, preferred_element_type=F32) + bn
    p_past, p_new, inv = softmax_pair(s_past, s_new)
    p_past = p_past.astype(BF16)
    o_s = jnp.dot(p_new.astype(BF16), ksv_ref[...].astype(BF16), preferred_element_type=F32)
    for p in range(0, n_pages, span):
        o_s = o_s + lax.dot_general(p_past[:, p * page_tokens:(p + span) * page_tokens], pages(p, LANES), nt,
                                    preferred_element_type=F32)
    o_s = o_s * inv
    s_win = jnp.dot(qb, win_ref[0, :LANES, :].astype(BF16), preferred_element_type=F32) + biasw_ref[...]
    s_wnew = lax.dot_general(qb, kwk_ref[...].astype(BF16), nt, preferred_element_type=F32) + bn
    p_win, p_wnew, inv_w = softmax_pair(s_win, s_wnew)
    o_w = (lax.dot_general(p_win.astype(BF16), win_ref[0, LANES:, :].astype(BF16), nt, preferred_element_type=F32)
           + jnp.dot(p_wnew.astype(BF16), kwv_ref[...].astype(BF16), preferred_element_type=F32)) * inv_w
    gs = jax.nn.sigmoid(gt_ref[0])
    o = gs[:, 0:1] * o_c + gs[:, 1:2] * o_s + gs[:, 2:3] * o_w
    row = lax.broadcasted_iota(jnp.int32, (n_rows, HEAD_DIM), 0)
    o_ref[0] = jnp.where(row < GQA_R * t_len, o[:, :HEAD_DIM], o[:, HEAD_DIM:])


def _sample_attn(page_table_flat, q, gt, ck, cv, pool, win, ksk, ksv, kwk, kwv, tables, n_batch, n_pages, page_tokens,
                 t_len):
    biasc, biasp, biasn, biasw, ov4, expand = tables
    past = n_pages * page_tokens
    n_rows = N_HEADS * t_len
    cur_block = past // SEL_BLOCK
    assert past % SEL_BLOCK == 0 and t_len <= SEL_BLOCK and cur_block < N_SEL_PAD
    new_spec = pl.BlockSpec((t_len, LANES), lambda b, pt: (b, 0))
    cst = lambda a: pl.BlockSpec(a.shape, lambda b, pt: (0,) * a.ndim)
    grid_spec = pltpu.PrefetchScalarGridSpec(
        num_scalar_prefetch=1,
        grid=(n_batch,),
        in_specs=[pl.BlockSpec((1, n_rows, LANES), lambda b, pt: (b, 0, 0)),
                  pl.BlockSpec((1, n_rows, 3), lambda b, pt: (b, 0, 0)),
                  pl.BlockSpec((1, past // CMP_STRIDE, LANES), lambda b, pt: (b, 0, 0)),
                  pl.BlockSpec((1, past // CMP_STRIDE, LANES), lambda b, pt: (b, 0, 0)),
                  pl.BlockSpec(memory_space=pl.ANY),
                  pl.BlockSpec((1, 2 * LANES, WINDOW), lambda b, pt: (b, 0, 0)),
                  new_spec, new_spec, new_spec, new_spec,
                  cst(biasc), cst(biasp), cst(biasn), cst(biasw), cst(ov4), cst(expand)],
        out_specs=pl.BlockSpec((1, n_rows, HEAD_DIM), lambda b, pt: (b, 0, 0)),
        scratch_shapes=[pltpu.VMEM((2, n_pages * 2 * LANES, page_tokens), F32), pltpu.SemaphoreType.DMA((2,)),
                        pltpu.VMEM((n_rows, past), F32)],
    )
    return pl.pallas_call(
        functools.partial(_sample_attn_kernel, n_pages=n_pages, page_tokens=page_tokens, t_len=t_len,
                          cur_block=cur_block),
        grid_spec=grid_spec,
        out_shape=jax.ShapeDtypeStruct((n_batch, n_rows, HEAD_DIM), F32),
        compiler_params=_cparams("arbitrary"),
        name="sample_attn",
    )(page_table_flat, q, gt, ck, cv, pool, win, ksk, ksv, kwk, kwv, biasc, biasp, biasn, biasw, ov4, expand)


def kernel(x_prompt, x_sample, mem_prompt, cache_cmp_kv, cache_slc_kv, cache_win_kv, cache_mem_kv, page_table, norm_mix, w_in, sgu_norm_g, sgu_norm_b, sgu_w, sgu_b, cmp_w, cmp_b, w_o, rel_bias, norm_x, norm_mem, w_xq, w_mem_kv, w_xo, norm_ffn, w_router, b_router, w_gu, b_gu, w_down, b_down, final_norm):
    assert norm_mix.shape[0] == 1, "single-layer trunk"
    bsz, t_p, _ = x_prompt.shape
    dbs, t_s, _ = x_sample.shape
    n_phys, page_tokens = cache_cmp_kv.shape[1], cache_cmp_kv.shape[2]
    n_pages = page_table.shape[1]
    past = n_pages * page_tokens
    wb = cache_win_kv.shape[2]
    assert wb == WINDOW and t_p >= WINDOW and t_p >= CHUNK and t_s <= CHUNK
    n_p, n_s = bsz * t_p, dbs * t_s
    row1 = lambda v: v.reshape(1, -1)

    w_all = _build_w_in(w_in[0])
    mix_p, mixb_p = _build_sgu_mix(sgu_w[0], sgu_b[0], CHUNK, CHUNK)
    mix_s, mixb_s = _build_sgu_mix(sgu_w[0], sgu_b[0], t_s, ROW_TILE)
    w_cmp = _build_cmp_w(cmp_w[0])
    b_cmp = cmp_b[0].reshape(2, LANES)
    w_oa, w_ob = w_o[0, :SGU_W].astype(BF16), w_o[0, SGU_W:].astype(BF16)
    w_xq_b, w_xo_b = w_xq[0].astype(BF16), w_xo[0].astype(BF16)
    w_rt = w_router[0].T.astype(BF16)
    b_rt = b_router[0].reshape(N_EXPERTS, 1)
    in_args = (row1(norm_mix[0]), w_all, row1(sgu_norm_g[0]), row1(sgu_norm_b[0]))
    tail_args = (w_xo_b, row1(norm_ffn[0]), w_rt, b_rt)

    xp = x_prompt.reshape(n_p, D_MODEL)
    (a_p, v_p, q_p, gt_p, kct, kst, kwt, kskb, ksvt, kwkb, kwvt) = _inproj(
        xp, *in_args, mix_p, mixb_p, t_p, True)
    ck_p, cvt_p = _compress_prompt(kct, w_cmp, b_cmp, bsz, t_p)
    b_p = _prompt_attn(q_p.reshape(bsz, t_p, N_HEADS * LANES), gt_p.reshape(bsz, t_p, LANES), ck_p, cvt_p,
                       kskb, ksvt, kwkb, kwvt, _attn_tables(rel_bias, t_p), bsz, t_p)
    x1_p, qm_p = _oproj(xp, a_p, b_p.reshape(n_p, NSA_W), w_oa, w_ob, row1(norm_x[0]), w_xq_b)
    memkv_f, memkv_b = _memkv(mem_prompt.reshape(bsz * MEM_LEN, D_MODEL), row1(norm_mem[0]), w_mem_kv[0].astype(BF16))
    x2_p, h_p, te_p, tg_p = _xattn(x1_p, qm_p, memkv_b, *tail_args, t_p, True)

    xs = x_sample.reshape(n_s, D_MODEL)
    (a_s, v_s, q_s, gt_s, kck_s, kcv_s, ksk_s, ksv_s, kwk_s, kwv_s) = _inproj(
        xs, *in_args, mix_s, mixb_s, t_s, False)
    pt_flat = page_table.reshape(-1)
    stored = lambda c: c[0].transpose(0, 2, 3, 4, 1).reshape(c.shape[1], 2 * LANES, c.shape[2])
    pool_c, pool_s = stored(cache_cmp_kv), stored(cache_slc_kv)
    win_t = cache_win_kv[0].transpose(0, 2, 3, 4, 1).reshape(dbs, 2 * LANES, wb)
    hi_new = _hi_new(kck_s, kcv_s, w_cmp, dbs, t_s)
    ck_s, cv_s = _compress_sample(pt_flat, pool_c, hi_new, w_cmp, b_cmp, dbs, n_pages, page_tokens)
    q_sb = q_s.reshape(dbs, t_s, N_HEADS, LANES).transpose(0, 2, 1, 3).reshape(dbs, N_HEADS * t_s, LANES)
    gt_sb = gt_s[:, :3 * N_HEADS].reshape(dbs, t_s, 3, N_HEADS).transpose(0, 3, 1, 2).reshape(dbs, N_HEADS * t_s, 3)
    bo_s = _sample_attn(pt_flat, q_sb, gt_sb, ck_s, cv_s, pool_s, win_t, ksk_s, ksv_s, kwk_s, kwv_s,
                        _sample_tables(rel_bias, past, t_s), dbs, n_pages, page_tokens, t_s)
    b_s = bo_s.reshape(dbs, N_HEADS, t_s, HEAD_DIM).transpose(0, 2, 1, 3).reshape(n_s, NSA_W).astype(BF16)
    x1_s, qm_s = _oproj(xs, a_s, b_s, w_oa, w_ob, row1(norm_x[0]), w_xq_b)
    memkv_s = cache_mem_kv[0].reshape(dbs, MEM_LEN * 2 * MEM_HEADS, MEM_HD)
    x2_s, h_s, te_s, tg_s = _xattn(x1_s, qm_s, memkv_s, *tail_args, t_s, False)

    row_tok, pos, items = _moe_dispatch(jnp.concatenate([te_p, te_s], axis=1))
    xs_sorted = jnp.concatenate([h_p, h_s], axis=0)[row_tok]
    y = _experts(items, xs_sorted, w_gu[0], b_gu[0][:, None, :], w_down[0], b_down[0][:, None, :])
    fn = row1(final_norm)
    y_prompt = _final(x2_p, y[pos[:, :n_p]], tg_p.T, fn).reshape(bsz, t_p, D_MODEL)
    y_sample = _final(x2_s, y[pos[:, n_p:]], tg_s.T, fn).reshape(dbs, t_s, D_MODEL)

    kv6 = lambda k, v, b, t: jnp.concatenate([k, v], axis=1).reshape(1, b, t, 2, N_KV, HEAD_DIM)
    unstored = lambda a: a.reshape(1, a.shape[0], 2, N_KV, HEAD_DIM, a.shape[2]).transpose(0, 1, 5, 2, 3, 4)
    kw_s = kv6(kwk_s, kwv_s, dbs, t_s)
    win_s = jnp.concatenate([cache_win_kv, kw_s], axis=2)[:, :, -wb:]
    return (y_prompt, y_sample, unstored(kct), unstored(kst), unstored(kwt[:, :, t_p - WINDOW:]),
            memkv_f.reshape(1, bsz, MEM_LEN, 2, MEM_HEADS, MEM_HD), v_p.reshape(1, bsz, t_p, SGU_W)[:, :, -CHUNK:],
            kv6(kck_s, kcv_s, dbs, t_s), kv6(ksk_s, ksv_s, dbs, t_s), win_s, v_s.reshape(1, dbs, t_s, SGU_W))
```
